```python
import jax, jax.numpy as jnp
from jax import lax
import numpy as np

D_MODEL = 2048
BATCH = 2
SEQ = 4096
DEPTH = 2

N_A_LAYERS = DEPTH // 2
N_B_LAYERS = DEPTH - N_A_LAYERS
ALPHA = (2.0 * DEPTH) ** 0.25
BETA = (8.0 * DEPTH) ** -0.25
POOL_WINDOWS = (2, 4, 8, 16)
N_POOL_GROUPS = 4
POOL_GROUP_DIM = D_MODEL // N_POOL_GROUPS
N_HEADS = 16
QK_NOPE_DIM = 128
QK_ROPE_DIM = 64
QK_DIM = QK_NOPE_DIM + QK_ROPE_DIM
V_DIM = 128
Q_LORA_RANK = D_MODEL // 4
KV_LORA_RANK = D_MODEL // 4
ROPE_THETA = 10000.0
Q_BLOCK = 128
N_EXPERTS = 64
TOP_K = 8
N_EXPERT_GROUPS = 8
TOPK_GROUPS = 4
EXPERT_DIM = D_MODEL // 4
SHARED_DIM = EXPERT_DIM
ROUTED_SCALE = 2.5
ROW_BLOCK = 128
LN_EPS = 1e-5
RMS_EPS = 1e-6

kernel_name = 'yoco_pool_mla_moe_deepnorm'


def layer_norm(x, g, b):
    xf = x.astype(jnp.float32)
    mu = jnp.mean(xf, axis=-1, keepdims=True)
    var = jnp.mean(jnp.square(xf - mu), axis=-1, keepdims=True)
    return ((xf - mu) * lax.rsqrt(var + LN_EPS) * g + b).astype(x.dtype)


def rms_norm(x, g):
    xf = x.astype(jnp.float32)
    ms = jnp.mean(jnp.square(xf), axis=-1, keepdims=True)
    return (xf * lax.rsqrt(ms + RMS_EPS) * g).astype(x.dtype)


def apply_rope(x):
    S = x.shape[1]
    half = QK_ROPE_DIM // 2
    inv_freq = 1.0 / (ROPE_THETA ** (jnp.arange(0, QK_ROPE_DIM, 2, dtype=jnp.float32) / QK_ROPE_DIM))
    ang = jnp.arange(S, dtype=jnp.float32)[:, None] * inv_freq[None, :]
    bshape = (S,) + (1,) * (x.ndim - 3) + (half,)
    cos = jnp.cos(ang).reshape(bshape)
    sin = jnp.sin(ang).reshape(bshape)
    xf = x.astype(jnp.float32)
    x1, x2 = xf[..., :half], xf[..., half:]
    return jnp.concatenate([x1 * cos - x2 * sin, x2 * cos + x1 * sin], axis=-1).astype(x.dtype)


def pool_mixer(x, w_in, w_grp, scale, w_out):
    B, S, _ = x.shape
    h = (x @ w_in).reshape(B, S, N_POOL_GROUPS, POOL_GROUP_DIM)
    hf = h.astype(jnp.float32)
    csum = jnp.concatenate([jnp.zeros((B, 1, N_POOL_GROUPS, POOL_GROUP_DIM), jnp.float32),
                            jnp.cumsum(hf, axis=1)], axis=1)
    t = jnp.arange(S, dtype=jnp.float32)
    outs = []
    for g, w in enumerate(POOL_WINDOWS):
        c = csum[:, :, g]
        lo = jnp.concatenate([jnp.zeros((B, w - 1, POOL_GROUP_DIM), jnp.float32), c[:, :S - w + 1]], axis=1)
        count = jnp.minimum(t + 1.0, float(w))[None, :, None]
        outs.append((c[:, 1:] - lo) / count - hf[:, :, g])
    pooled = jnp.stack(outs, axis=2).astype(x.dtype)
    mixed = jnp.einsum('bsgc,gcd->bsgd', pooled, w_grp).reshape(B, S, D_MODEL)
    return (mixed * scale) @ w_out


def shared_kv(x, w_down, norm_g, w_uk, w_uv):
    B, S, _ = x.shape
    ckr = x @ w_down
    c = rms_norm(ckr[..., :KV_LORA_RANK], norm_g)
    k_rope = apply_rope(ckr[..., KV_LORA_RANK:])
    k_nope = jnp.einsum('bsr,rhd->bshd', c, w_uk)
    v = jnp.einsum('bsr,rhd->bshd', c, w_uv)
    k = jnp.concatenate([k_nope, jnp.broadcast_to(k_rope[:, :, None, :], (B, S, N_HEADS, QK_ROPE_DIM))], axis=-1)
    return k, v


def causal_block_attention(q, k, v):
    B, S, H, Dq = q.shape
    nb = S // Q_BLOCK
    qb = jnp.moveaxis(q.reshape(B, nb, Q_BLOCK, H, Dq), 1, 0)
    key_pos = jnp.arange(S)
    sm_scale = QK_DIM ** -0.5

    def one_block(args):
        i, qi = args
        s = jnp.einsum('bqhd,bkhd->bhqk', qi, k, preferred_element_type=jnp.float32) * sm_scale
        q_pos = i * Q_BLOCK + jnp.arange(Q_BLOCK)
        s = jnp.where(key_pos[None, :] <= q_pos[:, None], s, -jnp.inf)
        p = jax.nn.softmax(s, axis=-1).astype(v.dtype)
        return jnp.einsum('bhqk,bkhd->bqhd', p, v)

    out = lax.map(one_block, (jnp.arange(nb), qb))
    return jnp.moveaxis(out, 0, 1).reshape(B, S, H * V_DIM)


def mla_mixer(x, k, v, w_dq, q_norm, w_uq, w_out):
    cq = rms_norm(x @ w_dq, q_norm)
    q = jnp.einsum('bsr,rhd->bshd', cq, w_uq)
    q = jnp.concatenate([q[..., :QK_NOPE_DIM], apply_rope(q[..., QK_NOPE_DIM:])], axis=-1)
    return causal_block_attention(q, k, v) @ w_out


def swiglu(x, w_gate, w_up, w_down):
    return (jax.nn.silu(x @ w_gate) * (x @ w_up)) @ w_down


def moe_ffn(x, router_w, router_bias, w_gate, w_up, w_down, s_gate, s_up, s_down):
    B, S, D = x.shape
    T = B * S
    xf = x.reshape(T, D)
    scores = jax.nn.sigmoid((xf @ router_w).astype(jnp.float32))
    choice = scores + router_bias.astype(jnp.float32)
    grp = choice.reshape(T, N_EXPERT_GROUPS, N_EXPERTS // N_EXPERT_GROUPS)
    grp_score = jnp.sum(lax.top_k(grp, 2)[0], axis=-1)
    _, gidx = lax.top_k(grp_score, TOPK_GROUPS)
    gmask = jnp.any(gidx[:, :, None] == jnp.arange(N_EXPERT_GROUPS)[None, None, :], axis=1)
    emask = jnp.repeat(gmask, N_EXPERTS // N_EXPERT_GROUPS, axis=1)
    _, eidx = lax.top_k(jnp.where(emask, choice, -jnp.inf), TOP_K)
    gate = jnp.take_along_axis(scores, eidx, axis=1)
    gate = gate / jnp.sum(gate, axis=-1, keepdims=True) * ROUTED_SCALE

    A = T * TOP_K
    e_flat = eidx.reshape(-1)
    g_flat = gate.reshape(-1).astype(x.dtype)
    tok_flat = jnp.repeat(jnp.arange(T, dtype=jnp.int32), TOP_K)
    order = jnp.argsort(e_flat, stable=True)
    e_sorted, tok_sorted, g_sorted = e_flat[order], tok_flat[order], g_flat[order]
    counts = jnp.bincount(e_flat, length=N_EXPERTS)
    start = jnp.cumsum(counts) - counts
    padded = (counts + ROW_BLOCK - 1) // ROW_BLOCK * ROW_BLOCK
    pad_end = jnp.cumsum(padded)
    pad_start = pad_end - padded
    dest = pad_start[e_sorted] + (jnp.arange(A) - start[e_sorted])
    n_blocks = (A + ROW_BLOCK - 1) // ROW_BLOCK + N_EXPERTS
    P = n_blocks * ROW_BLOCK
    tok_pad = jnp.full((P,), T, dtype=jnp.int32).at[dest].set(tok_sorted)
    g_pad = jnp.zeros((P,), x.dtype).at[dest].set(g_sorted)
    blk_expert = jnp.minimum(jnp.searchsorted(pad_end, jnp.arange(n_blocks) * ROW_BLOCK, side='right'),
                             N_EXPERTS - 1)
    x_pad = jnp.concatenate([xf, jnp.zeros((1, D), x.dtype)], axis=0)

    def expert_block(args):
        tok_b, g_b, e = args
        xi = x_pad[tok_b]
        return swiglu(xi, w_gate[e], w_up[e], w_down[e]) * g_b[:, None]

    yb = lax.map(expert_block, (tok_pad.reshape(n_blocks, ROW_BLOCK), g_pad.reshape(n_blocks, ROW_BLOCK), blk_expert))
    routed = jnp.zeros((T + 1, D), x.dtype).at[tok_pad].add(yb.reshape(P, D))[:T]
    shared = swiglu(xf, s_gate, s_up, s_down)
    return (routed + shared).reshape(B, S, D)


def setup_inputs(seed: int = 0) -> dict:
    key = jax.random.key(seed)
    ks = jax.random.split(key, 24)
    f32 = jnp.float32

    def nrm(k, shape, fan_in, scale=1.0):
        return jax.random.normal(k, shape, f32) * (scale * fan_in ** -0.5)

    def gain(k, shape):
        return 1.0 + 0.02 * jax.random.normal(k, shape, f32)

    def bias(k, shape):
        return 0.02 * jax.random.normal(k, shape, f32)

    return {
        'x': jax.random.normal(ks[0], (BATCH, SEQ, D_MODEL), f32),
        'a_w_in': nrm(ks[1], (N_A_LAYERS, D_MODEL, D_MODEL), D_MODEL),
        'a_w_grp': nrm(ks[2], (N_A_LAYERS, N_POOL_GROUPS, POOL_GROUP_DIM, POOL_GROUP_DIM), POOL_GROUP_DIM),
        'a_scale': gain(ks[3], (N_A_LAYERS, D_MODEL)),
        'a_w_out': nrm(ks[4], (N_A_LAYERS, D_MODEL, D_MODEL), D_MODEL, BETA),
        'kv_w_down': nrm(ks[5], (D_MODEL, KV_LORA_RANK + QK_ROPE_DIM), D_MODEL),
        'kv_norm': gain(ks[6], (KV_LORA_RANK,)),
        'kv_w_uk': nrm(ks[7], (KV_LORA_RANK, N_HEADS, QK_NOPE_DIM), KV_LORA_RANK),
        'kv_w_uv': nrm(ks[8], (KV_LORA_RANK, N_HEADS, V_DIM), KV_LORA_RANK),
        'b_w_dq': nrm(ks[9], (N_B_LAYERS, D_MODEL, Q_LORA_RANK), D_MODEL),
        'b_q_norm': gain(ks[10], (N_B_LAYERS, Q_LORA_RANK)),
        'b_w_uq': nrm(ks[11], (N_B_LAYERS, Q_LORA_RANK, N_HEADS, QK_DIM), Q_LORA_RANK),
        'b_w_out': nrm(ks[12], (N_B_LAYERS, N_HEADS * V_DIM, D_MODEL), N_HEADS * V_DIM, BETA),
        'ln1_g': gain(ks[13], (DEPTH, D_MODEL)),
        'ln1_b': bias(ks[14], (DEPTH, D_MODEL)),
        'ln2_g': gain(ks[15], (DEPTH, D_MODEL)),
        'ln2_b': bias(ks[16], (DEPTH, D_MODEL)),
        'router_w': nrm(ks[17], (DEPTH, D_MODEL, N_EXPERTS), D_MODEL),
        'router_bias': 0.01 * jax.random.normal(ks[18], (DEPTH, N_EXPERTS), f32),
        'exp_w_gate': nrm(ks[19], (DEPTH, N_EXPERTS, D_MODEL, EXPERT_DIM), D_MODEL),
        'exp_w_up': nrm(ks[20], (DEPTH, N_EXPERTS, D_MODEL, EXPERT_DIM), D_MODEL),
        'exp_w_down': nrm(ks[21], (DEPTH, N_EXPERTS, EXPERT_DIM, D_MODEL), EXPERT_DIM, BETA),
        'sh_w_gate': nrm(ks[22], (DEPTH, D_MODEL, SHARED_DIM), D_MODEL),
        'sh_w_up': nrm(jax.random.fold_in(ks[22], 1), (DEPTH, D_MODEL, SHARED_DIM), D_MODEL),
        'sh_w_down': nrm(ks[23], (DEPTH, SHARED_DIM, D_MODEL), SHARED_DIM, BETA),
    }


def reference(x, a_w_in, a_w_grp, a_scale, a_w_out, kv_w_down, kv_norm, kv_w_uk, kv_w_uv,
              b_w_dq, b_q_norm, b_w_uq, b_w_out, ln1_g, ln1_b, ln2_g, ln2_b,
              router_w, router_bias, exp_w_gate, exp_w_up, exp_w_down,
              sh_w_gate, sh_w_up, sh_w_down):
    k_sh, v_sh = None, None
    for l in range(DEPTH):
        if l < N_A_LAYERS:
            y = pool_mixer(x, a_w_in[l], a_w_grp[l], a_scale[l], a_w_out[l])
        else:
            if l == N_A_LAYERS:
                k_sh, v_sh = shared_kv(x, kv_w_down, kv_norm, kv_w_uk, kv_w_uv)
            j = l - N_A_LAYERS
            y = mla_mixer(x, k_sh, v_sh, b_w_dq[j], b_q_norm[j], b_w_uq[j], b_w_out[j])
        x = layer_norm(ALPHA * x + y, ln1_g[l], ln1_b[l])
        f = moe_ffn(x, router_w[l], router_bias[l], exp_w_gate[l], exp_w_up[l], exp_w_down[l],
                    sh_w_gate[l], sh_w_up[l], sh_w_down[l])
        x = layer_norm(ALPHA * x + f, ln2_g[l], ln2_b[l])
    return x
```

```python
import functools

import jax
import jax.numpy as jnp
from jax import lax
from jax.experimental import pallas as pl
from jax.experimental.pallas import tpu as pltpu

D_MODEL = 2048
BATCH = 2
SEQ = 4096
DEPTH = 2
N_TOK = BATCH * SEQ
ALPHA = (2.0 * DEPTH) ** 0.25
POOL_WINDOWS = (2, 4, 8, 16)
POOL_GROUP_DIM = D_MODEL // len(POOL_WINDOWS)
POOL_HALO = 16
N_HEADS = 16
QK_NOPE_DIM = 128
QK_ROPE_DIM = 64
QK_DIM = QK_NOPE_DIM + QK_ROPE_DIM
V_DIM = 128
Q_LORA_RANK = D_MODEL // 4
KV_LORA_RANK = D_MODEL // 4
ROPE_THETA = 10000.0
N_EXPERTS = 64
TOP_K = 8
N_EXPERT_GROUPS = 8
GROUP_SIZE = N_EXPERTS // N_EXPERT_GROUPS
TOPK_GROUPS = 4
EXPERT_DIM = D_MODEL // 4
ROUTED_SCALE = 2.5
ROW_BLOCK = 128
LN_EPS = 1e-5
RMS_EPS = 1e-6

N_ASSIGN = N_TOK * TOP_K
N_BLOCKS = N_ASSIGN // ROW_BLOCK + N_EXPERTS
N_SRT = (N_BLOCKS + 2) * ROW_BLOCK
Y_ROWS = N_TOK + 2 * ROW_BLOCK
TOK_BITS = 14
assert N_TOK <= 1 << TOK_BITS and TOP_K * Y_ROWS < 1 << (31 - TOK_BITS)
SLAB_ROWS = 8
SLAB_LANES = 128
HALF_D = D_MODEL // 2
VMEM_LIMIT = 60 * 1024 * 1024

F32 = jnp.float32
BF16 = jnp.bfloat16
I32 = jnp.int32
U32 = jnp.uint32
NT_DIMS = (((1,), (1,)), ((), ()))


def _pack_rows(z):
    return pltpu.pack_elementwise([z[:, :HALF_D], z[:, HALF_D:]], packed_dtype=BF16)


def _unpack_lo(w):
    return pltpu.unpack_elementwise(w, index=0, packed_dtype=BF16, unpacked_dtype=F32)


def _unpack_hi(w):
    return pltpu.unpack_elementwise(w, index=1, packed_dtype=BF16, unpacked_dtype=F32)


def _silu(x):
    return x * jax.nn.sigmoid(x)


def _layer_norm(z, g, b):
    mu = jnp.mean(z, axis=-1, keepdims=True)
    zc = z - mu
    var = jnp.mean(zc * zc, axis=-1, keepdims=True)
    return zc * lax.rsqrt(var + LN_EPS) * g + b


def _rms_norm(z, g):
    ms = jnp.mean(z * z, axis=-1, keepdims=True)
    return z * lax.rsqrt(ms + RMS_EPS) * g


def _resident(shape):
    nd = len(shape)
    return pl.BlockSpec(shape, lambda *_: (0,) * nd, pipeline_mode=pl.Buffered(1))


def _pool_in_kernel(x_ref, w_ref, o_ref, tail_ref, *, tm):
    j = pl.program_id(1)

    @pl.when(j == 0)
    def _():
        tail_ref[...] = jnp.zeros_like(tail_ref)

    h = jnp.dot(x_ref[...].astype(BF16), w_ref[...], preferred_element_type=F32)
    ext = jnp.concatenate([tail_ref[...], h], axis=0)
    tail_ref[...] = h[tm - POOL_HALO:, :]
    pos = j * tm + lax.broadcasted_iota(I32, (tm, 1), 0)
    for g, w in enumerate(POOL_WINDOWS):
        c0, c1 = g * POOL_GROUP_DIM, (g + 1) * POOL_GROUP_DIM
        s = ext[:, c0:c1]
        sh = 1
        while sh < w:
            s = s + pltpu.roll(s, sh, axis=0)
            sh *= 2
        inv = 1.0 / jnp.minimum(pos + 1, w).astype(F32)
        o_ref[:, c0:c1] = (s[POOL_HALO:, :] * inv - h[:, c0:c1]).astype(BF16)


def _pool_in(x2d, w_in, *, tm=512):
    nj = SEQ // tm
    return pl.pallas_call(
        functools.partial(_pool_in_kernel, tm=tm),
        grid=(BATCH, nj),
        in_specs=[pl.BlockSpec((tm, D_MODEL), lambda b, j: (b * nj + j, 0)),
                  _resident((D_MODEL, D_MODEL))],
        out_specs=pl.BlockSpec((tm, D_MODEL), lambda b, j: (b * nj + j, 0)),
        out_shape=jax.ShapeDtypeStruct((N_TOK, D_MODEL), BF16),
        scratch_shapes=[pltpu.VMEM((POOL_HALO, D_MODEL), F32)],
        compiler_params=pltpu.CompilerParams(
            dimension_semantics=("arbitrary", "arbitrary"), vmem_limit_bytes=VMEM_LIMIT),
        name="pool_in",
    )(x2d, w_in)


def _ln1_epilogue(z, g_ref, b_ref, rwt_ref, xo_ref, slab_ref, lg_ref, tm):
    xn = _layer_norm(z, g_ref[...], b_ref[...])
    xo_ref[...] = xn
    pk = _pack_rows(xn)
    for c in range(SLAB_ROWS):
        slab_ref[pl.ds(c, tm, stride=SLAB_ROWS), :] = pk[:, c * SLAB_LANES:(c + 1) * SLAB_LANES]
    lg_ref[...] = lax.dot_general(rwt_ref[...], xn, NT_DIMS, precision=lax.Precision.HIGHEST,
                                  preferred_element_type=F32)


def _mix_out_kernel(p_ref, x_ref, wg_ref, sc_ref, wo_ref, g_ref, b_ref, rwt_ref,
                    xo_ref, slab_ref, lg_ref, *, tm):
    parts = []
    for g in range(len(POOL_WINDOWS)):
        c0, c1 = g * POOL_GROUP_DIM, (g + 1) * POOL_GROUP_DIM
        parts.append(jnp.dot(p_ref[:, c0:c1], wg_ref[g], preferred_element_type=F32))
    mixed = (jnp.concatenate(parts, axis=1) * sc_ref[...]).astype(BF16)
    y = jnp.dot(mixed, wo_ref[...], preferred_element_type=F32)
    _ln1_epilogue(ALPHA * x_ref[...] + y, g_ref, b_ref, rwt_ref, xo_ref, slab_ref, lg_ref, tm)


def _ln1_out_specs(tm):
    return (
        [pl.BlockSpec((tm, D_MODEL), lambda i: (i, 0)),
         pl.BlockSpec((tm * SLAB_ROWS, SLAB_LANES), lambda i: (i, 0)),
         pl.BlockSpec((N_EXPERTS, tm), lambda i: (0, i))],
        [jax.ShapeDtypeStruct((N_TOK, D_MODEL), F32),
         jax.ShapeDtypeStruct((N_TOK * SLAB_ROWS, SLAB_LANES), U32),
         jax.ShapeDtypeStruct((N_EXPERTS, N_TOK), F32)],
    )


def _mix_out(pooled, x2d, w_grp, scale, w_out, ln_g, ln_b, rwt, *, tm=256):
    out_specs, out_shape = _ln1_out_specs(tm)
    return pl.pallas_call(
        functools.partial(_mix_out_kernel, tm=tm),
        grid=(N_TOK // tm,),
        in_specs=[pl.BlockSpec((tm, D_MODEL), lambda i: (i, 0)),
                  pl.BlockSpec((tm, D_MODEL), lambda i: (i, 0)),
                  _resident(w_grp.shape), _resident((1, D_MODEL)), _resident((D_MODEL, D_MODEL)),
                  _resident((1, D_MODEL)), _resident((1, D_MODEL)), _resident((N_EXPERTS, D_MODEL))],
        out_specs=out_specs, out_shape=out_shape,
        compiler_params=pltpu.CompilerParams(
            dimension_semantics=("arbitrary",), vmem_limit_bytes=VMEM_LIMIT),
        name="mix_out",
    )(pooled, x2d, w_grp, scale, w_out, ln_g, ln_b, rwt)


def _first_index(hit_src, best, iota, n):
    return jnp.min(jnp.where(hit_src == best, iota, n), axis=0, keepdims=True)


def _route_kernel(lg_ref, bias_ref, pk_ref, gate_ref, cnt_ref, carry_ref, *, tm):
    i = pl.program_id(0)

    @pl.when(i == 0)
    def _():
        carry_ref[...] = jnp.zeros_like(carry_ref)

    neg = -jnp.inf
    scores = jax.nn.sigmoid(lg_ref[...])
    choice = scores + bias_ref[...]
    iota_g = lax.broadcasted_iota(I32, (GROUP_SIZE, tm), 0)
    gscore = []
    for g in range(N_EXPERT_GROUPS):
        c = choice[g * GROUP_SIZE:(g + 1) * GROUP_SIZE, :]
        m1 = jnp.max(c, axis=0, keepdims=True)
        f1 = _first_index(c, m1, iota_g, GROUP_SIZE)
        m2 = jnp.max(jnp.where(iota_g == f1, neg, c), axis=0, keepdims=True)
        gscore.append(m1 + m2)
    gs = jnp.concatenate(gscore, axis=0)
    iota_ng = lax.broadcasted_iota(I32, (N_EXPERT_GROUPS, tm), 0)
    gsel = jnp.zeros((N_EXPERT_GROUPS, tm), jnp.bool_)
    for _ in range(TOPK_GROUPS):
        m = jnp.max(gs, axis=0, keepdims=True)
        hit = iota_ng == _first_index(gs, m, iota_ng, N_EXPERT_GROUPS)
        gsel = gsel | hit
        gs = jnp.where(hit, neg, gs)
    masked = jnp.concatenate(
        [jnp.where(gsel[g:g + 1, :], choice[g * GROUP_SIZE:(g + 1) * GROUP_SIZE, :], neg)
         for g in range(N_EXPERT_GROUPS)], axis=0)
    iota_e = lax.broadcasted_iota(I32, (N_EXPERTS, tm), 0)
    hits, eidx, gates = [], [], []
    sel = jnp.zeros((N_EXPERTS, tm), jnp.bool_)
    for _ in range(TOP_K):
        m = jnp.max(masked, axis=0, keepdims=True)
        f = _first_index(masked, m, iota_e, N_EXPERTS)
        hit = iota_e == f
        hits.append(hit)
        eidx.append(f)
        gates.append(jnp.sum(jnp.where(hit, scores, 0.0), axis=0, keepdims=True))
        masked = jnp.where(hit, neg, masked)
        sel = sel | hit
    gate = jnp.concatenate(gates, axis=0)
    gate_ref[...] = gate / jnp.sum(gate, axis=0, keepdims=True) * ROUTED_SCALE
    selb = jnp.where(sel, 1.0, 0.0).astype(BF16)
    before = (lax.broadcasted_iota(I32, (tm, tm), 0) < lax.broadcasted_iota(I32, (tm, tm), 1))
    rank = jnp.dot(selb, jnp.where(before, 1.0, 0.0).astype(BF16), preferred_element_type=F32)
    rank = (rank + carry_ref[...]).astype(I32)
    rk = [jnp.sum(jnp.where(h, rank, 0), axis=0, keepdims=True) for h in hits]
    pk_ref[...] = (jnp.concatenate(eidx, axis=0) << 16) | jnp.concatenate(rk, axis=0)
    carry_ref[...] = carry_ref[...] + jnp.sum(selb.astype(F32), axis=1, keepdims=True)
    cnt_ref[...] = jnp.broadcast_to(carry_ref[...], cnt_ref.shape).astype(I32)


def _route(logits_t, bias, *, tm=512):
    return pl.pallas_call(
        functools.partial(_route_kernel, tm=tm),
        grid=(N_TOK // tm,),
        in_specs=[pl.BlockSpec((N_EXPERTS, tm), lambda i: (0, i)),
                  _resident((N_EXPERTS, 1))],
        out_specs=[pl.BlockSpec((TOP_K, tm), lambda i: (0, i)),
                   pl.BlockSpec((TOP_K, tm), lambda i: (0, i)),
                   pl.BlockSpec((N_EXPERTS, SLAB_LANES), lambda i: (0, 0))],
        out_shape=[jax.ShapeDtypeStruct((TOP_K, N_TOK), I32),
                   jax.ShapeDtypeStruct((TOP_K, N_TOK), F32),
                   jax.ShapeDtypeStruct((N_EXPERTS, SLAB_LANES), I32)],
        scratch_shapes=[pltpu.VMEM((N_EXPERTS, 1), F32)],
        compiler_params=pltpu.CompilerParams(dimension_semantics=("arbitrary",)),
        name="route",
    )(logits_t, bias)


def _finalize_kernel(pk_ref, cnt_ref, srt_ref, blk_ref, first_ref, nxt_ref, nused_ref, ps_ref, *, tt):
    i = pl.program_id(0)

    def pad_fill(lo, hi):
        def body(r, carry):
            srt_ref[r] = (N_TOK + lax.rem(r // ROW_BLOCK + 1, 2) * ROW_BLOCK
                          + lax.rem(r, ROW_BLOCK)) << TOK_BITS
            return carry

        lax.fori_loop(lo, hi, body, 0)

    @pl.when(i == 0)
    def _():
        def offsets(e, acc):
            ps_ref[e] = acc
            return acc + ((cnt_ref[e] + (ROW_BLOCK - 1)) // ROW_BLOCK) * ROW_BLOCK

        total = lax.fori_loop(0, N_EXPERTS, offsets, jnp.int32(ROW_BLOCK))
        nused_ref[0] = total // ROW_BLOCK - 1
        pad_fill(0, ROW_BLOCK)
        pad_fill(total, N_SRT)

        def defaults(b, c):
            blk_ref[b] = N_EXPERTS - 1
            first_ref[b] = 0
            nxt_ref[b] = -1
            return c

        lax.fori_loop(0, N_BLOCKS, defaults, 0)

        def per_expert(j, nxt_e):
            e = N_EXPERTS - 1 - j
            c = cnt_ref[e]
            nb = (c + (ROW_BLOCK - 1)) // ROW_BLOCK
            r0 = ps_ref[e]
            b0 = r0 // ROW_BLOCK - 1

            def blocks(jb, carry):
                blk_ref[b0 + jb] = e
                first_ref[b0 + jb] = jnp.where(jb == 0, 1, 0)
                nxt_ref[b0 + jb] = nxt_e
                return carry

            lax.fori_loop(0, nb, blocks, 0)
            pad_fill(r0 + c, r0 + nb * ROW_BLOCK)
            return jnp.where(c > 0, e, nxt_e)

        lax.fori_loop(0, N_EXPERTS, per_expert, jnp.int32(-1))

    def per_token(tl, carry):
        t = i * tt + tl
        for k in range(TOP_K):
            p = pk_ref[k * N_TOK + t]
            srt_ref[ps_ref[p >> 16] + (p & 0xFFFF)] = ((k * Y_ROWS + t) << TOK_BITS) | t
        return carry

    lax.fori_loop(0, tt, per_token, 0)


def _finalize(packed_flat, counts, *, tt=512):
    smem = pl.BlockSpec(memory_space=pltpu.SMEM)
    return pl.pallas_call(
        functools.partial(_finalize_kernel, tt=tt),
        grid=(N_TOK // tt,),
        in_specs=[smem, smem],
        out_specs=[smem] * 5,
        out_shape=[jax.ShapeDtypeStruct((N_SRT,), I32),
                   jax.ShapeDtypeStruct((N_BLOCKS,), I32),
                   jax.ShapeDtypeStruct((N_BLOCKS,), I32),
                   jax.ShapeDtypeStruct((N_BLOCKS,), I32),
                   jax.ShapeDtypeStruct((1,), I32)],
        scratch_shapes=[pltpu.SMEM((N_EXPERTS,), I32)],
        compiler_params=pltpu.CompilerParams(dimension_semantics=("arbitrary",)),
        name="finalize",
    )(packed_flat, counts)


BLOCK_SLAB_ROWS = ROW_BLOCK * SLAB_ROWS
BLOCKS_PER_STEP = 2
CAST_VREGS = 32


def _cast_weight(src_ref, dst_ref):
    rows, cols = src_ref.shape
    step = CAST_VREGS * 8 * 128 // cols

    def body(r, carry):
        r0 = pl.multiple_of(r * step, step)
        dst_ref[pl.ds(r0, step), :] = src_ref[pl.ds(r0, step), :].astype(BF16)
        return carry

    lax.fori_loop(0, rows // step, body, 0)


def _expert_kernel(blk_ref, first_ref, nxt_ref, nused_ref, srt_ref,
                   xs_ref, wg_hbm, wu_hbm, wd_hbm, y_hbm,
                   sg_ref, su_ref, sd_ref, bg_ref, bu_ref, bd_ref,
                   xg_ref, xb_ref, ys_ref, wsem, ysem, *, layer):
    step = pl.program_id(0)
    nused = nused_ref[0]

    def weight_copies(ee):
        return (pltpu.make_async_copy(wg_hbm.at[layer, ee], sg_ref, wsem.at[0]),
                pltpu.make_async_copy(wu_hbm.at[layer, ee], su_ref, wsem.at[1]),
                pltpu.make_async_copy(wd_hbm.at[layer, ee], sd_ref, wsem.at[2]))

    def wait_rows(slot):
        pltpu.make_async_copy(ys_ref.at[slot], y_hbm.at[pl.ds(0, BLOCK_SLAB_ROWS)],
                              ysem.at[slot]).wait()

    def gather_block(blk, slot):
        base = (blk + 1) * ROW_BLOCK
        for i in range(ROW_BLOCK):
            tok = srt_ref[base + i] & ((1 << TOK_BITS) - 1)
            xg_ref[pl.ds(i * SLAB_ROWS, SLAB_ROWS), :] = (
                xs_ref[pl.ds(pl.multiple_of(tok * SLAB_ROWS, SLAB_ROWS), SLAB_ROWS), :])
        for c in range(SLAB_ROWS):
            w = xg_ref[pl.ds(c, ROW_BLOCK, stride=SLAB_ROWS), :]
            xb_ref[slot, :, c * SLAB_LANES:(c + 1) * SLAB_LANES] = _unpack_lo(w).astype(BF16)
            xb_ref[slot, :, HALF_D + c * SLAB_LANES:HALF_D + (c + 1) * SLAB_LANES] = (
                _unpack_hi(w).astype(BF16))

    @pl.when(step == 0)
    def _():
        for cp in weight_copies(blk_ref[0]):
            cp.start()
        ys_ref[...] = jnp.zeros_like(ys_ref)
        zero_copies = [
            pltpu.make_async_copy(
                ys_ref.at[0],
                y_hbm.at[pl.ds((k * Y_ROWS + N_TOK + s * ROW_BLOCK) * SLAB_ROWS, BLOCK_SLAB_ROWS)],
                ysem.at[0])
            for k in range(TOP_K) for s in range(2)]
        for cp in zero_copies:
            cp.start()
        for cp in zero_copies:
            cp.wait()
        gather_block(0, 0)

    def one_block(b, cur):
        prev = 1 - cur

        @pl.when(b <= nused)
        def _():
            @pl.when(first_ref[b] == 1)
            def _():
                for cp in weight_copies(blk_ref[b]):
                    cp.wait()
                _cast_weight(sg_ref, bg_ref)
                _cast_weight(su_ref, bu_ref)
                _cast_weight(sd_ref, bd_ref)

                @pl.when(nxt_ref[b] >= 0)
                def _():
                    for cp in weight_copies(nxt_ref[b]):
                        cp.start()

            pbase = b * ROW_BLOCK
            for i in range(ROW_BLOCK):
                slab = srt_ref[pbase + i] >> TOK_BITS
                pltpu.make_async_copy(
                    ys_ref.at[prev, pl.ds(i * SLAB_ROWS, SLAB_ROWS)],
                    y_hbm.at[pl.ds(pl.multiple_of(slab * SLAB_ROWS, SLAB_ROWS), SLAB_ROWS)],
                    ysem.at[prev]).start()

            x = xb_ref[cur]
            hg = jnp.dot(x, bg_ref[...], preferred_element_type=F32)
            hu = jnp.dot(x, bu_ref[...], preferred_element_type=F32)

            @pl.when(b > 0)
            def _():
                wait_rows(cur)

            gather_block(b + 1, prev)
            act = (_silu(hg) * hu).astype(BF16)
            pk = _pack_rows(jnp.dot(act, bd_ref[...], preferred_element_type=F32))
            for c in range(SLAB_ROWS):
                ys_ref[cur, pl.ds(c, ROW_BLOCK, stride=SLAB_ROWS), :] = (
                    pk[:, c * SLAB_LANES:(c + 1) * SLAB_LANES])

            @pl.when(b == nused)
            def _():
                wait_rows(prev)

    for j in range(BLOCKS_PER_STEP):
        one_block(step * BLOCKS_PER_STEP + j, j)


def _experts(blk, first, nxt, nused, srt, slabs, w_gate, w_up, w_down, *, layer):
    grid_spec = pltpu.PrefetchScalarGridSpec(
        num_scalar_prefetch=5,
        grid=(N_BLOCKS // BLOCKS_PER_STEP,),
        in_specs=[pl.BlockSpec((N_TOK * SLAB_ROWS, SLAB_LANES), lambda b, *_: (0, 0),
                               pipeline_mode=pl.Buffered(1)),
                  pl.BlockSpec(memory_space=pl.ANY),
                  pl.BlockSpec(memory_space=pl.ANY),
                  pl.BlockSpec(memory_space=pl.ANY)],
        out_specs=pl.BlockSpec(memory_space=pl.ANY),
        scratch_shapes=[pltpu.VMEM((D_MODEL, EXPERT_DIM), F32),
                        pltpu.VMEM((D_MODEL, EXPERT_DIM), F32),
                        pltpu.VMEM((EXPERT_DIM, D_MODEL), F32),
                        pltpu.VMEM((D_MODEL, EXPERT_DIM), BF16),
                        pltpu.VMEM((D_MODEL, EXPERT_DIM), BF16),
                        pltpu.VMEM((EXPERT_DIM, D_MODEL), BF16),
                        pltpu.VMEM((BLOCK_SLAB_ROWS, SLAB_LANES), U32),
                        pltpu.VMEM((2, ROW_BLOCK, D_MODEL), BF16),
                        pltpu.VMEM((2, BLOCK_SLAB_ROWS, SLAB_LANES), U32),
                        pltpu.SemaphoreType.DMA((3,)),
                        pltpu.SemaphoreType.DMA((2,))],
    )
    return pl.pallas_call(
        functools.partial(_expert_kernel, layer=layer),
        grid_spec=grid_spec,
        out_shape=jax.ShapeDtypeStruct((TOP_K * Y_ROWS * SLAB_ROWS, SLAB_LANES), U32),
        compiler_params=pltpu.CompilerParams(
            dimension_semantics=("arbitrary",), vmem_limit_bytes=VMEM_LIMIT),
        name="experts",
    )(blk, first, nxt, nused, srt, slabs, w_gate, w_up, w_down)


def _combine_kernel(x_ref, y_ref, gate_ref, sg_ref, su_ref, sd_ref, g_ref, b_ref, o_ref, r_ref, *, tm):
    x = x_ref[...]
    xb = x.astype(BF16)
    hg = jnp.dot(xb, sg_ref[...], preferred_element_type=F32)
    hu = jnp.dot(xb, su_ref[...], preferred_element_type=F32)
    shared = jnp.dot((_silu(hg) * hu).astype(BF16), sd_ref[...], preferred_element_type=F32)
    gate = gate_ref[...]
    for c in range(SLAB_ROWS):
        lo = jnp.zeros((tm, SLAB_LANES), F32)
        hi = jnp.zeros((tm, SLAB_LANES), F32)
        for k in range(TOP_K):
            w = y_ref[k, pl.ds(c, tm, stride=SLAB_ROWS), :]
            gk = gate[:, k:k + 1]
            lo = lo + gk * _unpack_lo(w)
            hi = hi + gk * _unpack_hi(w)
        r_ref[:, c * SLAB_LANES:(c + 1) * SLAB_LANES] = lo
        r_ref[:, HALF_D + c * SLAB_LANES:HALF_D + (c + 1) * SLAB_LANES] = hi
    z = ALPHA * x + (r_ref[...] + shared)
    o_ref[...] = _layer_norm(z, g_ref[...], b_ref[...])


def _combine(x2d, y8, gate_tk, s_gate, s_up, s_down, ln_g, ln_b, *, tm=256):
    return pl.pallas_call(
        functools.partial(_combine_kernel, tm=tm),
        grid=(N_TOK // tm,),
        in_specs=[pl.BlockSpec((tm, D_MODEL), lambda i: (i, 0)),
                  pl.BlockSpec((TOP_K, tm * SLAB_ROWS, SLAB_LANES), lambda i: (0, i, 0)),
                  pl.BlockSpec((tm, TOP_K), lambda i: (i, 0)),
                  _resident((D_MODEL, EXPERT_DIM)), _resident((D_MODEL, EXPERT_DIM)),
                  _resident((EXPERT_DIM, D_MODEL)),
                  _resident((1, D_MODEL)), _resident((1, D_MODEL))],
        out_specs=pl.BlockSpec((tm, D_MODEL), lambda i: (i, 0)),
        out_shape=jax.ShapeDtypeStruct((N_TOK, D_MODEL), F32),
        scratch_shapes=[pltpu.VMEM((tm, D_MODEL), F32)],
        compiler_params=pltpu.CompilerParams(
            dimension_semantics=("arbitrary",), vmem_limit_bytes=VMEM_LIMIT),
        name="combine",
    )(x2d, y8, gate_tk, s_gate, s_up, s_down, ln_g, ln_b)


def _moe(x_f32, slabs, logits_t, layer, router_bias, exp_w_gate, exp_w_up, exp_w_down,
         s_gate, s_up, s_down, ln_g, ln_b):
    packed, gate8, counts = _route(logits_t, router_bias[layer].reshape(N_EXPERTS, 1))
    srt, blk, first, nxt, nused = _finalize(packed.reshape(-1), counts[:, 0])
    y8 = _experts(blk, first, nxt, nused, srt, slabs, exp_w_gate, exp_w_up, exp_w_down, layer=layer)
    y8 = y8.reshape(TOP_K, Y_ROWS * SLAB_ROWS, SLAB_LANES)
    return _combine(x_f32, y8, gate8.T, s_gate[layer].astype(BF16), s_up[layer].astype(BF16),
                    s_down[layer].astype(BF16), ln_g[layer].reshape(1, D_MODEL),
                    ln_b[layer].reshape(1, D_MODEL))


W1_COLS = KV_LORA_RANK + 2 * QK_ROPE_DIM + Q_LORA_RANK
Q_HEAD_COLS = QK_NOPE_DIM + 2 * QK_ROPE_DIM
SM_SCALE = QK_DIM ** -0.5


def _rope_chunk(chunk, cs):
    u = chunk * cs
    return u + pltpu.roll(u, QK_ROPE_DIM, axis=1)


def _proj_kernel(x_ref, cs_ref, w1_ref, kvn_ref, qn_ref, wuk_ref, wuv_ref, wuq_ref,
                 q_ref, kn_ref, kr_ref, v_ref, *, tm):
    xb = x_ref[...].astype(BF16)
    h = jnp.dot(xb, w1_ref[...], preferred_element_type=F32)
    cs = cs_ref[...]
    c = _rms_norm(h[:, :KV_LORA_RANK], kvn_ref[...]).astype(BF16)
    kr = _rope_chunk(h[:, KV_LORA_RANK:KV_LORA_RANK + 2 * QK_ROPE_DIM], cs)
    lane = lax.broadcasted_iota(I32, (tm, 2 * QK_ROPE_DIM), 1)
    kr_ref[...] = jnp.where(lane < QK_ROPE_DIM, kr, 0.0).astype(BF16)
    cq = _rms_norm(h[:, KV_LORA_RANK + 2 * QK_ROPE_DIM:], qn_ref[...]).astype(BF16)
    kn = jnp.dot(c, wuk_ref[...], preferred_element_type=F32)
    v = jnp.dot(c, wuv_ref[...], preferred_element_type=F32)
    q = jnp.dot(cq, wuq_ref[...], preferred_element_type=F32)
    for hd in range(N_HEADS):
        kn_ref[hd] = kn[:, hd * QK_NOPE_DIM:(hd + 1) * QK_NOPE_DIM].astype(BF16)
        v_ref[hd] = v[:, hd * V_DIM:(hd + 1) * V_DIM].astype(BF16)
        q0 = hd * Q_HEAD_COLS
        q_ref[hd, :, :QK_NOPE_DIM] = (q[:, q0:q0 + QK_NOPE_DIM] * SM_SCALE).astype(BF16)
        q_ref[hd, :, QK_NOPE_DIM:] = (
            _rope_chunk(q[:, q0 + QK_NOPE_DIM:q0 + Q_HEAD_COLS], cs) * SM_SCALE).astype(BF16)


def _proj(x2d, cs, w1, kv_norm, q_norm, w_uk, w_uv, w_uq, *, tm=256):
    nj = SEQ // tm
    head_spec = lambda d: pl.BlockSpec((None, N_HEADS, tm, d), lambda b, j: (b, 0, j, 0))
    return pl.pallas_call(
        functools.partial(_proj_kernel, tm=tm),
        grid=(BATCH, nj),
        in_specs=[pl.BlockSpec((tm, D_MODEL), lambda b, j: (b * nj + j, 0)),
                  pl.BlockSpec((tm, 2 * QK_ROPE_DIM), lambda b, j: (j, 0)),
                  _resident(w1.shape), _resident((1, KV_LORA_RANK)), _resident((1, Q_LORA_RANK)),
                  _resident(w_uk.shape), _resident(w_uv.shape), _resident(w_uq.shape)],
        out_specs=[head_spec(Q_HEAD_COLS), head_spec(QK_NOPE_DIM),
                   pl.BlockSpec((None, tm, 2 * QK_ROPE_DIM), lambda b, j: (b, j, 0)),
                   head_spec(V_DIM)],
        out_shape=[jax.ShapeDtypeStruct((BATCH, N_HEADS, SEQ, Q_HEAD_COLS), BF16),
                   jax.ShapeDtypeStruct((BATCH, N_HEADS, SEQ, QK_NOPE_DIM), BF16),
                   jax.ShapeDtypeStruct((BATCH, SEQ, 2 * QK_ROPE_DIM), BF16),
                   jax.ShapeDtypeStruct((BATCH, N_HEADS, SEQ, V_DIM), BF16)],
        compiler_params=pltpu.CompilerParams(
            dimension_semantics=("arbitrary", "arbitrary"), vmem_limit_bytes=VMEM_LIMIT),
        name="proj",
    )(x2d, cs, w1, kv_norm, q_norm, w_uk, w_uv, w_uq)


def _flash_kernel(q_ref, kn_ref, kr_ref, v_ref, o_ref, *, tq):
    qi = pl.program_id(2)
    q = q_ref[...]

    def step(j, carry, diagonal):
        m, l, acc = carry
        k0 = pl.multiple_of(j * tq, tq)
        k = jnp.concatenate([kn_ref[pl.ds(k0, tq), :], kr_ref[pl.ds(k0, tq), :]], axis=1)
        s = lax.dot_general(q, k, NT_DIMS, preferred_element_type=F32)
        if diagonal:
            row = lax.broadcasted_iota(I32, (tq, tq), 0)
            col = lax.broadcasted_iota(I32, (tq, tq), 1)
            s = jnp.where(col <= row, s, -jnp.inf)
        m_new = jnp.maximum(m, jnp.max(s, axis=1, keepdims=True))
        p = jnp.exp(s - m_new)
        a = jnp.exp(m - m_new)
        l = a * l + jnp.sum(p, axis=1, keepdims=True)
        acc = a * acc + jnp.dot(p.astype(BF16), v_ref[pl.ds(k0, tq), :], preferred_element_type=F32)
        return m_new, l, acc

    init = (jnp.full((tq, 1), -jnp.inf, F32), jnp.zeros((tq, 1), F32), jnp.zeros((tq, V_DIM), F32))
    carry = lax.fori_loop(0, qi, lambda j, cr: step(j, cr, False), init)
    _, l, acc = step(qi, carry, True)
    o_ref[...] = (acc / l).astype(BF16)


def _flash(q, kn, kr, v, *, tq=512):
    return pl.pallas_call(
        functools.partial(_flash_kernel, tq=tq),
        grid=(BATCH, N_HEADS, SEQ // tq),
        in_specs=[pl.BlockSpec((None, None, tq, Q_HEAD_COLS), lambda b, h, i: (b, h, i, 0)),
                  pl.BlockSpec((None, None, SEQ, QK_NOPE_DIM), lambda b, h, i: (b, h, 0, 0)),
                  pl.BlockSpec((None, SEQ, 2 * QK_ROPE_DIM), lambda b, h, i: (b, 0, 0)),
                  pl.BlockSpec((None, None, SEQ, V_DIM), lambda b, h, i: (b, h, 0, 0))],
        out_specs=pl.BlockSpec((None, tq, V_DIM), lambda b, h, i: (b, i, h)),
        out_shape=jax.ShapeDtypeStruct((BATCH, SEQ, N_HEADS * V_DIM), BF16),
        compiler_params=pltpu.CompilerParams(
            dimension_semantics=("arbitrary", "arbitrary", "arbitrary"), vmem_limit_bytes=VMEM_LIMIT),
        name="flash",
    )(q, kn, kr, v)


def _attn_out_kernel(a_ref, x_ref, wo_ref, g_ref, b_ref, rwt_ref, xo_ref, slab_ref, lg_ref, *, tm):
    y = jnp.dot(a_ref[...], wo_ref[...], preferred_element_type=F32)
    _ln1_epilogue(ALPHA * x_ref[...] + y, g_ref, b_ref, rwt_ref, xo_ref, slab_ref, lg_ref, tm)


def _attn_out(attn2d, x2d, w_out, ln_g, ln_b, rwt, *, tm=256):
    out_specs, out_shape = _ln1_out_specs(tm)
    return pl.pallas_call(
        functools.partial(_attn_out_kernel, tm=tm),
        grid=(N_TOK // tm,),
        in_specs=[pl.BlockSpec((tm, D_MODEL), lambda i: (i, 0)),
                  pl.BlockSpec((tm, D_MODEL), lambda i: (i, 0)),
                  _resident((D_MODEL, D_MODEL)),
                  _resident((1, D_MODEL)), _resident((1, D_MODEL)), _resident((N_EXPERTS, D_MODEL))],
        out_specs=out_specs, out_shape=out_shape,
        compiler_params=pltpu.CompilerParams(
            dimension_semantics=("arbitrary",), vmem_limit_bytes=VMEM_LIMIT),
        name="attn_out",
    )(attn2d, x2d, w_out, ln_g, ln_b, rwt)


def _rope_table():
    inv_freq = 1.0 / (ROPE_THETA ** (jnp.arange(0, QK_ROPE_DIM, 2, dtype=F32) / QK_ROPE_DIM))
    ang = jnp.arange(SEQ, dtype=F32)[:, None] * inv_freq[None, :]
    cos, sin = jnp.cos(ang), jnp.sin(ang)
    return jnp.concatenate([cos, cos, -sin, sin], axis=1)


def _swap_halves(w):
    half = QK_ROPE_DIM // 2
    return jnp.concatenate([w[..., half:], w[..., :half]], axis=-1)


def kernel(x, a_w_in, a_w_grp, a_scale, a_w_out, kv_w_down, kv_norm, kv_w_uk, kv_w_uv, b_w_dq, b_q_norm, b_w_uq, b_w_out, ln1_g, ln1_b, ln2_g, ln2_b, router_w, router_bias, exp_w_gate, exp_w_up, exp_w_down, sh_w_gate, sh_w_up, sh_w_down):
    x2d = x.reshape(N_TOK, D_MODEL)
    row = lambda v: v.reshape(1, -1)
    moe_args = (router_bias, exp_w_gate, exp_w_up, exp_w_down, sh_w_gate, sh_w_up, sh_w_down, ln2_g, ln2_b)

    pooled = _pool_in(x2d, a_w_in[0].astype(BF16))
    x1, slabs, logits_t = _mix_out(pooled, x2d, a_w_grp[0].astype(BF16), row(a_scale[0]),
                                   a_w_out[0].astype(BF16), row(ln1_g[0]), row(ln1_b[0]), router_w[0].T)
    x2 = _moe(x1, slabs, logits_t, 0, *moe_args)

    kr_w = kv_w_down[:, KV_LORA_RANK:]
    w1 = jnp.concatenate([kv_w_down[:, :KV_LORA_RANK], kr_w, _swap_halves(kr_w), b_w_dq[0]],
                         axis=1).astype(BF16)
    uq = b_w_uq[0]
    uq_rope = uq[:, :, QK_NOPE_DIM:]
    w_uq = jnp.concatenate([uq[:, :, :QK_NOPE_DIM], uq_rope, _swap_halves(uq_rope)], axis=2)
    w_uq = w_uq.reshape(Q_LORA_RANK, N_HEADS * Q_HEAD_COLS).astype(BF16)
    q, kn, kr, v = _proj(x2, _rope_table(), w1, row(kv_norm), row(b_q_norm[0]),
                         kv_w_uk.reshape(KV_LORA_RANK, N_HEADS * QK_NOPE_DIM).astype(BF16),
                         kv_w_uv.reshape(KV_LORA_RANK, N_HEADS * V_DIM).astype(BF16), w_uq)
    attn = _flash(q, kn, kr, v).reshape(N_TOK, N_HEADS * V_DIM)
    x3, slabs, logits_t = _attn_out(attn, x2, b_w_out[0].astype(BF16), row(ln1_g[1]), row(ln1_b[1]),
                                    router_w[1].T)
    x4 = _moe(x3, slabs, logits_t, 1, *moe_args)
    return x4.reshape(BATCH, SEQ, D_MODEL)
```

```python
import functools

import jax
import jax.numpy as jnp
from jax import lax
from jax.experimental import pallas as pl
from jax.experimental.pallas import tpu as pltpu

D_MODEL = 2048
BATCH = 2
SEQ = 4096
DEPTH = 2
N_TOK = BATCH * SEQ
ALPHA = (2.0 * DEPTH) ** 0.25
POOL_WINDOWS = (2, 4, 8, 16)
POOL_GROUP_DIM = D_MODEL // len(POOL_WINDOWS)
POOL_HALO = 16
N_HEADS = 16
QK_NOPE_DIM = 128
QK_ROPE_DIM = 64
QK_DIM = QK_NOPE_DIM + QK_ROPE_DIM
V_DIM = 128
Q_LORA_RANK = D_MODEL // 4
KV_LORA_RANK = D_MODEL // 4
ROPE_THETA = 10000.0
N_EXPERTS = 64
TOP_K = 8
N_EXPERT_GROUPS = 8
GROUP_SIZE = N_EXPERTS // N_EXPERT_GROUPS
TOPK_GROUPS = 4
EXPERT_DIM = D_MODEL // 4
ROUTED_SCALE = 2.5
ROW_BLOCK = 128
LN_EPS = 1e-5
RMS_EPS = 1e-6

N_ASSIGN = N_TOK * TOP_K
N_BLOCKS = N_ASSIGN // ROW_BLOCK + N_EXPERTS
N_SRT = (N_BLOCKS + 2) * ROW_BLOCK
Y_ROWS = N_TOK + 2 * ROW_BLOCK
TOK_BITS = 14
assert N_TOK <= 1 << TOK_BITS and TOP_K * Y_ROWS < 1 << (31 - TOK_BITS)
SLAB_ROWS = 8
SLAB_LANES = 128
HALF_D = D_MODEL // 2
VMEM_LIMIT = 60 * 1024 * 1024

F32 = jnp.float32
BF16 = jnp.bfloat16
I32 = jnp.int32
U32 = jnp.uint32
NT_DIMS = (((1,), (1,)), ((), ()))


def _pack_rows(z):
    return pltpu.pack_elementwise([z[:, :HALF_D], z[:, HALF_D:]], packed_dtype=BF16)


def _unpack_lo(w):
    return pltpu.unpack_elementwise(w, index=0, packed_dtype=BF16, unpacked_dtype=F32)


def _unpack_hi(w):
    return pltpu.unpack_elementwise(w, index=1, packed_dtype=BF16, unpacked_dtype=F32)


def _silu(x):
    return x * jax.nn.sigmoid(x)


def _layer_norm(z, g, b):
    mu = jnp.mean(z, axis=-1, keepdims=True)
    zc = z - mu
    var = jnp.mean(zc * zc, axis=-1, keepdims=True)
    return zc * lax.rsqrt(var + LN_EPS) * g + b


def _rms_norm(z, g):
    ms = jnp.mean(z * z, axis=-1, keepdims=True)
    return z * lax.rsqrt(ms + RMS_EPS) * g


def _resident(shape):
    nd = len(shape)
    return pl.BlockSpec(shape, lambda *_: (0,) * nd, pipeline_mode=pl.Buffered(1))


def _pool_in_kernel(x_ref, w_ref, o_ref, tail_ref, *, tm):
    j = pl.program_id(1)

    @pl.when(j == 0)
    def _():
        tail_ref[...] = jnp.zeros_like(tail_ref)

    h = jnp.dot(x_ref[...].astype(BF16), w_ref[...], preferred_element_type=F32)
    ext = jnp.concatenate([tail_ref[...], h], axis=0)
    tail_ref[...] = h[tm - POOL_HALO:, :]
    pos = j * tm + lax.broadcasted_iota(I32, (tm, 1), 0)
    for g, w in enumerate(POOL_WINDOWS):
        c0, c1 = g * POOL_GROUP_DIM, (g + 1) * POOL_GROUP_DIM
        s = ext[:, c0:c1]
        sh = 1
        while sh < w:
            s = s + pltpu.roll(s, sh, axis=0)
            sh *= 2
        inv = 1.0 / jnp.minimum(pos + 1, w).astype(F32)
        o_ref[:, c0:c1] = (s[POOL_HALO:, :] * inv - h[:, c0:c1]).astype(BF16)


def _pool_in(x2d, w_in, *, tm=512):
    nj = SEQ // tm
    return pl.pallas_call(
        functools.partial(_pool_in_kernel, tm=tm),
        grid=(BATCH, nj),
        in_specs=[pl.BlockSpec((tm, D_MODEL), lambda b, j: (b * nj + j, 0)),
                  _resident((D_MODEL, D_MODEL))],
        out_specs=pl.BlockSpec((tm, D_MODEL), lambda b, j: (b * nj + j, 0)),
        out_shape=jax.ShapeDtypeStruct((N_TOK, D_MODEL), BF16),
        scratch_shapes=[pltpu.VMEM((POOL_HALO, D_MODEL), F32)],
        compiler_params=pltpu.CompilerParams(
            dimension_semantics=("arbitrary", "arbitrary"), vmem_limit_bytes=VMEM_LIMIT),
        name="pool_in",
    )(x2d, w_in)


SUB_ROWS = 128


def _ln1_epilogue(ys, x_ref, g_ref, b_ref, rw_ref, xo_ref, slab_ref, lg_ref):
    nt = lambda a, b: lax.dot_general(a, b, NT_DIMS, preferred_element_type=F32)
    for s, y in enumerate(ys):
        r0 = s * SUB_ROWS
        xn = _layer_norm(ALPHA * x_ref[r0:r0 + SUB_ROWS, :] + y, g_ref[...], b_ref[...])
        xo_ref[r0:r0 + SUB_ROWS, :] = xn
        pk = _pack_rows(xn)
        for c in range(SLAB_ROWS):
            slab_ref[pl.ds(r0 * SLAB_ROWS + c, SUB_ROWS, stride=SLAB_ROWS), :] = (
                pk[:, c * SLAB_LANES:(c + 1) * SLAB_LANES])
        x_hi = xn.astype(BF16)
        x_lo = (xn - x_hi.astype(F32)).astype(BF16)
        lg_ref[:, r0:r0 + SUB_ROWS] = nt(rw_ref[0], x_hi) + (nt(rw_ref[0], x_lo) + nt(rw_ref[1], x_hi))


def _mix_out_kernel(p_ref, x_ref, wg_ref, sc_ref, wo_ref, g_ref, b_ref, rw_ref,
                    xo_ref, slab_ref, lg_ref, *, tm):
    ys = []
    for s in range(tm // SUB_ROWS):
        r0 = s * SUB_ROWS
        parts = []
        for g in range(len(POOL_WINDOWS)):
            c0, c1 = g * POOL_GROUP_DIM, (g + 1) * POOL_GROUP_DIM
            parts.append(jnp.dot(p_ref[r0:r0 + SUB_ROWS, c0:c1], wg_ref[g], preferred_element_type=F32))
        mixed = (jnp.concatenate(parts, axis=1) * sc_ref[...]).astype(BF16)
        ys.append(jnp.dot(mixed, wo_ref[...], preferred_element_type=F32))
    _ln1_epilogue(ys, x_ref, g_ref, b_ref, rw_ref, xo_ref, slab_ref, lg_ref)


def _ln1_out_specs(tm):
    return (
        [pl.BlockSpec((tm, D_MODEL), lambda i: (i, 0)),
         pl.BlockSpec((tm * SLAB_ROWS, SLAB_LANES), lambda i: (i, 0)),
         pl.BlockSpec((N_EXPERTS, tm), lambda i: (0, i))],
        [jax.ShapeDtypeStruct((N_TOK, D_MODEL), F32),
         jax.ShapeDtypeStruct((N_TOK * SLAB_ROWS, SLAB_LANES), U32),
         jax.ShapeDtypeStruct((N_EXPERTS, N_TOK), F32)],
    )


def _split_bf16(w):
    hi = w.astype(BF16)
    return jnp.stack([hi, (w - hi.astype(F32)).astype(BF16)])


def _mix_out(pooled, x2d, w_grp, scale, w_out, ln_g, ln_b, rw, *, tm=512):
    out_specs, out_shape = _ln1_out_specs(tm)
    return pl.pallas_call(
        functools.partial(_mix_out_kernel, tm=tm),
        grid=(N_TOK // tm,),
        in_specs=[pl.BlockSpec((tm, D_MODEL), lambda i: (i, 0)),
                  pl.BlockSpec((tm, D_MODEL), lambda i: (i, 0)),
                  _resident(w_grp.shape), _resident((1, D_MODEL)), _resident((D_MODEL, D_MODEL)),
                  _resident((1, D_MODEL)), _resident((1, D_MODEL)),
                  _resident((2, N_EXPERTS, D_MODEL))],
        out_specs=out_specs, out_shape=out_shape,
        compiler_params=pltpu.CompilerParams(
            dimension_semantics=("arbitrary",), vmem_limit_bytes=VMEM_LIMIT),
        name="mix_out",
    )(pooled, x2d, w_grp, scale, w_out, ln_g, ln_b, rw)


def _first_index(hit_src, best, iota, n):
    return jnp.min(jnp.where(hit_src == best, iota, n), axis=0, keepdims=True)


def _route_kernel(lg_ref, bias_ref, pk_ref, gate_ref, cnt_ref, carry_ref, *, tm):
    i = pl.program_id(0)

    @pl.when(i == 0)
    def _():
        carry_ref[...] = jnp.zeros_like(carry_ref)

    neg = -jnp.inf
    scores = jax.nn.sigmoid(lg_ref[...])
    choice = scores + bias_ref[...]
    iota_g = lax.broadcasted_iota(I32, (GROUP_SIZE, tm), 0)
    gscore = []
    for g in range(N_EXPERT_GROUPS):
        c = choice[g * GROUP_SIZE:(g + 1) * GROUP_SIZE, :]
        m1 = jnp.max(c, axis=0, keepdims=True)
        f1 = _first_index(c, m1, iota_g, GROUP_SIZE)
        m2 = jnp.max(jnp.where(iota_g == f1, neg, c), axis=0, keepdims=True)
        gscore.append(m1 + m2)
    gs = jnp.concatenate(gscore, axis=0)
    iota_ng = lax.broadcasted_iota(I32, (N_EXPERT_GROUPS, tm), 0)
    gsel = jnp.zeros((N_EXPERT_GROUPS, tm), jnp.bool_)
    for _ in range(TOPK_GROUPS):
        m = jnp.max(gs, axis=0, keepdims=True)
        hit = iota_ng == _first_index(gs, m, iota_ng, N_EXPERT_GROUPS)
        gsel = gsel | hit
        gs = jnp.where(hit, neg, gs)
    masked = jnp.concatenate(
        [jnp.where(gsel[g:g + 1, :], choice[g * GROUP_SIZE:(g + 1) * GROUP_SIZE, :], neg)
         for g in range(N_EXPERT_GROUPS)], axis=0)
    iota_e = lax.broadcasted_iota(I32, (N_EXPERTS, tm), 0)
    hits, eidx, gates = [], [], []
    sel = jnp.zeros((N_EXPERTS, tm), jnp.bool_)
    for _ in range(TOP_K):
        m = jnp.max(masked, axis=0, keepdims=True)
        f = _first_index(masked, m, iota_e, N_EXPERTS)
        hit = iota_e == f
        hits.append(hit)
        eidx.append(f)
        gates.append(jnp.sum(jnp.where(hit, scores, 0.0), axis=0, keepdims=True))
        masked = jnp.where(hit, neg, masked)
        sel = sel | hit
    gate = jnp.concatenate(gates, axis=0)
    gate_ref[...] = gate / jnp.sum(gate, axis=0, keepdims=True) * ROUTED_SCALE
    selb = jnp.where(sel, 1.0, 0.0).astype(BF16)
    before = (lax.broadcasted_iota(I32, (tm, tm), 0) < lax.broadcasted_iota(I32, (tm, tm), 1))
    rank = jnp.dot(selb, jnp.where(before, 1.0, 0.0).astype(BF16), preferred_element_type=F32)
    rank = (rank + carry_ref[...]).astype(I32)
    rk = [jnp.sum(jnp.where(h, rank, 0), axis=0, keepdims=True) for h in hits]
    pk_ref[...] = (jnp.concatenate(eidx, axis=0) << 16) | jnp.concatenate(rk, axis=0)
    carry_ref[...] = carry_ref[...] + jnp.sum(selb.astype(F32), axis=1, keepdims=True)
    cnt_ref[...] = jnp.broadcast_to(carry_ref[...], cnt_ref.shape).astype(I32)


def _route(logits_t, bias, *, tm=512):
    return pl.pallas_call(
        functools.partial(_route_kernel, tm=tm),
        grid=(N_TOK // tm,),
        in_specs=[pl.BlockSpec((N_EXPERTS, tm), lambda i: (0, i)),
                  _resident((N_EXPERTS, 1))],
        out_specs=[pl.BlockSpec((TOP_K, tm), lambda i: (0, i)),
                   pl.BlockSpec((TOP_K, tm), lambda i: (0, i)),
                   pl.BlockSpec((N_EXPERTS, SLAB_LANES), lambda i: (0, 0))],
        out_shape=[jax.ShapeDtypeStruct((TOP_K, N_TOK), I32),
                   jax.ShapeDtypeStruct((TOP_K, N_TOK), F32),
                   jax.ShapeDtypeStruct((N_EXPERTS, SLAB_LANES), I32)],
        scratch_shapes=[pltpu.VMEM((N_EXPERTS, 1), F32)],
        compiler_params=pltpu.CompilerParams(dimension_semantics=("arbitrary",)),
        name="route",
    )(logits_t, bias)


def _expert_offsets(cnt_ref, ps_ref):
    def offsets(e, acc):
        ps_ref[e] = acc
        return acc + ((cnt_ref[e] + (ROW_BLOCK - 1)) // ROW_BLOCK) * ROW_BLOCK

    return lax.fori_loop(0, N_EXPERTS, offsets, jnp.int32(ROW_BLOCK))


def _positions_kernel(cnt_ref, pk_ref, pos_ref, ps_ref):
    _expert_offsets(cnt_ref, ps_ref)
    pk = pk_ref[...]
    eidx = pk >> 16
    pos = pk & 0xFFFF
    for e in range(N_EXPERTS):
        pos = pos + jnp.where(eidx == e, ps_ref[e], 0)
    pos_ref[...] = pos


def _positions(counts, packed):
    return pl.pallas_call(
        _positions_kernel,
        in_specs=[pl.BlockSpec(memory_space=pltpu.SMEM), pl.BlockSpec(memory_space=pltpu.VMEM)],
        out_specs=pl.BlockSpec(memory_space=pltpu.VMEM),
        out_shape=jax.ShapeDtypeStruct((TOP_K, N_TOK), I32),
        scratch_shapes=[pltpu.SMEM((N_EXPERTS,), I32)],
        name="positions",
    )(counts, packed)


def _finalize_kernel(pos_ref, cnt_ref, srt_ref, blk_ref, first_ref, nxt_ref, nused_ref, ps_ref, *, tt):
    i = pl.program_id(0)

    def pad_fill(lo, hi):
        def body(r, carry):
            srt_ref[r] = (N_TOK + lax.rem(r // ROW_BLOCK + 1, 2) * ROW_BLOCK
                          + lax.rem(r, ROW_BLOCK)) << TOK_BITS
            return carry

        lax.fori_loop(lo, hi, body, 0)

    @pl.when(i == 0)
    def _():
        total = _expert_offsets(cnt_ref, ps_ref)
        nused_ref[0] = total // ROW_BLOCK - 1
        pad_fill(0, ROW_BLOCK)
        pad_fill(total, N_SRT)

        def defaults(b, c):
            blk_ref[b] = N_EXPERTS - 1
            first_ref[b] = 0
            nxt_ref[b] = -1
            return c

        lax.fori_loop(0, N_BLOCKS, defaults, 0)

        def per_expert(j, nxt_e):
            e = N_EXPERTS - 1 - j
            c = cnt_ref[e]
            nb = (c + (ROW_BLOCK - 1)) // ROW_BLOCK
            r0 = ps_ref[e]
            b0 = r0 // ROW_BLOCK - 1

            def blocks(jb, carry):
                blk_ref[b0 + jb] = e
                first_ref[b0 + jb] = jnp.where(jb == 0, 1, 0)
                nxt_ref[b0 + jb] = nxt_e
                return carry

            lax.fori_loop(0, nb, blocks, 0)
            pad_fill(r0 + c, r0 + nb * ROW_BLOCK)
            return jnp.where(c > 0, e, nxt_e)

        lax.fori_loop(0, N_EXPERTS, per_expert, jnp.int32(-1))

    def per_token(tl, carry):
        t = i * tt + tl
        entry = t * ((1 << TOK_BITS) + 1)
        for k in range(TOP_K):
            srt_ref[pos_ref[k * N_TOK + t]] = entry + ((k * Y_ROWS) << TOK_BITS)
        return carry

    lax.fori_loop(0, tt, per_token, 0)


def _finalize(packed_flat, counts, *, tt=512):
    smem = pl.BlockSpec(memory_space=pltpu.SMEM)
    return pl.pallas_call(
        functools.partial(_finalize_kernel, tt=tt),
        grid=(N_TOK // tt,),
        in_specs=[smem, smem],
        out_specs=[smem] * 5,
        out_shape=[jax.ShapeDtypeStruct((N_SRT,), I32),
                   jax.ShapeDtypeStruct((N_BLOCKS,), I32),
                   jax.ShapeDtypeStruct((N_BLOCKS,), I32),
                   jax.ShapeDtypeStruct((N_BLOCKS,), I32),
                   jax.ShapeDtypeStruct((1,), I32)],
        scratch_shapes=[pltpu.SMEM((N_EXPERTS,), I32)],
        compiler_params=pltpu.CompilerParams(dimension_semantics=("arbitrary",)),
        name="finalize",
    )(packed_flat, counts)


BLOCK_SLAB_ROWS = ROW_BLOCK * SLAB_ROWS
BLOCKS_PER_STEP = 2
CAST_VREGS = 32


def _cast_weight(src_ref, dst_ref):
    rows, cols = src_ref.shape
    step = CAST_VREGS * 8 * 128 // cols

    def body(r, carry):
        r0 = pl.multiple_of(r * step, step)
        dst_ref[pl.ds(r0, step), :] = src_ref[pl.ds(r0, step), :].astype(BF16)
        return carry

    lax.fori_loop(0, rows // step, body, 0)


def _expert_kernel(blk_ref, first_ref, nxt_ref, nused_ref, srt_ref,
                   xs_ref, wg_hbm, wu_hbm, wd_hbm, y_hbm,
                   sg_ref, su_ref, sd_ref, bg_ref, bu_ref, bd_ref,
                   xg_ref, xb_ref, ys_ref, wsem, ysem, *, layer):
    step = pl.program_id(0)
    nused = nused_ref[0]

    def weight_copies(ee):
        return (pltpu.make_async_copy(wg_hbm.at[layer, ee], sg_ref, wsem.at[0]),
                pltpu.make_async_copy(wu_hbm.at[layer, ee], su_ref, wsem.at[1]),
                pltpu.make_async_copy(wd_hbm.at[layer, ee], sd_ref, wsem.at[2]))

    def wait_rows(slot):
        pltpu.make_async_copy(ys_ref.at[slot], y_hbm.at[pl.ds(0, BLOCK_SLAB_ROWS)],
                              ysem.at[slot]).wait()

    def gather_block(blk, slot):
        base = (blk + 1) * ROW_BLOCK
        for i in range(ROW_BLOCK):
            tok = srt_ref[base + i] & ((1 << TOK_BITS) - 1)
            xg_ref[pl.ds(i * SLAB_ROWS, SLAB_ROWS), :] = (
                xs_ref[pl.ds(pl.multiple_of(tok * SLAB_ROWS, SLAB_ROWS), SLAB_ROWS), :])
        for c in range(SLAB_ROWS):
            w = xg_ref[pl.ds(c, ROW_BLOCK, stride=SLAB_ROWS), :]
            xb_ref[slot, :, c * SLAB_LANES:(c + 1) * SLAB_LANES] = _unpack_lo(w).astype(BF16)
            xb_ref[slot, :, HALF_D + c * SLAB_LANES:HALF_D + (c + 1) * SLAB_LANES] = (
                _unpack_hi(w).astype(BF16))

    @pl.when(step == 0)
    def _():
        for cp in weight_copies(blk_ref[0]):
            cp.start()
        ys_ref[...] = jnp.zeros_like(ys_ref)
        zero_copies = [
            pltpu.make_async_copy(
                ys_ref.at[0],
                y_hbm.at[pl.ds((k * Y_ROWS + N_TOK + s * ROW_BLOCK) * SLAB_ROWS, BLOCK_SLAB_ROWS)],
                ysem.at[0])
            for k in range(TOP_K) for s in range(2)]
        for cp in zero_copies:
            cp.start()
        for cp in zero_copies:
            cp.wait()
        gather_block(0, 0)

    def one_block(b, cur):
        prev = 1 - cur

        @pl.when(b <= nused)
        def _():
            @pl.when(first_ref[b] == 1)
            def _():
                for cp in weight_copies(blk_ref[b]):
                    cp.wait()
                _cast_weight(sg_ref, bg_ref)
                _cast_weight(su_ref, bu_ref)
                _cast_weight(sd_ref, bd_ref)

                @pl.when(nxt_ref[b] >= 0)
                def _():
                    for cp in weight_copies(nxt_ref[b]):
                        cp.start()

            pbase = b * ROW_BLOCK
            for i in range(ROW_BLOCK):
                slab = srt_ref[pbase + i] >> TOK_BITS
                pltpu.make_async_copy(
                    ys_ref.at[prev, pl.ds(i * SLAB_ROWS, SLAB_ROWS)],
                    y_hbm.at[pl.ds(pl.multiple_of(slab * SLAB_ROWS, SLAB_ROWS), SLAB_ROWS)],
                    ysem.at[prev]).start()

            x = xb_ref[cur]
            hg = jnp.dot(x, bg_ref[...], preferred_element_type=F32)
            hu = jnp.dot(x, bu_ref[...], preferred_element_type=F32)

            @pl.when(b > 0)
            def _():
                wait_rows(cur)

            gather_block(b + 1, prev)
            act = (_silu(hg) * hu).astype(BF16)
            pk = _pack_rows(jnp.dot(act, bd_ref[...], preferred_element_type=F32))
            for c in range(SLAB_ROWS):
                ys_ref[cur, pl.ds(c, ROW_BLOCK, stride=SLAB_ROWS), :] = (
                    pk[:, c * SLAB_LANES:(c + 1) * SLAB_LANES])

            @pl.when(b == nused)
            def _():
                wait_rows(prev)

    for j in range(BLOCKS_PER_STEP):
        one_block(step * BLOCKS_PER_STEP + j, j)


def _experts(blk, first, nxt, nused, srt, slabs, w_gate, w_up, w_down, *, layer):
    grid_spec = pltpu.PrefetchScalarGridSpec(
        num_scalar_prefetch=5,
        grid=(N_BLOCKS // BLOCKS_PER_STEP,),
        in_specs=[pl.BlockSpec((N_TOK * SLAB_ROWS, SLAB_LANES), lambda b, *_: (0, 0),
                               pipeline_mode=pl.Buffered(1)),
                  pl.BlockSpec(memory_space=pl.ANY),
                  pl.BlockSpec(memory_space=pl.ANY),
                  pl.BlockSpec(memory_space=pl.ANY)],
        out_specs=pl.BlockSpec(memory_space=pl.ANY),
        scratch_shapes=[pltpu.VMEM((D_MODEL, EXPERT_DIM), F32),
                        pltpu.VMEM((D_MODEL, EXPERT_DIM), F32),
                        pltpu.VMEM((EXPERT_DIM, D_MODEL), F32),
                        pltpu.VMEM((D_MODEL, EXPERT_DIM), BF16),
                        pltpu.VMEM((D_MODEL, EXPERT_DIM), BF16),
                        pltpu.VMEM((EXPERT_DIM, D_MODEL), BF16),
                        pltpu.VMEM((BLOCK_SLAB_ROWS, SLAB_LANES), U32),
                        pltpu.VMEM((2, ROW_BLOCK, D_MODEL), BF16),
                        pltpu.VMEM((2, BLOCK_SLAB_ROWS, SLAB_LANES), U32),
                        pltpu.SemaphoreType.DMA((3,)),
                        pltpu.SemaphoreType.DMA((2,))],
    )
    return pl.pallas_call(
        functools.partial(_expert_kernel, layer=layer),
        grid_spec=grid_spec,
        out_shape=jax.ShapeDtypeStruct((TOP_K * Y_ROWS * SLAB_ROWS, SLAB_LANES), U32),
        compiler_params=pltpu.CompilerParams(
            dimension_semantics=("arbitrary",), vmem_limit_bytes=VMEM_LIMIT),
        name="experts",
    )(blk, first, nxt, nused, srt, slabs, w_gate, w_up, w_down)


def _combine_kernel(x_ref, y_ref, gate_ref, sg_ref, su_ref, sd_ref, g_ref, b_ref, o_ref, r_ref, *, tm):
    x = x_ref[...]
    xb = x.astype(BF16)
    hg = jnp.dot(xb, sg_ref[...], preferred_element_type=F32)
    hu = jnp.dot(xb, su_ref[...], preferred_element_type=F32)
    shared = jnp.dot((_silu(hg) * hu).astype(BF16), sd_ref[...], preferred_element_type=F32)
    gate = gate_ref[...]
    for c in range(SLAB_ROWS):
        lo = jnp.zeros((tm, SLAB_LANES), F32)
        hi = jnp.zeros((tm, SLAB_LANES), F32)
        for k in range(TOP_K):
            w = y_ref[k, pl.ds(c, tm, stride=SLAB_ROWS), :]
            gk = gate[:, k:k + 1]
            lo = lo + gk * _unpack_lo(w)
            hi = hi + gk * _unpack_hi(w)
        r_ref[:, c * SLAB_LANES:(c + 1) * SLAB_LANES] = lo
        r_ref[:, HALF_D + c * SLAB_LANES:HALF_D + (c + 1) * SLAB_LANES] = hi
    z = ALPHA * x + (r_ref[...] + shared)
    o_ref[...] = _layer_norm(z, g_ref[...], b_ref[...])


def _combine(x2d, y8, gate_tk, s_gate, s_up, s_down, ln_g, ln_b, *, tm=256):
    return pl.pallas_call(
        functools.partial(_combine_kernel, tm=tm),
        grid=(N_TOK // tm,),
        in_specs=[pl.BlockSpec((tm, D_MODEL), lambda i: (i, 0)),
                  pl.BlockSpec((TOP_K, tm * SLAB_ROWS, SLAB_LANES), lambda i: (0, i, 0)),
                  pl.BlockSpec((tm, TOP_K), lambda i: (i, 0)),
                  _resident((D_MODEL, EXPERT_DIM)), _resident((D_MODEL, EXPERT_DIM)),
                  _resident((EXPERT_DIM, D_MODEL)),
                  _resident((1, D_MODEL)), _resident((1, D_MODEL))],
        out_specs=pl.BlockSpec((tm, D_MODEL), lambda i: (i, 0)),
        out_shape=jax.ShapeDtypeStruct((N_TOK, D_MODEL), F32),
        scratch_shapes=[pltpu.VMEM((tm, D_MODEL), F32)],
        compiler_params=pltpu.CompilerParams(
            dimension_semantics=("arbitrary",), vmem_limit_bytes=VMEM_LIMIT),
        name="combine",
    )(x2d, y8, gate_tk, s_gate, s_up, s_down, ln_g, ln_b)


def _moe(x_f32, slabs, logits_t, layer, router_bias, exp_w_gate, exp_w_up, exp_w_down,
         s_gate, s_up, s_down, ln_g, ln_b):
    packed, gate8, counts = _route(logits_t, router_bias[layer].reshape(N_EXPERTS, 1))
    counts = counts[:, 0]
    srt, blk, first, nxt, nused = _finalize(_positions(counts, packed).reshape(-1), counts)
    y8 = _experts(blk, first, nxt, nused, srt, slabs, exp_w_gate, exp_w_up, exp_w_down, layer=layer)
    y8 = y8.reshape(TOP_K, Y_ROWS * SLAB_ROWS, SLAB_LANES)
    return _combine(x_f32, y8, gate8.T, s_gate[layer].astype(BF16), s_up[layer].astype(BF16),
                    s_down[layer].astype(BF16), ln_g[layer].reshape(1, D_MODEL),
                    ln_b[layer].reshape(1, D_MODEL))


W1_COLS = KV_LORA_RANK + 2 * QK_ROPE_DIM + Q_LORA_RANK
Q_HEAD_COLS = QK_NOPE_DIM + 2 * QK_ROPE_DIM
LOG2_E = 1.4426950408889634
Q_SCALE = QK_DIM ** -0.5 * LOG2_E
FLASH_CHAIN = 256


def _rope_chunk(chunk, cs):
    u = chunk * cs
    return u + pltpu.roll(u, QK_ROPE_DIM, axis=1)


def _proj_kernel(x_ref, cs_ref, w1_ref, kvn_ref, qn_ref, wuk_ref, wuv_ref, wuq_ref,
                 q_ref, kn_ref, kr_ref, v_ref, *, tm):
    xb = x_ref[...].astype(BF16)
    h = jnp.dot(xb, w1_ref[...], preferred_element_type=F32)
    cs = cs_ref[...]
    c = _rms_norm(h[:, :KV_LORA_RANK], kvn_ref[...]).astype(BF16)
    kr = _rope_chunk(h[:, KV_LORA_RANK:KV_LORA_RANK + 2 * QK_ROPE_DIM], cs)
    lane = lax.broadcasted_iota(I32, (tm, 2 * QK_ROPE_DIM), 1)
    kr_ref[...] = jnp.where(lane < QK_ROPE_DIM, kr, 0.0).astype(BF16)
    cq = _rms_norm(h[:, KV_LORA_RANK + 2 * QK_ROPE_DIM:], qn_ref[...]).astype(BF16)
    kn = jnp.dot(c, wuk_ref[...], preferred_element_type=F32)
    vt = lax.dot_general(wuv_ref[...], c, NT_DIMS, preferred_element_type=F32)
    q = jnp.dot(cq, wuq_ref[...], preferred_element_type=F32)
    for hd in range(N_HEADS):
        kn_ref[hd] = kn[:, hd * QK_NOPE_DIM:(hd + 1) * QK_NOPE_DIM].astype(BF16)
        v_ref[hd] = vt[hd * V_DIM:(hd + 1) * V_DIM, :].astype(BF16)
        q0 = hd * Q_HEAD_COLS
        q_ref[hd, :, :QK_NOPE_DIM] = (q[:, q0:q0 + QK_NOPE_DIM] * Q_SCALE).astype(BF16)
        q_ref[hd, :, QK_NOPE_DIM:] = (
            _rope_chunk(q[:, q0 + QK_NOPE_DIM:q0 + Q_HEAD_COLS], cs) * Q_SCALE).astype(BF16)


def _proj(x2d, cs, w1, kv_norm, q_norm, w_uk, w_uv, w_uq, *, tm=256):
    nj = SEQ // tm
    head_spec = lambda d: pl.BlockSpec((None, N_HEADS, tm, d), lambda b, j: (b, 0, j, 0))
    return pl.pallas_call(
        functools.partial(_proj_kernel, tm=tm),
        grid=(BATCH, nj),
        in_specs=[pl.BlockSpec((tm, D_MODEL), lambda b, j: (b * nj + j, 0)),
                  pl.BlockSpec((tm, 2 * QK_ROPE_DIM), lambda b, j: (j, 0)),
                  _resident(w1.shape), _resident((1, KV_LORA_RANK)), _resident((1, Q_LORA_RANK)),
                  _resident(w_uk.shape), _resident(w_uv.shape), _resident(w_uq.shape)],
        out_specs=[head_spec(Q_HEAD_COLS), head_spec(QK_NOPE_DIM),
                   pl.BlockSpec((None, tm, 2 * QK_ROPE_DIM), lambda b, j: (b, j, 0)),
                   pl.BlockSpec((None, N_HEADS, V_DIM, tm), lambda b, j: (b, 0, 0, j))],
        out_shape=[jax.ShapeDtypeStruct((BATCH, N_HEADS, SEQ, Q_HEAD_COLS), BF16),
                   jax.ShapeDtypeStruct((BATCH, N_HEADS, SEQ, QK_NOPE_DIM), BF16),
                   jax.ShapeDtypeStruct((BATCH, SEQ, 2 * QK_ROPE_DIM), BF16),
                   jax.ShapeDtypeStruct((BATCH, N_HEADS, V_DIM, SEQ), BF16)],
        compiler_params=pltpu.CompilerParams(
            dimension_semantics=("arbitrary", "arbitrary"), vmem_limit_bytes=VMEM_LIMIT),
        name="proj",
    )(x2d, cs, w1, kv_norm, q_norm, w_uk, w_uv, w_uq)


def _flash_kernel(q_ref, kn_ref, kr_ref, vt_ref, o_ref, *, tq):
    qi = pl.program_id(2)
    n_chain = tq // FLASH_CHAIN
    qs = [q_ref[c * FLASH_CHAIN:(c + 1) * FLASH_CHAIN, :] for c in range(n_chain)]

    def scores(j):
        k0 = pl.multiple_of(j * tq, tq)
        k = jnp.concatenate([kn_ref[pl.ds(k0, tq), :], kr_ref[pl.ds(k0, tq), :]], axis=1)
        return tuple(lax.dot_general(k, qc, NT_DIMS, preferred_element_type=F32) for qc in qs)

    def consume(j, stats, ss, diagonal):
        vt = vt_ref[:, pl.ds(pl.multiple_of(j * tq, tq), tq)]
        out = []
        for c, ((m, l, acc), s) in enumerate(zip(stats, ss)):
            if diagonal:
                key = lax.broadcasted_iota(I32, (tq, FLASH_CHAIN), 0)
                qry = lax.broadcasted_iota(I32, (tq, FLASH_CHAIN), 1) + c * FLASH_CHAIN
                s = jnp.where(key <= qry, s, -jnp.inf)
            m_new = jnp.maximum(m, jnp.max(s, axis=0, keepdims=True))
            p = jnp.exp2(s - m_new)
            a = jnp.exp2(m - m_new)
            l = a * l + jnp.sum(p, axis=0, keepdims=True)
            acc = a * acc + jnp.dot(vt, p.astype(BF16), preferred_element_type=F32)
            out.append((m_new, l, acc))
        return tuple(out)

    def body(j, carry):
        stats, ss = carry
        nxt = scores(j + 1)
        return consume(j, stats, ss, False), nxt

    init = tuple((jnp.full((1, FLASH_CHAIN), -jnp.inf, F32), jnp.zeros((1, FLASH_CHAIN), F32),
                  jnp.zeros((V_DIM, FLASH_CHAIN), F32)) for _ in range(n_chain))
    stats, ss = lax.fori_loop(0, qi, body, (init, scores(0)))
    for c, (_, l, acc) in enumerate(consume(qi, stats, ss, True)):
        o_ref[c * FLASH_CHAIN:(c + 1) * FLASH_CHAIN, :] = (acc / l).T.astype(BF16)


def _flash(q, kn, kr, v, *, tq=512):
    return pl.pallas_call(
        functools.partial(_flash_kernel, tq=tq),
        grid=(BATCH, N_HEADS, SEQ // tq),
        in_specs=[pl.BlockSpec((None, None, tq, Q_HEAD_COLS), lambda b, h, i: (b, h, i, 0)),
                  pl.BlockSpec((None, None, SEQ, QK_NOPE_DIM), lambda b, h, i: (b, h, 0, 0)),
                  pl.BlockSpec((None, SEQ, 2 * QK_ROPE_DIM), lambda b, h, i: (b, 0, 0)),
                  pl.BlockSpec((None, None, V_DIM, SEQ), lambda b, h, i: (b, h, 0, 0))],
        out_specs=pl.BlockSpec((None, tq, V_DIM), lambda b, h, i: (b, i, h)),
        out_shape=jax.ShapeDtypeStruct((BATCH, SEQ, N_HEADS * V_DIM), BF16),
        compiler_params=pltpu.CompilerParams(
            dimension_semantics=("arbitrary", "arbitrary", "arbitrary"), vmem_limit_bytes=VMEM_LIMIT),
        name="flash",
    )(q, kn, kr, v)


def _attn_out_kernel(a_ref, x_ref, wo_ref, g_ref, b_ref, rw_ref, xo_ref, slab_ref, lg_ref, *, tm):
    ys = [jnp.dot(a_ref[s * SUB_ROWS:(s + 1) * SUB_ROWS, :], wo_ref[...], preferred_element_type=F32)
          for s in range(tm // SUB_ROWS)]
    _ln1_epilogue(ys, x_ref, g_ref, b_ref, rw_ref, xo_ref, slab_ref, lg_ref)


def _attn_out(attn2d, x2d, w_out, ln_g, ln_b, rw, *, tm=512):
    out_specs, out_shape = _ln1_out_specs(tm)
    return pl.pallas_call(
        functools.partial(_attn_out_kernel, tm=tm),
        grid=(N_TOK // tm,),
        in_specs=[pl.BlockSpec((tm, D_MODEL), lambda i: (i, 0)),
                  pl.BlockSpec((tm, D_MODEL), lambda i: (i, 0)),
                  _resident((D_MODEL, D_MODEL)),
                  _resident((1, D_MODEL)), _resident((1, D_MODEL)),
                  _resident((2, N_EXPERTS, D_MODEL))],
        out_specs=out_specs, out_shape=out_shape,
        compiler_params=pltpu.CompilerParams(
            dimension_semantics=("arbitrary",), vmem_limit_bytes=VMEM_LIMIT),
        name="attn_out",
    )(attn2d, x2d, w_out, ln_g, ln_b, rw)


def _rope_table():
    inv_freq = 1.0 / (ROPE_THETA ** (jnp.arange(0, QK_ROPE_DIM, 2, dtype=F32) / QK_ROPE_DIM))
    ang = jnp.arange(SEQ, dtype=F32)[:, None] * inv_freq[None, :]
    cos, sin = jnp.cos(ang), jnp.sin(ang)
    return jnp.concatenate([cos, cos, -sin, sin], axis=1)


def _swap_halves(w):
    half = QK_ROPE_DIM // 2
    return jnp.concatenate([w[..., half:], w[..., :half]], axis=-1)


def kernel(x, a_w_in, a_w_grp, a_scale, a_w_out, kv_w_down, kv_norm, kv_w_uk, kv_w_uv, b_w_dq, b_q_norm, b_w_uq, b_w_out, ln1_g, ln1_b, ln2_g, ln2_b, router_w, router_bias, exp_w_gate, exp_w_up, exp_w_down, sh_w_gate, sh_w_up, sh_w_down):
    x2d = x.reshape(N_TOK, D_MODEL)
    row = lambda v: v.reshape(1, -1)
    moe_args = (router_bias, exp_w_gate, exp_w_up, exp_w_down, sh_w_gate, sh_w_up, sh_w_down, ln2_g, ln2_b)

    pooled = _pool_in(x2d, a_w_in[0].astype(BF16))
    x1, slabs, logits_t = _mix_out(pooled, x2d, a_w_grp[0].astype(BF16), row(a_scale[0]),
                                   a_w_out[0].astype(BF16), row(ln1_g[0]), row(ln1_b[0]),
                                   _split_bf16(router_w[0].T))
    x2 = _moe(x1, slabs, logits_t, 0, *moe_args)

    kr_w = kv_w_down[:, KV_LORA_RANK:]
    w1 = jnp.concatenate([kv_w_down[:, :KV_LORA_RANK], kr_w, _swap_halves(kr_w), b_w_dq[0]],
                         axis=1).astype(BF16)
    uq = b_w_uq[0]
    uq_rope = uq[:, :, QK_NOPE_DIM:]
    w_uq = jnp.concatenate([uq[:, :, :QK_NOPE_DIM], uq_rope, _swap_halves(uq_rope)], axis=2)
    w_uq = w_uq.reshape(Q_LORA_RANK, N_HEADS * Q_HEAD_COLS).astype(BF16)
    q, kn, kr, v = _proj(x2, _rope_table(), w1, row(kv_norm), row(b_q_norm[0]),
                         kv_w_uk.reshape(KV_LORA_RANK, N_HEADS * QK_NOPE_DIM).astype(BF16),
                         kv_w_uv.reshape(KV_LORA_RANK, N_HEADS * V_DIM).T.astype(BF16), w_uq)
    attn = _flash(q, kn, kr, v).reshape(N_TOK, N_HEADS * V_DIM)
    x3, slabs, logits_t = _attn_out(attn, x2, b_w_out[0].astype(BF16), row(ln1_g[1]), row(ln1_b[1]),
                                    _split_bf16(router_w[1].T))
    x4 = _moe(x3, slabs, logits_t, 1, *moe_args)
    return x4.reshape(BATCH, SEQ, D_MODEL)
```

```python
import functools

import jax
import jax.numpy as jnp
from jax import lax
from jax.experimental import pallas as pl
from jax.experimental.pallas import tpu as pltpu

D_MODEL = 2048
BATCH = 2
SEQ = 4096
DEPTH = 2
N_TOK = BATCH * SEQ
ALPHA = (2.0 * DEPTH) ** 0.25
POOL_WINDOWS = (2, 4, 8, 16)
POOL_GROUP_DIM = D_MODEL // len(POOL_WINDOWS)
POOL_HALO = 16
N_HEADS = 16
QK_NOPE_DIM = 128
QK_ROPE_DIM = 64
QK_DIM = QK_NOPE_DIM + QK_ROPE_DIM
V_DIM = 128
Q_LORA_RANK = D_MODEL // 4
KV_LORA_RANK = D_MODEL // 4
ROPE_THETA = 10000.0
N_EXPERTS = 64
TOP_K = 8
N_EXPERT_GROUPS = 8
GROUP_SIZE = N_EXPERTS // N_EXPERT_GROUPS
TOPK_GROUPS = 4
EXPERT_DIM = D_MODEL // 4
ROUTED_SCALE = 2.5
ROW_BLOCK = 128
ROW_SHIFT = ROW_BLOCK.bit_length() - 1
assert 1 << ROW_SHIFT == ROW_BLOCK
LN_EPS = 1e-5
RMS_EPS = 1e-6

N_ASSIGN = N_TOK * TOP_K
N_BLOCKS = N_ASSIGN // ROW_BLOCK + N_EXPERTS
MAP_ROWS = N_BLOCKS + 4
Y_ROWS = N_TOK + 2 * ROW_BLOCK
TOK_BITS = 14
assert N_TOK <= 1 << TOK_BITS and TOP_K * Y_ROWS < 1 << (31 - TOK_BITS)
SLAB_ROWS = 8
SLAB_LANES = 128
HALF_D = D_MODEL // 2
VMEM_LIMIT = 60 * 1024 * 1024

F32 = jnp.float32
BF16 = jnp.bfloat16
I32 = jnp.int32
U32 = jnp.uint32
NT_DIMS = (((1,), (1,)), ((), ()))


def _lo_col(c):
    return 2 * c * SLAB_LANES


def _hi_col(c):
    return (2 * c + 1) * SLAB_LANES


def _pack_rows(z):
    return [pltpu.pack_elementwise([z[:, _lo_col(c):_lo_col(c) + SLAB_LANES],
                                    z[:, _hi_col(c):_hi_col(c) + SLAB_LANES]], packed_dtype=BF16)
            for c in range(SLAB_ROWS)]


def _unpack_lo(w):
    return pltpu.unpack_elementwise(w, index=0, packed_dtype=BF16, unpacked_dtype=F32)


def _unpack_hi(w):
    return pltpu.unpack_elementwise(w, index=1, packed_dtype=BF16, unpacked_dtype=F32)


def _silu(x):
    return x * jax.nn.sigmoid(x)


def _layer_norm(z, g, b):
    mu = jnp.mean(z, axis=-1, keepdims=True)
    zc = z - mu
    var = jnp.mean(zc * zc, axis=-1, keepdims=True)
    return zc * lax.rsqrt(var + LN_EPS) * g + b


def _rms_norm(z, g):
    ms = jnp.mean(z * z, axis=-1, keepdims=True)
    return z * lax.rsqrt(ms + RMS_EPS) * g


def _resident(shape):
    nd = len(shape)
    return pl.BlockSpec(shape, lambda *_: (0,) * nd, pipeline_mode=pl.Buffered(1))


def _pool_in_kernel(x_ref, w_ref, o_ref, tail_ref, *, tm):
    j = pl.program_id(1)

    @pl.when(j == 0)
    def _():
        tail_ref[...] = jnp.zeros_like(tail_ref)

    h = jnp.dot(x_ref[...].astype(BF16), w_ref[...], preferred_element_type=F32)
    ext = jnp.concatenate([tail_ref[...], h], axis=0)
    tail_ref[...] = h[tm - POOL_HALO:, :]
    pos = j * tm + lax.broadcasted_iota(I32, (tm, 1), 0)
    for g, w in enumerate(POOL_WINDOWS):
        c0, c1 = g * POOL_GROUP_DIM, (g + 1) * POOL_GROUP_DIM
        s = ext[:, c0:c1]
        sh = 1
        while sh < w:
            s = s + pltpu.roll(s, sh, axis=0)
            sh *= 2
        inv = 1.0 / jnp.minimum(pos + 1, w).astype(F32)
        o_ref[:, c0:c1] = (s[POOL_HALO:, :] * inv - h[:, c0:c1]).astype(BF16)


def _pool_in(x2d, w_in, *, tm=512):
    nj = SEQ // tm
    return pl.pallas_call(
        functools.partial(_pool_in_kernel, tm=tm),
        grid=(BATCH, nj),
        in_specs=[pl.BlockSpec((tm, D_MODEL), lambda b, j: (b * nj + j, 0)),
                  _resident((D_MODEL, D_MODEL))],
        out_specs=pl.BlockSpec((tm, D_MODEL), lambda b, j: (b * nj + j, 0)),
        out_shape=jax.ShapeDtypeStruct((N_TOK, D_MODEL), BF16),
        scratch_shapes=[pltpu.VMEM((POOL_HALO, D_MODEL), F32)],
        compiler_params=pltpu.CompilerParams(
            dimension_semantics=("arbitrary", "arbitrary"), vmem_limit_bytes=VMEM_LIMIT),
        name="pool_in",
    )(x2d, w_in)


SUB_ROWS = 128


def _ln1_epilogue(ys, x_ref, g_ref, b_ref, rw_ref, xo_ref, slab_ref, lg_ref):
    nt = lambda a, b: lax.dot_general(a, b, NT_DIMS, preferred_element_type=F32)
    for s, y in enumerate(ys):
        r0 = s * SUB_ROWS
        xn = _layer_norm(ALPHA * x_ref[r0:r0 + SUB_ROWS, :] + y, g_ref[...], b_ref[...])
        xo_ref[r0:r0 + SUB_ROWS, :] = xn
        pk = _pack_rows(xn)
        for c in range(SLAB_ROWS):
            slab_ref[pl.ds(r0 * SLAB_ROWS + c, SUB_ROWS, stride=SLAB_ROWS), :] = pk[c]
        x_hi = xn.astype(BF16)
        x_lo = (xn - x_hi.astype(F32)).astype(BF16)
        lg_ref[:, r0:r0 + SUB_ROWS] = nt(rw_ref[0], x_hi) + (nt(rw_ref[0], x_lo) + nt(rw_ref[1], x_hi))


def _mix_out_kernel(p_ref, x_ref, wg_ref, sc_ref, wo_ref, g_ref, b_ref, rw_ref,
                    xo_ref, slab_ref, lg_ref, *, tm):
    ys = []
    for s in range(tm // SUB_ROWS):
        r0 = s * SUB_ROWS
        parts = []
        for g in range(len(POOL_WINDOWS)):
            c0, c1 = g * POOL_GROUP_DIM, (g + 1) * POOL_GROUP_DIM
            parts.append(jnp.dot(p_ref[r0:r0 + SUB_ROWS, c0:c1], wg_ref[g], preferred_element_type=F32))
        mixed = (jnp.concatenate(parts, axis=1) * sc_ref[...]).astype(BF16)
        ys.append(jnp.dot(mixed, wo_ref[...], preferred_element_type=F32))
    _ln1_epilogue(ys, x_ref, g_ref, b_ref, rw_ref, xo_ref, slab_ref, lg_ref)


def _ln1_out_specs(tm):
    return (
        [pl.BlockSpec((tm, D_MODEL), lambda i: (i, 0)),
         pl.BlockSpec((tm * SLAB_ROWS, SLAB_LANES), lambda i: (i, 0)),
         pl.BlockSpec((N_EXPERTS, tm), lambda i: (0, i))],
        [jax.ShapeDtypeStruct((N_TOK, D_MODEL), F32),
         jax.ShapeDtypeStruct((N_TOK * SLAB_ROWS, SLAB_LANES), U32),
         jax.ShapeDtypeStruct((N_EXPERTS, N_TOK), F32)],
    )


def _split_bf16(w):
    hi = w.astype(BF16)
    return jnp.stack([hi, (w - hi.astype(F32)).astype(BF16)])


def _mix_out(pooled, x2d, w_grp, scale, w_out, ln_g, ln_b, rw, *, tm=512):
    out_specs, out_shape = _ln1_out_specs(tm)
    return pl.pallas_call(
        functools.partial(_mix_out_kernel, tm=tm),
        grid=(N_TOK // tm,),
        in_specs=[pl.BlockSpec((tm, D_MODEL), lambda i: (i, 0)),
                  pl.BlockSpec((tm, D_MODEL), lambda i: (i, 0)),
                  _resident(w_grp.shape), _resident((1, D_MODEL)), _resident((D_MODEL, D_MODEL)),
                  _resident((1, D_MODEL)), _resident((1, D_MODEL)),
                  _resident((2, N_EXPERTS, D_MODEL))],
        out_specs=out_specs, out_shape=out_shape,
        compiler_params=pltpu.CompilerParams(
            dimension_semantics=("arbitrary",), vmem_limit_bytes=VMEM_LIMIT),
        name="mix_out",
    )(pooled, x2d, w_grp, scale, w_out, ln_g, ln_b, rw)


def _first_index(hit_src, best, iota, n):
    return jnp.min(jnp.where(hit_src == best, iota, n), axis=0, keepdims=True)


def _route_kernel(lg_ref, bias_ref, pk_ref, gate_ref, cnt_ref, carry_ref, *, tm):
    i = pl.program_id(0)

    @pl.when(i == 0)
    def _():
        carry_ref[...] = jnp.zeros_like(carry_ref)

    neg = -jnp.inf
    scores = jax.nn.sigmoid(lg_ref[...])
    choice = scores + bias_ref[...]
    iota_g = lax.broadcasted_iota(I32, (GROUP_SIZE, tm), 0)
    gscore = []
    for g in range(N_EXPERT_GROUPS):
        c = choice[g * GROUP_SIZE:(g + 1) * GROUP_SIZE, :]
        m1 = jnp.max(c, axis=0, keepdims=True)
        f1 = _first_index(c, m1, iota_g, GROUP_SIZE)
        m2 = jnp.max(jnp.where(iota_g == f1, neg, c), axis=0, keepdims=True)
        gscore.append(m1 + m2)
    gs = jnp.concatenate(gscore, axis=0)
    iota_ng = lax.broadcasted_iota(I32, (N_EXPERT_GROUPS, tm), 0)
    gsel = jnp.zeros((N_EXPERT_GROUPS, tm), jnp.bool_)
    for _ in range(TOPK_GROUPS):
        m = jnp.max(gs, axis=0, keepdims=True)
        hit = iota_ng == _first_index(gs, m, iota_ng, N_EXPERT_GROUPS)
        gsel = gsel | hit
        gs = jnp.where(hit, neg, gs)
    masked = jnp.concatenate(
        [jnp.where(gsel[g:g + 1, :], choice[g * GROUP_SIZE:(g + 1) * GROUP_SIZE, :], neg)
         for g in range(N_EXPERT_GROUPS)], axis=0)
    iota_e = lax.broadcasted_iota(I32, (N_EXPERTS, tm), 0)
    hits, eidx, gates = [], [], []
    sel = jnp.zeros((N_EXPERTS, tm), jnp.bool_)
    for _ in range(TOP_K):
        m = jnp.max(masked, axis=0, keepdims=True)
        f = _first_index(masked, m, iota_e, N_EXPERTS)
        hit = iota_e == f
        hits.append(hit)
        eidx.append(f)
        gates.append(jnp.sum(jnp.where(hit, scores, 0.0), axis=0, keepdims=True))
        masked = jnp.where(hit, neg, masked)
        sel = sel | hit
    gate = jnp.concatenate(gates, axis=0)
    gate_ref[...] = gate / jnp.sum(gate, axis=0, keepdims=True) * ROUTED_SCALE
    selb = jnp.where(sel, 1.0, 0.0).astype(BF16)
    before = (lax.broadcasted_iota(I32, (tm, tm), 0) < lax.broadcasted_iota(I32, (tm, tm), 1))
    rank = jnp.dot(selb, jnp.where(before, 1.0, 0.0).astype(BF16), preferred_element_type=F32)
    rank = (rank + carry_ref[...]).astype(I32)
    rk = [jnp.sum(jnp.where(h, rank, 0), axis=0, keepdims=True) for h in hits]
    pk_ref[...] = (jnp.concatenate(eidx, axis=0) << 16) | jnp.concatenate(rk, axis=0)
    carry_ref[...] = carry_ref[...] + jnp.sum(selb.astype(F32), axis=1, keepdims=True)
    cnt_ref[...] = jnp.broadcast_to(carry_ref[...], cnt_ref.shape).astype(I32)


def _route(logits_t, bias, *, tm=512):
    return pl.pallas_call(
        functools.partial(_route_kernel, tm=tm),
        grid=(N_TOK // tm,),
        in_specs=[pl.BlockSpec((N_EXPERTS, tm), lambda i: (0, i)),
                  _resident((N_EXPERTS, 1))],
        out_specs=[pl.BlockSpec((TOP_K, tm), lambda i: (0, i)),
                   pl.BlockSpec((TOP_K, tm), lambda i: (0, i)),
                   pl.BlockSpec((N_EXPERTS, SLAB_LANES), lambda i: (0, 0))],
        out_shape=[jax.ShapeDtypeStruct((TOP_K, N_TOK), I32),
                   jax.ShapeDtypeStruct((TOP_K, N_TOK), F32),
                   jax.ShapeDtypeStruct((N_EXPERTS, SLAB_LANES), I32)],
        scratch_shapes=[pltpu.VMEM((N_EXPERTS, 1), F32)],
        compiler_params=pltpu.CompilerParams(dimension_semantics=("arbitrary",)),
        name="route",
    )(logits_t, bias)


def _expert_offsets(cnt_ref, ps_ref):
    def offsets(e, acc):
        ps_ref[e] = acc
        return acc + ((cnt_ref[e] + (ROW_BLOCK - 1)) // ROW_BLOCK) * ROW_BLOCK

    return lax.fori_loop(0, N_EXPERTS, offsets, jnp.int32(ROW_BLOCK))


def _positions_kernel(cnt_ref, pk_ref, pos_ref, ps_ref):
    _expert_offsets(cnt_ref, ps_ref)
    pk = pk_ref[...]
    eidx = pk >> 16
    pos = pk & 0xFFFF
    for e in range(N_EXPERTS):
        pos = pos + jnp.where(eidx == e, ps_ref[e], 0)
    pos_ref[...] = pos


def _positions(counts, packed):
    return pl.pallas_call(
        _positions_kernel,
        in_specs=[pl.BlockSpec(memory_space=pltpu.SMEM), pl.BlockSpec(memory_space=pltpu.VMEM)],
        out_specs=pl.BlockSpec(memory_space=pltpu.VMEM),
        out_shape=jax.ShapeDtypeStruct((TOP_K, N_TOK), I32),
        scratch_shapes=[pltpu.SMEM((N_EXPERTS,), I32)],
        name="positions",
    )(counts, packed)


def _finalize_kernel(pos_ref, cnt_ref, tok_ref, dst_ref, blk_ref, first_ref, nxt_ref, nused_ref,
                     srt_ref, ps_ref, *, tt):
    i = pl.program_id(0)
    lane = lax.broadcasted_iota(I32, (1, ROW_BLOCK), 1)

    @pl.when(i == 0)
    def _():
        total = _expert_offsets(cnt_ref, ps_ref)
        nused_ref[0] = total // ROW_BLOCK - 1
        blk_row = lax.broadcasted_iota(I32, srt_ref.shape, 0)
        blk_lane = lax.broadcasted_iota(I32, srt_ref.shape, 1)
        srt_ref[...] = (N_TOK + lax.rem(blk_row + 1, 2) * ROW_BLOCK + blk_lane) << TOK_BITS

        def defaults(b, c):
            blk_ref[b] = N_EXPERTS - 1
            first_ref[b] = 0
            nxt_ref[b] = -1
            return c

        lax.fori_loop(0, N_BLOCKS, defaults, 0)

        def per_expert(j, nxt_e):
            e = N_EXPERTS - 1 - j
            c = cnt_ref[e]
            nb = (c + (ROW_BLOCK - 1)) // ROW_BLOCK
            r0 = ps_ref[e]
            b0 = r0 // ROW_BLOCK - 1

            def blocks(jb, carry):
                blk_ref[b0 + jb] = e
                first_ref[b0 + jb] = jnp.where(jb == 0, 1, 0)
                nxt_ref[b0 + jb] = nxt_e
                return carry

            lax.fori_loop(0, nb, blocks, 0)
            return jnp.where(c > 0, e, nxt_e)

        lax.fori_loop(0, N_EXPERTS, per_expert, jnp.int32(-1))

    def per_token(tl, carry):
        t = i * tt + tl
        entry = t * ((1 << TOK_BITS) + 1)
        for k in range(TOP_K):
            p = pos_ref[t * TOP_K + k]
            val = jnp.full((1, ROW_BLOCK), entry + ((k * Y_ROWS) << TOK_BITS), I32)
            pltpu.store(srt_ref.at[pl.ds(p >> ROW_SHIFT, 1), :], val, mask=lane == (p & (ROW_BLOCK - 1)))
        return carry

    lax.fori_loop(0, tt, per_token, 0)

    @pl.when(i == pl.num_programs(0) - 1)
    def _():
        srt = srt_ref[...]
        tok_ref[...] = (srt & ((1 << TOK_BITS) - 1)) * SLAB_ROWS
        dst_ref[...] = (srt >> TOK_BITS) * SLAB_ROWS


def _finalize(pos_flat, counts, *, tt=512):
    smem = pl.BlockSpec(memory_space=pltpu.SMEM)
    row_map = pl.BlockSpec((MAP_ROWS, ROW_BLOCK), lambda i: (0, 0))
    return pl.pallas_call(
        functools.partial(_finalize_kernel, tt=tt),
        grid=(N_TOK // tt,),
        in_specs=[smem, smem],
        out_specs=[row_map, row_map] + [smem] * 4,
        out_shape=[jax.ShapeDtypeStruct((MAP_ROWS, ROW_BLOCK), I32),
                   jax.ShapeDtypeStruct((MAP_ROWS, ROW_BLOCK), I32),
                   jax.ShapeDtypeStruct((N_BLOCKS,), I32),
                   jax.ShapeDtypeStruct((N_BLOCKS,), I32),
                   jax.ShapeDtypeStruct((N_BLOCKS,), I32),
                   jax.ShapeDtypeStruct((1,), I32)],
        scratch_shapes=[pltpu.VMEM((MAP_ROWS, ROW_BLOCK), I32), pltpu.SMEM((N_EXPERTS,), I32)],
        compiler_params=pltpu.CompilerParams(dimension_semantics=("arbitrary",)),
        name="finalize",
    )(pos_flat, counts)


BLOCK_SLAB_ROWS = ROW_BLOCK * SLAB_ROWS
BLOCKS_PER_STEP = 2
CAST_VREGS = 32


def _cast_weight(src_ref, dst_ref):
    rows, cols = src_ref.shape
    step = CAST_VREGS * 8 * 128 // cols

    def body(r, carry):
        r0 = pl.multiple_of(r * step, step)
        dst_ref[pl.ds(r0, step), :] = src_ref[pl.ds(r0, step), :].astype(BF16)
        return carry

    lax.fori_loop(0, rows // step, body, 0)


def _expert_kernel(blk_ref, first_ref, nxt_ref, nused_ref,
                   xs_ref, wg_hbm, wu_hbm, wd_hbm, tok_hbm, dst_hbm, y_hbm,
                   sg_ref, su_ref, sd_ref, bg_ref, bu_ref, bd_ref,
                   xg_ref, xb_ref, ys_ref, idx_ref, wsem, ysem, isem, *, layer):
    step = pl.program_id(0)
    nused = nused_ref[0]

    def weight_copies(ee):
        return (pltpu.make_async_copy(wg_hbm.at[layer, ee], sg_ref, wsem.at[0]),
                pltpu.make_async_copy(wu_hbm.at[layer, ee], su_ref, wsem.at[1]),
                pltpu.make_async_copy(wd_hbm.at[layer, ee], sd_ref, wsem.at[2]))

    def index_copies(b, slot):
        return (pltpu.make_async_copy(tok_hbm.at[b + 2], idx_ref.at[slot], isem.at[slot]),
                pltpu.make_async_copy(dst_hbm.at[b], idx_ref.at[2 + slot], isem.at[2 + slot]))

    def wait_rows(slot):
        pltpu.make_async_copy(ys_ref.at[slot], y_hbm.at[pl.ds(0, BLOCK_SLAB_ROWS)],
                              ysem.at[slot]).wait()

    def gather_block(idx_row, slot):
        for i in range(ROW_BLOCK):
            xg_ref[pl.ds(i * SLAB_ROWS, SLAB_ROWS), :] = (
                xs_ref[pl.ds(pl.multiple_of(idx_ref[idx_row, i], SLAB_ROWS), SLAB_ROWS), :])
        for c in range(SLAB_ROWS):
            w = xg_ref[pl.ds(c, ROW_BLOCK, stride=SLAB_ROWS), :]
            xb_ref[slot, :, _lo_col(c):_lo_col(c) + SLAB_LANES] = _unpack_lo(w).astype(BF16)
            xb_ref[slot, :, _hi_col(c):_hi_col(c) + SLAB_LANES] = _unpack_hi(w).astype(BF16)

    @pl.when(step == 0)
    def _():
        for cp in weight_copies(blk_ref[0]):
            cp.start()
        ys_ref[...] = jnp.zeros_like(ys_ref)
        zero_copies = [
            pltpu.make_async_copy(
                ys_ref.at[0],
                y_hbm.at[pl.ds((k * Y_ROWS + N_TOK + s * ROW_BLOCK) * SLAB_ROWS, BLOCK_SLAB_ROWS)],
                ysem.at[0])
            for k in range(TOP_K) for s in range(2)]
        for cp in zero_copies:
            cp.start()
        for cp in zero_copies:
            cp.wait()
        first_rows = pltpu.make_async_copy(tok_hbm.at[1], idx_ref.at[1], isem.at[1])
        first_rows.start()
        first_rows.wait()
        gather_block(1, 0)
        for cp in index_copies(0, 0):
            cp.start()

    def one_block(b, cur):
        prev = 1 - cur

        @pl.when(b <= nused)
        def _():
            for cp in index_copies(b, cur):
                cp.wait()

            @pl.when(b < nused)
            def _():
                for cp in index_copies(b + 1, prev):
                    cp.start()

            @pl.when(first_ref[b] == 1)
            def _():
                for cp in weight_copies(blk_ref[b]):
                    cp.wait()
                _cast_weight(sg_ref, bg_ref)
                _cast_weight(su_ref, bu_ref)
                _cast_weight(sd_ref, bd_ref)

                @pl.when(nxt_ref[b] >= 0)
                def _():
                    for cp in weight_copies(nxt_ref[b]):
                        cp.start()

            for i in range(ROW_BLOCK):
                pltpu.make_async_copy(
                    ys_ref.at[prev, pl.ds(i * SLAB_ROWS, SLAB_ROWS)],
                    y_hbm.at[pl.ds(pl.multiple_of(idx_ref[2 + cur, i], SLAB_ROWS), SLAB_ROWS)],
                    ysem.at[prev]).start()

            x = xb_ref[cur]
            hg = jnp.dot(x, bg_ref[...], preferred_element_type=F32)
            hu = jnp.dot(x, bu_ref[...], preferred_element_type=F32)

            @pl.when(b > 0)
            def _():
                wait_rows(cur)

            gather_block(cur, prev)
            act = (_silu(hg) * hu).astype(BF16)
            pk = _pack_rows(jnp.dot(act, bd_ref[...], preferred_element_type=F32))
            for c in range(SLAB_ROWS):
                ys_ref[cur, pl.ds(c, ROW_BLOCK, stride=SLAB_ROWS), :] = pk[c]

            @pl.when(b == nused)
            def _():
                wait_rows(prev)

    for j in range(BLOCKS_PER_STEP):
        one_block(step * BLOCKS_PER_STEP + j, j)


def _experts(blk, first, nxt, nused, tok_rows, dst_rows, slabs, w_gate, w_up, w_down, *, layer):
    grid_spec = pltpu.PrefetchScalarGridSpec(
        num_scalar_prefetch=4,
        grid=(N_BLOCKS // BLOCKS_PER_STEP,),
        in_specs=[pl.BlockSpec((N_TOK * SLAB_ROWS, SLAB_LANES), lambda b, *_: (0, 0),
                               pipeline_mode=pl.Buffered(1))]
                 + [pl.BlockSpec(memory_space=pl.ANY)] * 5,
        out_specs=pl.BlockSpec(memory_space=pl.ANY),
        scratch_shapes=[pltpu.VMEM((D_MODEL, EXPERT_DIM), F32),
                        pltpu.VMEM((D_MODEL, EXPERT_DIM), F32),
                        pltpu.VMEM((EXPERT_DIM, D_MODEL), F32),
                        pltpu.VMEM((D_MODEL, EXPERT_DIM), BF16),
                        pltpu.VMEM((D_MODEL, EXPERT_DIM), BF16),
                        pltpu.VMEM((EXPERT_DIM, D_MODEL), BF16),
                        pltpu.VMEM((BLOCK_SLAB_ROWS, SLAB_LANES), U32),
                        pltpu.VMEM((2, ROW_BLOCK, D_MODEL), BF16),
                        pltpu.VMEM((2, BLOCK_SLAB_ROWS, SLAB_LANES), U32),
                        pltpu.SMEM((4, ROW_BLOCK), I32),
                        pltpu.SemaphoreType.DMA((3,)),
                        pltpu.SemaphoreType.DMA((2,)),
                        pltpu.SemaphoreType.DMA((4,))],
    )
    return pl.pallas_call(
        functools.partial(_expert_kernel, layer=layer),
        grid_spec=grid_spec,
        out_shape=jax.ShapeDtypeStruct((TOP_K * Y_ROWS * SLAB_ROWS, SLAB_LANES), U32),
        compiler_params=pltpu.CompilerParams(
            dimension_semantics=("arbitrary",), vmem_limit_bytes=VMEM_LIMIT),
        name="experts",
    )(blk, first, nxt, nused, slabs, w_gate, w_up, w_down, tok_rows, dst_rows)


def _combine_kernel(x_ref, y_ref, gate_ref, sg_ref, su_ref, sd_ref, g_ref, b_ref, o_ref, r_ref, *, tm):
    x = x_ref[...]
    xb = x.astype(BF16)
    hg = jnp.dot(xb, sg_ref[...], preferred_element_type=F32)
    hu = jnp.dot(xb, su_ref[...], preferred_element_type=F32)
    shared = jnp.dot((_silu(hg) * hu).astype(BF16), sd_ref[...], preferred_element_type=F32)
    gate = gate_ref[...]
    for c in range(SLAB_ROWS):
        lo = jnp.zeros((tm, SLAB_LANES), F32)
        hi = jnp.zeros((tm, SLAB_LANES), F32)
        for k in range(TOP_K):
            w = y_ref[k, pl.ds(c, tm, stride=SLAB_ROWS), :]
            gk = gate[:, k:k + 1]
            lo = lo + gk * _unpack_lo(w)
            hi = hi + gk * _unpack_hi(w)
        r_ref[:, _lo_col(c):_lo_col(c) + SLAB_LANES] = lo
        r_ref[:, _hi_col(c):_hi_col(c) + SLAB_LANES] = hi
    z = ALPHA * x + (r_ref[...] + shared)
    o_ref[...] = _layer_norm(z, g_ref[...], b_ref[...])


def _combine(x2d, y8, gate_tk, s_gate, s_up, s_down, ln_g, ln_b, *, tm=256):
    return pl.pallas_call(
        functools.partial(_combine_kernel, tm=tm),
        grid=(N_TOK // tm,),
        in_specs=[pl.BlockSpec((tm, D_MODEL), lambda i: (i, 0)),
                  pl.BlockSpec((TOP_K, tm * SLAB_ROWS, SLAB_LANES), lambda i: (0, i, 0)),
                  pl.BlockSpec((tm, TOP_K), lambda i: (i, 0)),
                  _resident((D_MODEL, EXPERT_DIM)), _resident((D_MODEL, EXPERT_DIM)),
                  _resident((EXPERT_DIM, D_MODEL)),
                  _resident((1, D_MODEL)), _resident((1, D_MODEL))],
        out_specs=pl.BlockSpec((tm, D_MODEL), lambda i: (i, 0)),
        out_shape=jax.ShapeDtypeStruct((N_TOK, D_MODEL), F32),
        scratch_shapes=[pltpu.VMEM((tm, D_MODEL), F32)],
        compiler_params=pltpu.CompilerParams(
            dimension_semantics=("arbitrary",), vmem_limit_bytes=VMEM_LIMIT),
        name="combine",
    )(x2d, y8, gate_tk, s_gate, s_up, s_down, ln_g, ln_b)


def _moe(x_f32, slabs, logits_t, layer, router_bias, exp_w_gate, exp_w_up, exp_w_down,
         s_gate, s_up, s_down, ln_g, ln_b):
    packed, gate8, counts = _route(logits_t, router_bias[layer].reshape(N_EXPERTS, 1))
    counts = counts[:, 0]
    tok_rows, dst_rows, blk, first, nxt, nused = _finalize(
        _positions(counts, packed).T.reshape(-1), counts)
    y8 = _experts(blk, first, nxt, nused, tok_rows, dst_rows, slabs, exp_w_gate, exp_w_up, exp_w_down,
                  layer=layer)
    y8 = y8.reshape(TOP_K, Y_ROWS * SLAB_ROWS, SLAB_LANES)
    return _combine(x_f32, y8, gate8.T, s_gate[layer].astype(BF16), s_up[layer].astype(BF16),
                    s_down[layer].astype(BF16), ln_g[layer].reshape(1, D_MODEL),
                    ln_b[layer].reshape(1, D_MODEL))


W1_COLS = KV_LORA_RANK + 2 * QK_ROPE_DIM + Q_LORA_RANK
Q_HEAD_COLS = QK_NOPE_DIM + 2 * QK_ROPE_DIM
LOG2_E = 1.4426950408889634
Q_SCALE = QK_DIM ** -0.5 * LOG2_E
FLASH_CHAIN = 256


def _rope_chunk(chunk, cs):
    u = chunk * cs
    return u + pltpu.roll(u, QK_ROPE_DIM, axis=1)


def _proj_kernel(x_ref, cs_ref, w1_ref, kvn_ref, qn_ref, wuk_ref, wuv_ref, wuq_ref,
                 q_ref, kn_ref, kr_ref, v_ref, *, tm):
    xb = x_ref[...].astype(BF16)
    h = jnp.dot(xb, w1_ref[...], preferred_element_type=F32)
    cs = cs_ref[...]
    c = _rms_norm(h[:, :KV_LORA_RANK], kvn_ref[...]).astype(BF16)
    kr = _rope_chunk(h[:, KV_LORA_RANK:KV_LORA_RANK + 2 * QK_ROPE_DIM], cs)
    lane = lax.broadcasted_iota(I32, (tm, 2 * QK_ROPE_DIM), 1)
    kr_ref[...] = jnp.where(lane < QK_ROPE_DIM, kr, 0.0).astype(BF16)
    cq = _rms_norm(h[:, KV_LORA_RANK + 2 * QK_ROPE_DIM:], qn_ref[...]).astype(BF16)
    kn = jnp.dot(c, wuk_ref[...], preferred_element_type=F32)
    vt = lax.dot_general(wuv_ref[...], c, NT_DIMS, preferred_element_type=F32)
    q = jnp.dot(cq, wuq_ref[...], preferred_element_type=F32)
    for hd in range(N_HEADS):
        kn_ref[hd] = kn[:, hd * QK_NOPE_DIM:(hd + 1) * QK_NOPE_DIM].astype(BF16)
        v_ref[hd] = vt[hd * V_DIM:(hd + 1) * V_DIM, :].astype(BF16)
        q0 = hd * Q_HEAD_COLS
        q_ref[hd, :, :QK_NOPE_DIM] = (q[:, q0:q0 + QK_NOPE_DIM] * Q_SCALE).astype(BF16)
        q_ref[hd, :, QK_NOPE_DIM:] = (
            _rope_chunk(q[:, q0 + QK_NOPE_DIM:q0 + Q_HEAD_COLS], cs) * Q_SCALE).astype(BF16)


def _proj(x2d, cs, w1, kv_norm, q_norm, w_uk, w_uv, w_uq, *, tm=256):
    nj = SEQ // tm
    head_spec = lambda d: pl.BlockSpec((None, N_HEADS, tm, d), lambda b, j: (b, 0, j, 0))
    return pl.pallas_call(
        functools.partial(_proj_kernel, tm=tm),
        grid=(BATCH, nj),
        in_specs=[pl.BlockSpec((tm, D_MODEL), lambda b, j: (b * nj + j, 0)),
                  pl.BlockSpec((tm, 2 * QK_ROPE_DIM), lambda b, j: (j, 0)),
                  _resident(w1.shape), _resident((1, KV_LORA_RANK)), _resident((1, Q_LORA_RANK)),
                  _resident(w_uk.shape), _resident(w_uv.shape), _resident(w_uq.shape)],
        out_specs=[head_spec(Q_HEAD_COLS), head_spec(QK_NOPE_DIM),
                   pl.BlockSpec((None, tm, 2 * QK_ROPE_DIM), lambda b, j: (b, j, 0)),
                   pl.BlockSpec((None, N_HEADS, V_DIM, tm), lambda b, j: (b, 0, 0, j))],
        out_shape=[jax.ShapeDtypeStruct((BATCH, N_HEADS, SEQ, Q_HEAD_COLS), BF16),
                   jax.ShapeDtypeStruct((BATCH, N_HEADS, SEQ, QK_NOPE_DIM), BF16),
                   jax.ShapeDtypeStruct((BATCH, SEQ, 2 * QK_ROPE_DIM), BF16),
                   jax.ShapeDtypeStruct((BATCH, N_HEADS, V_DIM, SEQ), BF16)],
        compiler_params=pltpu.CompilerParams(
            dimension_semantics=("arbitrary", "arbitrary"), vmem_limit_bytes=VMEM_LIMIT),
        name="proj",
    )(x2d, cs, w1, kv_norm, q_norm, w_uk, w_uv, w_uq)


def _flash_kernel(q_ref, kn_ref, kr_ref, vt_ref, o_ref, *, tq):
    qi = pl.program_id(2)
    n_chain = tq // FLASH_CHAIN
    qs = [q_ref[c * FLASH_CHAIN:(c + 1) * FLASH_CHAIN, :] for c in range(n_chain)]

    def scores(j):
        k0 = pl.multiple_of(j * tq, tq)
        k = jnp.concatenate([kn_ref[pl.ds(k0, tq), :], kr_ref[pl.ds(k0, tq), :]], axis=1)
        return tuple(lax.dot_general(k, qc, NT_DIMS, preferred_element_type=F32) for qc in qs)

    def consume(j, stats, ss, diagonal):
        vt = vt_ref[:, pl.ds(pl.multiple_of(j * tq, tq), tq)]
        out = []
        for c, ((m, l, acc), s) in enumerate(zip(stats, ss)):
            if diagonal:
                key = lax.broadcasted_iota(I32, (tq, FLASH_CHAIN), 0)
                qry = lax.broadcasted_iota(I32, (tq, FLASH_CHAIN), 1) + c * FLASH_CHAIN
                s = jnp.where(key <= qry, s, -jnp.inf)
            m_new = jnp.maximum(m, jnp.max(s, axis=0, keepdims=True))
            p = jnp.exp2(s - m_new)
            a = jnp.exp2(m - m_new)
            l = a * l + jnp.sum(p, axis=0, keepdims=True)
            acc = a * acc + jnp.dot(vt, p.astype(BF16), preferred_element_type=F32)
            out.append((m_new, l, acc))
        return tuple(out)

    def body(j, carry):
        stats, ss = carry
        nxt = scores(j + 1)
        return consume(j, stats, ss, False), nxt

    init = tuple((jnp.full((1, FLASH_CHAIN), -jnp.inf, F32), jnp.zeros((1, FLASH_CHAIN), F32),
                  jnp.zeros((V_DIM, FLASH_CHAIN), F32)) for _ in range(n_chain))
    stats, ss = lax.fori_loop(0, qi, body, (init, scores(0)))
    for c, (_, l, acc) in enumerate(consume(qi, stats, ss, True)):
        o_ref[c * FLASH_CHAIN:(c + 1) * FLASH_CHAIN, :] = (acc / l).T.astype(BF16)


def _flash(q, kn, kr, v, *, tq=512):
    return pl.pallas_call(
        functools.partial(_flash_kernel, tq=tq),
        grid=(BATCH, N_HEADS, SEQ // tq),
        in_specs=[pl.BlockSpec((None, None, tq, Q_HEAD_COLS), lambda b, h, i: (b, h, i, 0)),
                  pl.BlockSpec((None, None, SEQ, QK_NOPE_DIM), lambda b, h, i: (b, h, 0, 0)),
                  pl.BlockSpec((None, SEQ, 2 * QK_ROPE_DIM), lambda b, h, i: (b, 0, 0)),
                  pl.BlockSpec((None, None, V_DIM, SEQ), lambda b, h, i: (b, h, 0, 0))],
        out_specs=pl.BlockSpec((None, tq, V_DIM), lambda b, h, i: (b, i, h)),
        out_shape=jax.ShapeDtypeStruct((BATCH, SEQ, N_HEADS * V_DIM), BF16),
        compiler_params=pltpu.CompilerParams(
            dimension_semantics=("arbitrary", "arbitrary", "arbitrary"), vmem_limit_bytes=VMEM_LIMIT),
        name="flash",
    )(q, kn, kr, v)


def _attn_out_kernel(a_ref, x_ref, wo_ref, g_ref, b_ref, rw_ref, xo_ref, slab_ref, lg_ref, *, tm):
    ys = [jnp.dot(a_ref[s * SUB_ROWS:(s + 1) * SUB_ROWS, :], wo_ref[...], preferred_element_type=F32)
          for s in range(tm // SUB_ROWS)]
    _ln1_epilogue(ys, x_ref, g_ref, b_ref, rw_ref, xo_ref, slab_ref, lg_ref)


def _attn_out(attn2d, x2d, w_out, ln_g, ln_b, rw, *, tm=512):
    out_specs, out_shape = _ln1_out_specs(tm)
    return pl.pallas_call(
        functools.partial(_attn_out_kernel, tm=tm),
        grid=(N_TOK // tm,),
        in_specs=[pl.BlockSpec((tm, D_MODEL), lambda i: (i, 0)),
                  pl.BlockSpec((tm, D_MODEL), lambda i: (i, 0)),
                  _resident((D_MODEL, D_MODEL)),
                  _resident((1, D_MODEL)), _resident((1, D_MODEL)),
                  _resident((2, N_EXPERTS, D_MODEL))],
        out_specs=out_specs, out_shape=out_shape,
        compiler_params=pltpu.CompilerParams(
            dimension_semantics=("arbitrary",), vmem_limit_bytes=VMEM_LIMIT),
        name="attn_out",
    )(attn2d, x2d, w_out, ln_g, ln_b, rw)


def _rope_table():
    inv_freq = 1.0 / (ROPE_THETA ** (jnp.arange(0, QK_ROPE_DIM, 2, dtype=F32) / QK_ROPE_DIM))
    ang = jnp.arange(SEQ, dtype=F32)[:, None] * inv_freq[None, :]
    cos, sin = jnp.cos(ang), jnp.sin(ang)
    return jnp.concatenate([cos, cos, -sin, sin], axis=1)


def _swap_halves(w):
    half = QK_ROPE_DIM // 2
    return jnp.concatenate([w[..., half:], w[..., :half]], axis=-1)


def kernel(x, a_w_in, a_w_grp, a_scale, a_w_out, kv_w_down, kv_norm, kv_w_uk, kv_w_uv, b_w_dq, b_q_norm, b_w_uq, b_w_out, ln1_g, ln1_b, ln2_g, ln2_b, router_w, router_bias, exp_w_gate, exp_w_up, exp_w_down, sh_w_gate, sh_w_up, sh_w_down):
    x2d = x.reshape(N_TOK, D_MODEL)
    row = lambda v: v.reshape(1, -1)
    moe_args = (router_bias, exp_w_gate, exp_w_up, exp_w_down, sh_w_gate, sh_w_up, sh_w_down, ln2_g, ln2_b)

    pooled = _pool_in(x2d, a_w_in[0].astype(BF16))
    x1, slabs, logits_t = _mix_out(pooled, x2d, a_w_grp[0].astype(BF16), row(a_scale[0]),
                                   a_w_out[0].astype(BF16), row(ln1_g[0]), row(ln1_b[0]),
                                   _split_bf16(router_w[0].T))
    x2 = _moe(x1, slabs, logits_t, 0, *moe_args)

    kr_w = kv_w_down[:, KV_LORA_RANK:]
    w1 = jnp.concatenate([kv_w_down[:, :KV_LORA_RANK], kr_w, _swap_halves(kr_w), b_w_dq[0]],
                         axis=1).astype(BF16)
    uq = b_w_uq[0]
    uq_rope = uq[:, :, QK_NOPE_DIM:]
    w_uq = jnp.concatenate([uq[:, :, :QK_NOPE_DIM], uq_rope, _swap_halves(uq_rope)], axis=2)
    w_uq = w_uq.reshape(Q_LORA_RANK, N_HEADS * Q_HEAD_COLS).astype(BF16)
    q, kn, kr, v = _proj(x2, _rope_table(), w1, row(kv_norm), row(b_q_norm[0]),
                         kv_w_uk.reshape(KV_LORA_RANK, N_HEADS * QK_NOPE_DIM).astype(BF16),
                         kv_w_uv.reshape(KV_LORA_RANK, N_HEADS * V_DIM).T.astype(BF16), w_uq)
    attn = _flash(q, kn, kr, v).reshape(N_TOK, N_HEADS * V_DIM)
    x3, slabs, logits_t = _attn_out(attn, x2, b_w_out[0].astype(BF16), row(ln1_g[1]), row(ln1_b[1]),
                                    _split_bf16(router_w[1].T))
    x4 = _moe(x3, slabs, logits_t, 1, *moe_args)
    return x4.reshape(BATCH, SEQ, D_MODEL)
```

```python
import functools

import jax
import jax.numpy as jnp
from jax import lax
from jax.experimental import pallas as pl
from jax.experimental.pallas import tpu as pltpu

D_MODEL = 2048
BATCH = 2
SEQ = 4096
DEPTH = 2
N_TOK = BATCH * SEQ
ALPHA = (2.0 * DEPTH) ** 0.25
POOL_WINDOWS = (2, 4, 8, 16)
POOL_GROUP_DIM = D_MODEL // len(POOL_WINDOWS)
POOL_HALO = 16
N_HEADS = 16
QK_NOPE_DIM = 128
QK_ROPE_DIM = 64
QK_DIM = QK_NOPE_DIM + QK_ROPE_DIM
V_DIM = 128
Q_LORA_RANK = D_MODEL // 4
KV_LORA_RANK = D_MODEL // 4
ROPE_THETA = 10000.0
N_EXPERTS = 64
TOP_K = 8
N_EXPERT_GROUPS = 8
GROUP_SIZE = N_EXPERTS // N_EXPERT_GROUPS
TOPK_GROUPS = 4
EXPERT_DIM = D_MODEL // 4
ROUTED_SCALE = 2.5
ROW_BLOCK = 256
MAP_LANES = 128
MAP_SHIFT = MAP_LANES.bit_length() - 1
ROWS_PER_BLOCK = ROW_BLOCK // MAP_LANES
assert 1 << MAP_SHIFT == MAP_LANES and ROWS_PER_BLOCK * MAP_LANES == ROW_BLOCK
LN_EPS = 1e-5
RMS_EPS = 1e-6

N_ASSIGN = N_TOK * TOP_K
N_BLOCKS = N_ASSIGN // ROW_BLOCK + N_EXPERTS
MAP_ROWS = (N_BLOCKS + 4) * ROWS_PER_BLOCK
Y_ROWS = N_TOK + 2 * ROW_BLOCK
TOK_BITS = 14
assert N_TOK <= 1 << TOK_BITS and TOP_K * Y_ROWS < 1 << (31 - TOK_BITS)
SLAB_ROWS = 8
SLAB_LANES = 128
HALF_D = D_MODEL // 2
VMEM_LIMIT = 60 * 1024 * 1024

F32 = jnp.float32
BF16 = jnp.bfloat16
I32 = jnp.int32
U32 = jnp.uint32
NT_DIMS = (((1,), (1,)), ((), ()))


def _lo_col(c):
    return 2 * c * SLAB_LANES


def _hi_col(c):
    return (2 * c + 1) * SLAB_LANES


def _pack_rows(z):
    return [pltpu.pack_elementwise([z[:, _lo_col(c):_lo_col(c) + SLAB_LANES],
                                    z[:, _hi_col(c):_hi_col(c) + SLAB_LANES]], packed_dtype=BF16)
            for c in range(SLAB_ROWS)]


def _unpack_lo(w):
    return pltpu.unpack_elementwise(w, index=0, packed_dtype=BF16, unpacked_dtype=F32)


def _unpack_hi(w):
    return pltpu.unpack_elementwise(w, index=1, packed_dtype=BF16, unpacked_dtype=F32)


def _silu(x):
    return x * jax.nn.sigmoid(x)


def _layer_norm(z, g, b):
    mu = jnp.mean(z, axis=-1, keepdims=True)
    zc = z - mu
    var = jnp.mean(zc * zc, axis=-1, keepdims=True)
    return zc * lax.rsqrt(var + LN_EPS) * g + b


def _rms_norm(z, g):
    ms = jnp.mean(z * z, axis=-1, keepdims=True)
    return z * lax.rsqrt(ms + RMS_EPS) * g


def _resident(shape):
    nd = len(shape)
    return pl.BlockSpec(shape, lambda *_: (0,) * nd, pipeline_mode=pl.Buffered(1))


def _pool_in_kernel(x_ref, w_ref, o_ref, tail_ref, *, tm):
    j = pl.program_id(1)

    @pl.when(j == 0)
    def _():
        tail_ref[...] = jnp.zeros_like(tail_ref)

    h = jnp.dot(x_ref[...].astype(BF16), w_ref[...], preferred_element_type=F32)
    ext = jnp.concatenate([tail_ref[...], h], axis=0)
    tail_ref[...] = h[tm - POOL_HALO:, :]
    pos = j * tm + lax.broadcasted_iota(I32, (tm, 1), 0)
    for g, w in enumerate(POOL_WINDOWS):
        c0, c1 = g * POOL_GROUP_DIM, (g + 1) * POOL_GROUP_DIM
        s = ext[:, c0:c1]
        sh = 1
        while sh < w:
            s = s + pltpu.roll(s, sh, axis=0)
            sh *= 2
        inv = 1.0 / jnp.minimum(pos + 1, w).astype(F32)
        o_ref[:, c0:c1] = (s[POOL_HALO:, :] * inv - h[:, c0:c1]).astype(BF16)


def _pool_in(x2d, w_in, *, tm=512):
    nj = SEQ // tm
    return pl.pallas_call(
        functools.partial(_pool_in_kernel, tm=tm),
        grid=(BATCH, nj),
        in_specs=[pl.BlockSpec((tm, D_MODEL), lambda b, j: (b * nj + j, 0)),
                  _resident((D_MODEL, D_MODEL))],
        out_specs=pl.BlockSpec((tm, D_MODEL), lambda b, j: (b * nj + j, 0)),
        out_shape=jax.ShapeDtypeStruct((N_TOK, D_MODEL), BF16),
        scratch_shapes=[pltpu.VMEM((POOL_HALO, D_MODEL), F32)],
        compiler_params=pltpu.CompilerParams(
            dimension_semantics=("arbitrary", "arbitrary"), vmem_limit_bytes=VMEM_LIMIT),
        name="pool_in",
    )(x2d, w_in)


SUB_ROWS = 128


def _ln1_epilogue(ys, x_ref, g_ref, b_ref, rw_ref, xo_ref, slab_ref, lg_ref):
    nt = lambda a, b: lax.dot_general(a, b, NT_DIMS, preferred_element_type=F32)
    for s, y in enumerate(ys):
        r0 = s * SUB_ROWS
        xn = _layer_norm(ALPHA * x_ref[r0:r0 + SUB_ROWS, :] + y, g_ref[...], b_ref[...])
        xo_ref[r0:r0 + SUB_ROWS, :] = xn
        pk = _pack_rows(xn)
        for c in range(SLAB_ROWS):
            slab_ref[pl.ds(r0 * SLAB_ROWS + c, SUB_ROWS, stride=SLAB_ROWS), :] = pk[c]
        x_hi = xn.astype(BF16)
        x_lo = (xn - x_hi.astype(F32)).astype(BF16)
        lg_ref[:, r0:r0 + SUB_ROWS] = nt(rw_ref[0], x_hi) + (nt(rw_ref[0], x_lo) + nt(rw_ref[1], x_hi))


def _mix_out_kernel(p_ref, x_ref, wg_ref, sc_ref, wo_ref, g_ref, b_ref, rw_ref,
                    xo_ref, slab_ref, lg_ref, *, tm):
    ys = []
    for s in range(tm // SUB_ROWS):
        r0 = s * SUB_ROWS
        parts = []
        for g in range(len(POOL_WINDOWS)):
            c0, c1 = g * POOL_GROUP_DIM, (g + 1) * POOL_GROUP_DIM
            parts.append(jnp.dot(p_ref[r0:r0 + SUB_ROWS, c0:c1], wg_ref[g], preferred_element_type=F32))
        mixed = (jnp.concatenate(parts, axis=1) * sc_ref[...]).astype(BF16)
        ys.append(jnp.dot(mixed, wo_ref[...], preferred_element_type=F32))
    _ln1_epilogue(ys, x_ref, g_ref, b_ref, rw_ref, xo_ref, slab_ref, lg_ref)


def _ln1_out_specs(tm):
    return (
        [pl.BlockSpec((tm, D_MODEL), lambda i: (i, 0)),
         pl.BlockSpec((tm * SLAB_ROWS, SLAB_LANES), lambda i: (i, 0)),
         pl.BlockSpec((N_EXPERTS, tm), lambda i: (0, i))],
        [jax.ShapeDtypeStruct((N_TOK, D_MODEL), F32),
         jax.ShapeDtypeStruct((N_TOK * SLAB_ROWS, SLAB_LANES), U32),
         jax.ShapeDtypeStruct((N_EXPERTS, N_TOK), F32)],
    )


def _split_bf16(w):
    hi = w.astype(BF16)
    return jnp.stack([hi, (w - hi.astype(F32)).astype(BF16)])


def _mix_out(pooled, x2d, w_grp, scale, w_out, ln_g, ln_b, rw, *, tm=512):
    out_specs, out_shape = _ln1_out_specs(tm)
    return pl.pallas_call(
        functools.partial(_mix_out_kernel, tm=tm),
        grid=(N_TOK // tm,),
        in_specs=[pl.BlockSpec((tm, D_MODEL), lambda i: (i, 0)),
                  pl.BlockSpec((tm, D_MODEL), lambda i: (i, 0)),
                  _resident(w_grp.shape), _resident((1, D_MODEL)), _resident((D_MODEL, D_MODEL)),
                  _resident((1, D_MODEL)), _resident((1, D_MODEL)),
                  _resident((2, N_EXPERTS, D_MODEL))],
        out_specs=out_specs, out_shape=out_shape,
        compiler_params=pltpu.CompilerParams(
            dimension_semantics=("arbitrary",), vmem_limit_bytes=VMEM_LIMIT),
        name="mix_out",
    )(pooled, x2d, w_grp, scale, w_out, ln_g, ln_b, rw)


def _first_index(hit_src, best, iota, n):
    return jnp.min(jnp.where(hit_src == best, iota, n), axis=0, keepdims=True)


def _route_kernel(lg_ref, bias_ref, pk_ref, gate_ref, cnt_ref, carry_ref, *, tm):
    i = pl.program_id(0)

    @pl.when(i == 0)
    def _():
        carry_ref[...] = jnp.zeros_like(carry_ref)

    neg = -jnp.inf
    scores = jax.nn.sigmoid(lg_ref[...])
    choice = scores + bias_ref[...]
    iota_g = lax.broadcasted_iota(I32, (GROUP_SIZE, tm), 0)
    gscore = []
    for g in range(N_EXPERT_GROUPS):
        c = choice[g * GROUP_SIZE:(g + 1) * GROUP_SIZE, :]
        m1 = jnp.max(c, axis=0, keepdims=True)
        f1 = _first_index(c, m1, iota_g, GROUP_SIZE)
        m2 = jnp.max(jnp.where(iota_g == f1, neg, c), axis=0, keepdims=True)
        gscore.append(m1 + m2)
    gs = jnp.concatenate(gscore, axis=0)
    iota_ng = lax.broadcasted_iota(I32, (N_EXPERT_GROUPS, tm), 0)
    gsel = jnp.zeros((N_EXPERT_GROUPS, tm), jnp.bool_)
    for _ in range(TOPK_GROUPS):
        m = jnp.max(gs, axis=0, keepdims=True)
        hit = iota_ng == _first_index(gs, m, iota_ng, N_EXPERT_GROUPS)
        gsel = gsel | hit
        gs = jnp.where(hit, neg, gs)
    masked = jnp.concatenate(
        [jnp.where(gsel[g:g + 1, :], choice[g * GROUP_SIZE:(g + 1) * GROUP_SIZE, :], neg)
         for g in range(N_EXPERT_GROUPS)], axis=0)
    iota_e = lax.broadcasted_iota(I32, (N_EXPERTS, tm), 0)
    hits, eidx, gates = [], [], []
    sel = jnp.zeros((N_EXPERTS, tm), jnp.bool_)
    for _ in range(TOP_K):
        m = jnp.max(masked, axis=0, keepdims=True)
        f = _first_index(masked, m, iota_e, N_EXPERTS)
        hit = iota_e == f
        hits.append(hit)
        eidx.append(f)
        gates.append(jnp.sum(jnp.where(hit, scores, 0.0), axis=0, keepdims=True))
        masked = jnp.where(hit, neg, masked)
        sel = sel | hit
    gate = jnp.concatenate(gates, axis=0)
    gate_ref[...] = gate / jnp.sum(gate, axis=0, keepdims=True) * ROUTED_SCALE
    selb = jnp.where(sel, 1.0, 0.0).astype(BF16)
    before = (lax.broadcasted_iota(I32, (tm, tm), 0) < lax.broadcasted_iota(I32, (tm, tm), 1))
    rank = jnp.dot(selb, jnp.where(before, 1.0, 0.0).astype(BF16), preferred_element_type=F32)
    rank = (rank + carry_ref[...]).astype(I32)
    rk = [jnp.sum(jnp.where(h, rank, 0), axis=0, keepdims=True) for h in hits]
    pk_ref[...] = (jnp.concatenate(eidx, axis=0) << 16) | jnp.concatenate(rk, axis=0)
    carry_ref[...] = carry_ref[...] + jnp.sum(selb.astype(F32), axis=1, keepdims=True)
    cnt_ref[...] = jnp.broadcast_to(carry_ref[...], cnt_ref.shape).astype(I32)


def _route(logits_t, bias, *, tm=512):
    return pl.pallas_call(
        functools.partial(_route_kernel, tm=tm),
        grid=(N_TOK // tm,),
        in_specs=[pl.BlockSpec((N_EXPERTS, tm), lambda i: (0, i)),
                  _resident((N_EXPERTS, 1))],
        out_specs=[pl.BlockSpec((TOP_K, tm), lambda i: (0, i)),
                   pl.BlockSpec((TOP_K, tm), lambda i: (0, i)),
                   pl.BlockSpec((N_EXPERTS, SLAB_LANES), lambda i: (0, 0))],
        out_shape=[jax.ShapeDtypeStruct((TOP_K, N_TOK), I32),
                   jax.ShapeDtypeStruct((TOP_K, N_TOK), F32),
                   jax.ShapeDtypeStruct((N_EXPERTS, SLAB_LANES), I32)],
        scratch_shapes=[pltpu.VMEM((N_EXPERTS, 1), F32)],
        compiler_params=pltpu.CompilerParams(dimension_semantics=("arbitrary",)),
        name="route",
    )(logits_t, bias)


def _expert_offsets(cnt_ref, ps_ref):
    def offsets(e, acc):
        ps_ref[e] = acc
        return acc + ((cnt_ref[e] + (ROW_BLOCK - 1)) // ROW_BLOCK) * ROW_BLOCK

    return lax.fori_loop(0, N_EXPERTS, offsets, jnp.int32(ROW_BLOCK))


def _positions_kernel(cnt_ref, pk_ref, pos_ref, ps_ref):
    _expert_offsets(cnt_ref, ps_ref)
    pk = pk_ref[...]
    eidx = pk >> 16
    pos = pk & 0xFFFF
    for e in range(N_EXPERTS):
        pos = pos + jnp.where(eidx == e, ps_ref[e], 0)
    pos_ref[...] = pos


def _positions(counts, packed):
    return pl.pallas_call(
        _positions_kernel,
        in_specs=[pl.BlockSpec(memory_space=pltpu.SMEM), pl.BlockSpec(memory_space=pltpu.VMEM)],
        out_specs=pl.BlockSpec(memory_space=pltpu.VMEM),
        out_shape=jax.ShapeDtypeStruct((TOP_K, N_TOK), I32),
        scratch_shapes=[pltpu.SMEM((N_EXPERTS,), I32)],
        name="positions",
    )(counts, packed)


def _finalize_kernel(pos_ref, cnt_ref, tok_ref, dst_ref, blk_ref, first_ref, nxt_ref, nused_ref,
                     srt_ref, ps_ref, *, tt):
    i = pl.program_id(0)
    lane = lax.broadcasted_iota(I32, (1, MAP_LANES), 1)

    @pl.when(i == 0)
    def _():
        total = _expert_offsets(cnt_ref, ps_ref)
        nused_ref[0] = total // ROW_BLOCK - 1
        entry = (lax.broadcasted_iota(I32, srt_ref.shape, 0) * MAP_LANES
                 + lax.broadcasted_iota(I32, srt_ref.shape, 1))
        srt_ref[...] = (N_TOK + lax.rem(entry // ROW_BLOCK + 1, 2) * ROW_BLOCK
                        + lax.rem(entry, ROW_BLOCK)) << TOK_BITS

        def defaults(b, c):
            blk_ref[b] = N_EXPERTS - 1
            first_ref[b] = 0
            nxt_ref[b] = -1
            return c

        lax.fori_loop(0, N_BLOCKS, defaults, 0)

        def per_expert(j, nxt_e):
            e = N_EXPERTS - 1 - j
            c = cnt_ref[e]
            nb = (c + (ROW_BLOCK - 1)) // ROW_BLOCK
            r0 = ps_ref[e]
            b0 = r0 // ROW_BLOCK - 1

            def blocks(jb, carry):
                blk_ref[b0 + jb] = e
                first_ref[b0 + jb] = jnp.where(jb == 0, 1, 0)
                nxt_ref[b0 + jb] = nxt_e
                return carry

            lax.fori_loop(0, nb, blocks, 0)
            return jnp.where(c > 0, e, nxt_e)

        lax.fori_loop(0, N_EXPERTS, per_expert, jnp.int32(-1))

    def per_token(tl, carry):
        t = i * tt + tl
        entry = t * ((1 << TOK_BITS) + 1)
        for k in range(TOP_K):
            p = pos_ref[t * TOP_K + k]
            val = jnp.full((1, MAP_LANES), entry + ((k * Y_ROWS) << TOK_BITS), I32)
            pltpu.store(srt_ref.at[pl.ds(p >> MAP_SHIFT, 1), :], val, mask=lane == (p & (MAP_LANES - 1)))
        return carry

    lax.fori_loop(0, tt, per_token, 0)

    @pl.when(i == pl.num_programs(0) - 1)
    def _():
        srt = srt_ref[...]
        tok_ref[...] = (srt & ((1 << TOK_BITS) - 1)) * SLAB_ROWS
        dst_ref[...] = (srt >> TOK_BITS) * SLAB_ROWS


def _finalize(pos_flat, counts, *, tt=512):
    smem = pl.BlockSpec(memory_space=pltpu.SMEM)
    row_map = pl.BlockSpec((MAP_ROWS, MAP_LANES), lambda i: (0, 0))
    return pl.pallas_call(
        functools.partial(_finalize_kernel, tt=tt),
        grid=(N_TOK // tt,),
        in_specs=[smem, smem],
        out_specs=[row_map, row_map] + [smem] * 4,
        out_shape=[jax.ShapeDtypeStruct((MAP_ROWS, MAP_LANES), I32),
                   jax.ShapeDtypeStruct((MAP_ROWS, MAP_LANES), I32),
                   jax.ShapeDtypeStruct((N_BLOCKS,), I32),
                   jax.ShapeDtypeStruct((N_BLOCKS,), I32),
                   jax.ShapeDtypeStruct((N_BLOCKS,), I32),
                   jax.ShapeDtypeStruct((1,), I32)],
        scratch_shapes=[pltpu.VMEM((MAP_ROWS, MAP_LANES), I32), pltpu.SMEM((N_EXPERTS,), I32)],
        compiler_params=pltpu.CompilerParams(dimension_semantics=("arbitrary",)),
        name="finalize",
    )(pos_flat, counts)


BLOCK_SLAB_ROWS = ROW_BLOCK * SLAB_ROWS
BLOCKS_PER_STEP = 2
CAST_VREGS = 32


def _cast_weight(src_ref, dst_ref):
    rows, cols = src_ref.shape
    step = CAST_VREGS * 8 * 128 // cols

    def body(r, carry):
        r0 = pl.multiple_of(r * step, step)
        dst_ref[pl.ds(r0, step), :] = src_ref[pl.ds(r0, step), :].astype(BF16)
        return carry

    lax.fori_loop(0, rows // step, body, 0)


def _expert_kernel(blk_ref, first_ref, nxt_ref, nused_ref,
                   xs_ref, wg_hbm, wu_hbm, wd_hbm, tok_hbm, dst_hbm, y_hbm,
                   sg_ref, su_ref, sd_ref, bg_ref, bu_ref, bd_ref,
                   xg_ref, xb_ref, ys_ref, idx_ref, wsem, ysem, isem, *, layer):
    step = pl.program_id(0)
    nused = nused_ref[0]

    def weight_copies(ee):
        return (pltpu.make_async_copy(wg_hbm.at[layer, ee], sg_ref, wsem.at[0]),
                pltpu.make_async_copy(wu_hbm.at[layer, ee], su_ref, wsem.at[1]),
                pltpu.make_async_copy(wd_hbm.at[layer, ee], sd_ref, wsem.at[2]))

    def map_rows(blk):
        return pl.ds((blk + 1) * ROWS_PER_BLOCK, ROWS_PER_BLOCK)

    def idx_rows(j):
        return pl.ds(j * ROWS_PER_BLOCK, ROWS_PER_BLOCK)

    def idx_at(j, i):
        return idx_ref[j * ROWS_PER_BLOCK + i // MAP_LANES, i % MAP_LANES]

    def index_copies(b, slot):
        return (pltpu.make_async_copy(tok_hbm.at[map_rows(b + 1)], idx_ref.at[idx_rows(slot)],
                                      isem.at[slot]),
                pltpu.make_async_copy(dst_hbm.at[map_rows(b - 1)], idx_ref.at[idx_rows(2 + slot)],
                                      isem.at[2 + slot]))

    def wait_rows(slot):
        pltpu.make_async_copy(ys_ref.at[slot], y_hbm.at[pl.ds(0, BLOCK_SLAB_ROWS)],
                              ysem.at[slot]).wait()

    def gather_block(idx_row, slot):
        for i in range(ROW_BLOCK):
            xg_ref[pl.ds(i * SLAB_ROWS, SLAB_ROWS), :] = (
                xs_ref[pl.ds(pl.multiple_of(idx_at(idx_row, i), SLAB_ROWS), SLAB_ROWS), :])
        for c in range(SLAB_ROWS):
            w = xg_ref[pl.ds(c, ROW_BLOCK, stride=SLAB_ROWS), :]
            xb_ref[slot, :, _lo_col(c):_lo_col(c) + SLAB_LANES] = _unpack_lo(w).astype(BF16)
            xb_ref[slot, :, _hi_col(c):_hi_col(c) + SLAB_LANES] = _unpack_hi(w).astype(BF16)

    @pl.when(step == 0)
    def _():
        for cp in weight_copies(blk_ref[0]):
            cp.start()
        ys_ref[...] = jnp.zeros_like(ys_ref)
        zero_copies = [
            pltpu.make_async_copy(
                ys_ref.at[0],
                y_hbm.at[pl.ds((k * Y_ROWS + N_TOK + s * ROW_BLOCK) * SLAB_ROWS, BLOCK_SLAB_ROWS)],
                ysem.at[0])
            for k in range(TOP_K) for s in range(2)]
        for cp in zero_copies:
            cp.start()
        for cp in zero_copies:
            cp.wait()
        first_rows = pltpu.make_async_copy(tok_hbm.at[map_rows(0)], idx_ref.at[idx_rows(1)], isem.at[1])
        first_rows.start()
        first_rows.wait()
        gather_block(1, 0)
        for cp in index_copies(0, 0):
            cp.start()

    def one_block(b, cur):
        prev = 1 - cur

        @pl.when(b <= nused)
        def _():
            for cp in index_copies(b, cur):
                cp.wait()

            @pl.when(b < nused)
            def _():
                for cp in index_copies(b + 1, prev):
                    cp.start()

            @pl.when(first_ref[b] == 1)
            def _():
                for cp in weight_copies(blk_ref[b]):
                    cp.wait()
                _cast_weight(sg_ref, bg_ref)
                _cast_weight(su_ref, bu_ref)
                _cast_weight(sd_ref, bd_ref)

                @pl.when(nxt_ref[b] >= 0)
                def _():
                    for cp in weight_copies(nxt_ref[b]):
                        cp.start()

            @pl.when(b > 0)
            def _():
                wait_rows(cur)

            for i in range(ROW_BLOCK):
                pltpu.make_async_copy(
                    ys_ref.at[prev, pl.ds(i * SLAB_ROWS, SLAB_ROWS)],
                    y_hbm.at[pl.ds(pl.multiple_of(idx_at(2 + cur, i), SLAB_ROWS), SLAB_ROWS)],
                    ysem.at[prev]).start()

            x = xb_ref[cur]
            hg = jnp.dot(x, bg_ref[...], preferred_element_type=F32)
            hu = jnp.dot(x, bu_ref[...], preferred_element_type=F32)
            act = (_silu(hg) * hu).astype(BF16)
            pk = _pack_rows(jnp.dot(act, bd_ref[...], preferred_element_type=F32))
            for c in range(SLAB_ROWS):
                ys_ref[cur, pl.ds(c, ROW_BLOCK, stride=SLAB_ROWS), :] = pk[c]
            gather_block(cur, prev)

            @pl.when(b == nused)
            def _():
                wait_rows(prev)

    for j in range(BLOCKS_PER_STEP):
        one_block(step * BLOCKS_PER_STEP + j, j)


def _experts(blk, first, nxt, nused, tok_rows, dst_rows, slabs, w_gate, w_up, w_down, *, layer):
    grid_spec = pltpu.PrefetchScalarGridSpec(
        num_scalar_prefetch=4,
        grid=(N_BLOCKS // BLOCKS_PER_STEP,),
        in_specs=[pl.BlockSpec((N_TOK * SLAB_ROWS, SLAB_LANES), lambda b, *_: (0, 0),
                               pipeline_mode=pl.Buffered(1))]
                 + [pl.BlockSpec(memory_space=pl.ANY)] * 5,
        out_specs=pl.BlockSpec(memory_space=pl.ANY),
        scratch_shapes=[pltpu.VMEM((D_MODEL, EXPERT_DIM), F32),
                        pltpu.VMEM((D_MODEL, EXPERT_DIM), F32),
                        pltpu.VMEM((EXPERT_DIM, D_MODEL), F32),
                        pltpu.VMEM((D_MODEL, EXPERT_DIM), BF16),
                        pltpu.VMEM((D_MODEL, EXPERT_DIM), BF16),
                        pltpu.VMEM((EXPERT_DIM, D_MODEL), BF16),
                        pltpu.VMEM((BLOCK_SLAB_ROWS, SLAB_LANES), U32),
                        pltpu.VMEM((2, ROW_BLOCK, D_MODEL), BF16),
                        pltpu.VMEM((2, BLOCK_SLAB_ROWS, SLAB_LANES), U32),
                        pltpu.SMEM((4 * ROWS_PER_BLOCK, MAP_LANES), I32),
                        pltpu.SemaphoreType.DMA((3,)),
                        pltpu.SemaphoreType.DMA((2,)),
                        pltpu.SemaphoreType.DMA((4,))],
    )
    return pl.pallas_call(
        functools.partial(_expert_kernel, layer=layer),
        grid_spec=grid_spec,
        out_shape=jax.ShapeDtypeStruct((TOP_K * Y_ROWS * SLAB_ROWS, SLAB_LANES), U32),
        compiler_params=pltpu.CompilerParams(
            dimension_semantics=("arbitrary",), vmem_limit_bytes=VMEM_LIMIT),
        name="experts",
    )(blk, first, nxt, nused, slabs, w_gate, w_up, w_down, tok_rows, dst_rows)


def _combine_kernel(x_ref, y_ref, gate_ref, sg_ref, su_ref, sd_ref, g_ref, b_ref, o_ref, r_ref, *, tm):
    x = x_ref[...]
    xb = x.astype(BF16)
    hg = jnp.dot(xb, sg_ref[...], preferred_element_type=F32)
    hu = jnp.dot(xb, su_ref[...], preferred_element_type=F32)
    shared = jnp.dot((_silu(hg) * hu).astype(BF16), sd_ref[...], preferred_element_type=F32)
    gate = gate_ref[...]
    for c in range(SLAB_ROWS):
        lo = jnp.zeros((tm, SLAB_LANES), F32)
        hi = jnp.zeros((tm, SLAB_LANES), F32)
        for k in range(TOP_K):
            w = y_ref[k, pl.ds(c, tm, stride=SLAB_ROWS), :]
            gk = gate[:, k:k + 1]
            lo = lo + gk * _unpack_lo(w)
            hi = hi + gk * _unpack_hi(w)
        r_ref[:, _lo_col(c):_lo_col(c) + SLAB_LANES] = lo
        r_ref[:, _hi_col(c):_hi_col(c) + SLAB_LANES] = hi
    z = ALPHA * x + (r_ref[...] + shared)
    o_ref[...] = _layer_norm(z, g_ref[...], b_ref[...])


def _combine(x2d, y8, gate_tk, s_gate, s_up, s_down, ln_g, ln_b, *, tm=256):
    return pl.pallas_call(
        functools.partial(_combine_kernel, tm=tm),
        grid=(N_TOK // tm,),
        in_specs=[pl.BlockSpec((tm, D_MODEL), lambda i: (i, 0)),
                  pl.BlockSpec((TOP_K, tm * SLAB_ROWS, SLAB_LANES), lambda i: (0, i, 0)),
                  pl.BlockSpec((tm, TOP_K), lambda i: (i, 0)),
                  _resident((D_MODEL, EXPERT_DIM)), _resident((D_MODEL, EXPERT_DIM)),
                  _resident((EXPERT_DIM, D_MODEL)),
                  _resident((1, D_MODEL)), _resident((1, D_MODEL))],
        out_specs=pl.BlockSpec((tm, D_MODEL), lambda i: (i, 0)),
        out_shape=jax.ShapeDtypeStruct((N_TOK, D_MODEL), F32),
        scratch_shapes=[pltpu.VMEM((tm, D_MODEL), F32)],
        compiler_params=pltpu.CompilerParams(
            dimension_semantics=("arbitrary",), vmem_limit_bytes=VMEM_LIMIT),
        name="combine",
    )(x2d, y8, gate_tk, s_gate, s_up, s_down, ln_g, ln_b)


def _moe(x_f32, slabs, logits_t, layer, router_bias, exp_w_gate, exp_w_up, exp_w_down,
         s_gate, s_up, s_down, ln_g, ln_b):
    packed, gate8, counts = _route(logits_t, router_bias[layer].reshape(N_EXPERTS, 1))
    counts = counts[:, 0]
    tok_rows, dst_rows, blk, first, nxt, nused = _finalize(
        _positions(counts, packed).T.reshape(-1), counts)
    y8 = _experts(blk, first, nxt, nused, tok_rows, dst_rows, slabs, exp_w_gate, exp_w_up, exp_w_down,
                  layer=layer)
    y8 = y8.reshape(TOP_K, Y_ROWS * SLAB_ROWS, SLAB_LANES)
    return _combine(x_f32, y8, gate8.T, s_gate[layer].astype(BF16), s_up[layer].astype(BF16),
                    s_down[layer].astype(BF16), ln_g[layer].reshape(1, D_MODEL),
                    ln_b[layer].reshape(1, D_MODEL))


W1_COLS = KV_LORA_RANK + 2 * QK_ROPE_DIM + Q_LORA_RANK
Q_HEAD_COLS = QK_NOPE_DIM + 2 * QK_ROPE_DIM
LOG2_E = 1.4426950408889634
Q_SCALE = QK_DIM ** -0.5 * LOG2_E
FLASH_CHAIN = 256


def _rope_chunk(chunk, cs):
    u = chunk * cs
    return u + pltpu.roll(u, QK_ROPE_DIM, axis=1)


def _proj_kernel(x_ref, cs_ref, w1_ref, kvn_ref, qn_ref, wuk_ref, wuv_ref, wuq_ref,
                 q_ref, kn_ref, kr_ref, v_ref, *, tm):
    xb = x_ref[...].astype(BF16)
    h = jnp.dot(xb, w1_ref[...], preferred_element_type=F32)
    cs = cs_ref[...]
    c = _rms_norm(h[:, :KV_LORA_RANK], kvn_ref[...]).astype(BF16)
    kr = _rope_chunk(h[:, KV_LORA_RANK:KV_LORA_RANK + 2 * QK_ROPE_DIM], cs)
    lane = lax.broadcasted_iota(I32, (tm, 2 * QK_ROPE_DIM), 1)
    kr_ref[...] = jnp.where(lane < QK_ROPE_DIM, kr, 0.0).astype(BF16)
    cq = _rms_norm(h[:, KV_LORA_RANK + 2 * QK_ROPE_DIM:], qn_ref[...]).astype(BF16)
    kn = jnp.dot(c, wuk_ref[...], preferred_element_type=F32)
    vt = lax.dot_general(wuv_ref[...], c, NT_DIMS, preferred_element_type=F32)
    q = jnp.dot(cq, wuq_ref[...], preferred_element_type=F32)
    for hd in range(N_HEADS):
        kn_ref[hd] = kn[:, hd * QK_NOPE_DIM:(hd + 1) * QK_NOPE_DIM].astype(BF16)
        v_ref[hd] = vt[hd * V_DIM:(hd + 1) * V_DIM, :].astype(BF16)
        q0 = hd * Q_HEAD_COLS
        q_ref[hd, :, :QK_NOPE_DIM] = (q[:, q0:q0 + QK_NOPE_DIM] * Q_SCALE).astype(BF16)
        q_ref[hd, :, QK_NOPE_DIM:] = (
            _rope_chunk(q[:, q0 + QK_NOPE_DIM:q0 + Q_HEAD_COLS], cs) * Q_SCALE).astype(BF16)


def _proj(x2d, cs, w1, kv_norm, q_norm, w_uk, w_uv, w_uq, *, tm=256):
    nj = SEQ // tm
    head_spec = lambda d: pl.BlockSpec((None, N_HEADS, tm, d), lambda b, j: (b, 0, j, 0))
    return pl.pallas_call(
        functools.partial(_proj_kernel, tm=tm),
        grid=(BATCH, nj),
        in_specs=[pl.BlockSpec((tm, D_MODEL), lambda b, j: (b * nj + j, 0)),
                  pl.BlockSpec((tm, 2 * QK_ROPE_DIM), lambda b, j: (j, 0)),
                  _resident(w1.shape), _resident((1, KV_LORA_RANK)), _resident((1, Q_LORA_RANK)),
                  _resident(w_uk.shape), _resident(w_uv.shape), _resident(w_uq.shape)],
        out_specs=[head_spec(Q_HEAD_COLS), head_spec(QK_NOPE_DIM),
                   pl.BlockSpec((None, tm, 2 * QK_ROPE_DIM), lambda b, j: (b, j, 0)),
                   pl.BlockSpec((None, N_HEADS, V_DIM, tm), lambda b, j: (b, 0, 0, j))],
        out_shape=[jax.ShapeDtypeStruct((BATCH, N_HEADS, SEQ, Q_HEAD_COLS), BF16),
                   jax.ShapeDtypeStruct((BATCH, N_HEADS, SEQ, QK_NOPE_DIM), BF16),
                   jax.ShapeDtypeStruct((BATCH, SEQ, 2 * QK_ROPE_DIM), BF16),
                   jax.ShapeDtypeStruct((BATCH, N_HEADS, V_DIM, SEQ), BF16)],
        compiler_params=pltpu.CompilerParams(
            dimension_semantics=("arbitrary", "arbitrary"), vmem_limit_bytes=VMEM_LIMIT),
        name="proj",
    )(x2d, cs, w1, kv_norm, q_norm, w_uk, w_uv, w_uq)


def _flash_kernel(q_ref, kn_ref, kr_ref, vt_ref, o_ref, *, tq):
    qi = pl.program_id(2)
    n_chain = tq // FLASH_CHAIN
    qs = [q_ref[c * FLASH_CHAIN:(c + 1) * FLASH_CHAIN, :] for c in range(n_chain)]

    def scores(j):
        k0 = pl.multiple_of(j * tq, tq)
        k = jnp.concatenate([kn_ref[pl.ds(k0, tq), :], kr_ref[pl.ds(k0, tq), :]], axis=1)
        return tuple(lax.dot_general(k, qc, NT_DIMS, preferred_element_type=F32) for qc in qs)

    def consume(j, stats, ss, diagonal):
        vt = vt_ref[:, pl.ds(pl.multiple_of(j * tq, tq), tq)]
        out = []
        for c, ((m, l, acc), s) in enumerate(zip(stats, ss)):
            if diagonal:
                key = lax.broadcasted_iota(I32, (tq, FLASH_CHAIN), 0)
                qry = lax.broadcasted_iota(I32, (tq, FLASH_CHAIN), 1) + c * FLASH_CHAIN
                s = jnp.where(key <= qry, s, -jnp.inf)
            m_new = jnp.maximum(m, jnp.max(s, axis=0, keepdims=True))
            p = jnp.exp2(s - m_new)
            a = jnp.exp2(m - m_new)
            l = a * l + jnp.sum(p, axis=0, keepdims=True)
            acc = a * acc + jnp.dot(vt, p.astype(BF16), preferred_element_type=F32)
            out.append((m_new, l, acc))
        return tuple(out)

    def body(j, carry):
        stats, ss = carry
        nxt = scores(j + 1)
        return consume(j, stats, ss, False), nxt

    init = tuple((jnp.full((1, FLASH_CHAIN), -jnp.inf, F32), jnp.zeros((1, FLASH_CHAIN), F32),
                  jnp.zeros((V_DIM, FLASH_CHAIN), F32)) for _ in range(n_chain))
    stats, ss = lax.fori_loop(0, qi, body, (init, scores(0)))
    for c, (_, l, acc) in enumerate(consume(qi, stats, ss, True)):
        o_ref[c * FLASH_CHAIN:(c + 1) * FLASH_CHAIN, :] = (acc / l).T.astype(BF16)


def _flash(q, kn, kr, v, *, tq=512):
    return pl.pallas_call(
        functools.partial(_flash_kernel, tq=tq),
        grid=(BATCH, N_HEADS, SEQ // tq),
        in_specs=[pl.BlockSpec((None, None, tq, Q_HEAD_COLS), lambda b, h, i: (b, h, i, 0)),
                  pl.BlockSpec((None, None, SEQ, QK_NOPE_DIM), lambda b, h, i: (b, h, 0, 0)),
                  pl.BlockSpec((None, SEQ, 2 * QK_ROPE_DIM), lambda b, h, i: (b, 0, 0)),
                  pl.BlockSpec((None, None, V_DIM, SEQ), lambda b, h, i: (b, h, 0, 0))],
        out_specs=pl.BlockSpec((None, tq, V_DIM), lambda b, h, i: (b, i, h)),
        out_shape=jax.ShapeDtypeStruct((BATCH, SEQ, N_HEADS * V_DIM), BF16),
        compiler_params=pltpu.CompilerParams(
            dimension_semantics=("arbitrary", "arbitrary", "arbitrary"), vmem_limit_bytes=VMEM_LIMIT),
        name="flash",
    )(q, kn, kr, v)


def _attn_out_kernel(a_ref, x_ref, wo_ref, g_ref, b_ref, rw_ref, xo_ref, slab_ref, lg_ref, *, tm):
    ys = [jnp.dot(a_ref[s * SUB_ROWS:(s + 1) * SUB_ROWS, :], wo_ref[...], preferred_element_type=F32)
          for s in range(tm // SUB_ROWS)]
    _ln1_epilogue(ys, x_ref, g_ref, b_ref, rw_ref, xo_ref, slab_ref, lg_ref)


def _attn_out(attn2d, x2d, w_out, ln_g, ln_b, rw, *, tm=512):
    out_specs, out_shape = _ln1_out_specs(tm)
    return pl.pallas_call(
        functools.partial(_attn_out_kernel, tm=tm),
        grid=(N_TOK // tm,),
        in_specs=[pl.BlockSpec((tm, D_MODEL), lambda i: (i, 0)),
                  pl.BlockSpec((tm, D_MODEL), lambda i: (i, 0)),
                  _resident((D_MODEL, D_MODEL)),
                  _resident((1, D_MODEL)), _resident((1, D_MODEL)),
                  _resident((2, N_EXPERTS, D_MODEL))],
        out_specs=out_specs, out_shape=out_shape,
        compiler_params=pltpu.CompilerParams(
            dimension_semantics=("arbitrary",), vmem_limit_bytes=VMEM_LIMIT),
        name="attn_out",
    )(attn2d, x2d, w_out, ln_g, ln_b, rw)


def _rope_table():
    inv_freq = 1.0 / (ROPE_THETA ** (jnp.arange(0, QK_ROPE_DIM, 2, dtype=F32) / QK_ROPE_DIM))
    ang = jnp.arange(SEQ, dtype=F32)[:, None] * inv_freq[None, :]
    cos, sin = jnp.cos(ang), jnp.sin(ang)
    return jnp.concatenate([cos, cos, -sin, sin], axis=1)


def _swap_halves(w):
    half = QK_ROPE_DIM // 2
    return jnp.concatenate([w[..., half:], w[..., :half]], axis=-1)


def kernel(x, a_w_in, a_w_grp, a_scale, a_w_out, kv_w_down, kv_norm, kv_w_uk, kv_w_uv, b_w_dq, b_q_norm, b_w_uq, b_w_out, ln1_g, ln1_b, ln2_g, ln2_b, router_w, router_bias, exp_w_gate, exp_w_up, exp_w_down, sh_w_gate, sh_w_up, sh_w_down):
    x2d = x.reshape(N_TOK, D_MODEL)
    row = lambda v: v.reshape(1, -1)
    moe_args = (router_bias, exp_w_gate, exp_w_up, exp_w_down, sh_w_gate, sh_w_up, sh_w_down, ln2_g, ln2_b)

    pooled = _pool_in(x2d, a_w_in[0].astype(BF16))
    x1, slabs, logits_t = _mix_out(pooled, x2d, a_w_grp[0].astype(BF16), row(a_scale[0]),
                                   a_w_out[0].astype(BF16), row(ln1_g[0]), row(ln1_b[0]),
                                   _split_bf16(router_w[0].T))
    x2 = _moe(x1, slabs, logits_t, 0, *moe_args)

    kr_w = kv_w_down[:, KV_LORA_RANK:]
    w1 = jnp.concatenate([kv_w_down[:, :KV_LORA_RANK], kr_w, _swap_halves(kr_w), b_w_dq[0]],
                         axis=1).astype(BF16)
    uq = b_w_uq[0]
    uq_rope = uq[:, :, QK_NOPE_DIM:]
    w_uq = jnp.concatenate([uq[:, :, :QK_NOPE_DIM], uq_rope, _swap_halves(uq_rope)], axis=2)
    w_uq = w_uq.reshape(Q_LORA_RANK, N_HEADS * Q_HEAD_COLS).astype(BF16)
    q, kn, kr, v = _proj(x2, _rope_table(), w1, row(kv_norm), row(b_q_norm[0]),
                         kv_w_uk.reshape(KV_LORA_RANK, N_HEADS * QK_NOPE_DIM).astype(BF16),
                         kv_w_uv.reshape(KV_LORA_RANK, N_HEADS * V_DIM).T.astype(BF16), w_uq)
    attn = _flash(q, kn, kr, v).reshape(N_TOK, N_HEADS * V_DIM)
    x3, slabs, logits_t = _attn_out(attn, x2, b_w_out[0].astype(BF16), row(ln1_g[1]), row(ln1_b[1]),
                                    _split_bf16(router_w[1].T))
    x4 = _moe(x3, slabs, logits_t, 1, *moe_args)
    return x4.reshape(BATCH, SEQ, D_MODEL)
```

```python
import functools

import jax
import jax.numpy as jnp
from jax import lax
from jax.experimental import pallas as pl
from jax.experimental.pallas import tpu as pltpu

D_MODEL = 2048
BATCH = 2
SEQ = 4096
DEPTH = 2
N_TOK = BATCH * SEQ
ALPHA = (2.0 * DEPTH) ** 0.25
POOL_WINDOWS = (2, 4, 8, 16)
POOL_GROUP_DIM = D_MODEL // len(POOL_WINDOWS)
POOL_HALO = 16
N_HEADS = 16
QK_NOPE_DIM = 128
QK_ROPE_DIM = 64
QK_DIM = QK_NOPE_DIM + QK_ROPE_DIM
V_DIM = 128
Q_LORA_RANK = D_MODEL // 4
KV_LORA_RANK = D_MODEL // 4
ROPE_THETA = 10000.0
N_EXPERTS = 64
TOP_K = 8
N_EXPERT_GROUPS = 8
GROUP_SIZE = N_EXPERTS // N_EXPERT_GROUPS
TOPK_GROUPS = 4
EXPERT_DIM = D_MODEL // 4
ROUTED_SCALE = 2.5
ROW_BLOCK = 256
MAP_LANES = 128
MAP_SHIFT = MAP_LANES.bit_length() - 1
ROWS_PER_BLOCK = ROW_BLOCK // MAP_LANES
assert 1 << MAP_SHIFT == MAP_LANES and ROWS_PER_BLOCK * MAP_LANES == ROW_BLOCK
LN_EPS = 1e-5
RMS_EPS = 1e-6

N_ASSIGN = N_TOK * TOP_K
N_BLOCKS = N_ASSIGN // ROW_BLOCK + N_EXPERTS
MAP_ROWS = (N_BLOCKS + 4) * ROWS_PER_BLOCK
Y_ROWS = N_TOK + 2 * ROW_BLOCK
TOK_BITS = 14
assert N_TOK <= 1 << TOK_BITS and TOP_K * Y_ROWS < 1 << (31 - TOK_BITS)
SLAB_ROWS = 8
SLAB_LANES = 128
HALF_D = D_MODEL // 2
VMEM_LIMIT = 60 * 1024 * 1024

F32 = jnp.float32
BF16 = jnp.bfloat16
I32 = jnp.int32
U32 = jnp.uint32
NT_DIMS = (((1,), (1,)), ((), ()))


def _lo_col(c):
    return 2 * c * SLAB_LANES


def _hi_col(c):
    return (2 * c + 1) * SLAB_LANES


def _pack_rows(z):
    return [pltpu.pack_elementwise([z[:, _lo_col(c):_lo_col(c) + SLAB_LANES],
                                    z[:, _hi_col(c):_hi_col(c) + SLAB_LANES]], packed_dtype=BF16)
            for c in range(SLAB_ROWS)]


def _unpack_lo(w):
    return pltpu.unpack_elementwise(w, index=0, packed_dtype=BF16, unpacked_dtype=F32)


def _unpack_hi(w):
    return pltpu.unpack_elementwise(w, index=1, packed_dtype=BF16, unpacked_dtype=F32)


def _silu(x):
    return x * jax.nn.sigmoid(x)


def _layer_norm(z, g, b):
    mu = jnp.mean(z, axis=-1, keepdims=True)
    zc = z - mu
    var = jnp.mean(zc * zc, axis=-1, keepdims=True)
    return zc * lax.rsqrt(var + LN_EPS) * g + b


def _rms_norm(z, g):
    ms = jnp.mean(z * z, axis=-1, keepdims=True)
    return z * lax.rsqrt(ms + RMS_EPS) * g


def _resident(shape):
    nd = len(shape)
    return pl.BlockSpec(shape, lambda *_: (0,) * nd, pipeline_mode=pl.Buffered(1))


def _pool_in_kernel(x_ref, w_ref, o_ref, tail_ref, *, tm):
    j = pl.program_id(1)

    @pl.when(j == 0)
    def _():
        tail_ref[...] = jnp.zeros_like(tail_ref)

    h = jnp.dot(x_ref[...].astype(BF16), w_ref[...], preferred_element_type=F32)
    ext = jnp.concatenate([tail_ref[...], h], axis=0)
    tail_ref[...] = h[tm - POOL_HALO:, :]
    pos = j * tm + lax.broadcasted_iota(I32, (tm, 1), 0)
    for g, w in enumerate(POOL_WINDOWS):
        c0, c1 = g * POOL_GROUP_DIM, (g + 1) * POOL_GROUP_DIM
        s = ext[:, c0:c1]
        sh = 1
        while sh < w:
            s = s + pltpu.roll(s, sh, axis=0)
            sh *= 2
        inv = 1.0 / jnp.minimum(pos + 1, w).astype(F32)
        o_ref[:, c0:c1] = (s[POOL_HALO:, :] * inv - h[:, c0:c1]).astype(BF16)


def _pool_in(x2d, w_in, *, tm=512):
    nj = SEQ // tm
    return pl.pallas_call(
        functools.partial(_pool_in_kernel, tm=tm),
        grid=(BATCH, nj),
        in_specs=[pl.BlockSpec((tm, D_MODEL), lambda b, j: (b * nj + j, 0)),
                  _resident((D_MODEL, D_MODEL))],
        out_specs=pl.BlockSpec((tm, D_MODEL), lambda b, j: (b * nj + j, 0)),
        out_shape=jax.ShapeDtypeStruct((N_TOK, D_MODEL), BF16),
        scratch_shapes=[pltpu.VMEM((POOL_HALO, D_MODEL), F32)],
        compiler_params=pltpu.CompilerParams(
            dimension_semantics=("arbitrary", "arbitrary"), vmem_limit_bytes=VMEM_LIMIT),
        name="pool_in",
    )(x2d, w_in)


SUB_ROWS = 128


def _ln1_epilogue(ys, x_ref, g_ref, b_ref, rw_ref, xo_ref, slab_ref, lg_ref):
    nt = lambda a, b: lax.dot_general(a, b, NT_DIMS, preferred_element_type=F32)
    for s, y in enumerate(ys):
        r0 = s * SUB_ROWS
        xn = _layer_norm(ALPHA * x_ref[r0:r0 + SUB_ROWS, :] + y, g_ref[...], b_ref[...])
        xo_ref[r0:r0 + SUB_ROWS, :] = xn
        pk = _pack_rows(xn)
        for c in range(SLAB_ROWS):
            slab_ref[pl.ds(r0 * SLAB_ROWS + c, SUB_ROWS, stride=SLAB_ROWS), :] = pk[c]
        x_hi = xn.astype(BF16)
        x_lo = (xn - x_hi.astype(F32)).astype(BF16)
        lg_ref[:, r0:r0 + SUB_ROWS] = nt(rw_ref[0], x_hi) + (nt(rw_ref[0], x_lo) + nt(rw_ref[1], x_hi))


def _mix_out_kernel(p_ref, x_ref, wg_ref, sc_ref, wo_ref, g_ref, b_ref, rw_ref,
                    xo_ref, slab_ref, lg_ref, *, tm):
    ys = []
    for s in range(tm // SUB_ROWS):
        r0 = s * SUB_ROWS
        parts = []
        for g in range(len(POOL_WINDOWS)):
            c0, c1 = g * POOL_GROUP_DIM, (g + 1) * POOL_GROUP_DIM
            parts.append(jnp.dot(p_ref[r0:r0 + SUB_ROWS, c0:c1], wg_ref[g], preferred_element_type=F32))
        mixed = (jnp.concatenate(parts, axis=1) * sc_ref[...]).astype(BF16)
        ys.append(jnp.dot(mixed, wo_ref[...], preferred_element_type=F32))
    _ln1_epilogue(ys, x_ref, g_ref, b_ref, rw_ref, xo_ref, slab_ref, lg_ref)


def _ln1_out_specs(tm):
    return (
        [pl.BlockSpec((tm, D_MODEL), lambda i: (i, 0)),
         pl.BlockSpec((tm * SLAB_ROWS, SLAB_LANES), lambda i: (i, 0)),
         pl.BlockSpec((N_EXPERTS, tm), lambda i: (0, i))],
        [jax.ShapeDtypeStruct((N_TOK, D_MODEL), F32),
         jax.ShapeDtypeStruct((N_TOK * SLAB_ROWS, SLAB_LANES), U32),
         jax.ShapeDtypeStruct((N_EXPERTS, N_TOK), F32)],
    )


def _split_bf16(w):
    hi = w.astype(BF16)
    return jnp.stack([hi, (w - hi.astype(F32)).astype(BF16)])


def _mix_out(pooled, x2d, w_grp, scale, w_out, ln_g, ln_b, rw, *, tm=512):
    out_specs, out_shape = _ln1_out_specs(tm)
    return pl.pallas_call(
        functools.partial(_mix_out_kernel, tm=tm),
        grid=(N_TOK // tm,),
        in_specs=[pl.BlockSpec((tm, D_MODEL), lambda i: (i, 0)),
                  pl.BlockSpec((tm, D_MODEL), lambda i: (i, 0)),
                  _resident(w_grp.shape), _resident((1, D_MODEL)), _resident((D_MODEL, D_MODEL)),
                  _resident((1, D_MODEL)), _resident((1, D_MODEL)),
                  _resident((2, N_EXPERTS, D_MODEL))],
        out_specs=out_specs, out_shape=out_shape,
        compiler_params=pltpu.CompilerParams(
            dimension_semantics=("arbitrary",), vmem_limit_bytes=VMEM_LIMIT),
        name="mix_out",
    )(pooled, x2d, w_grp, scale, w_out, ln_g, ln_b, rw)


def _first_index(hit_src, best, iota, n):
    return jnp.min(jnp.where(hit_src == best, iota, n), axis=0, keepdims=True)


def _route_kernel(lg_ref, bias_ref, pk_ref, gate_ref, cnt_ref, carry_ref, *, tm):
    i = pl.program_id(0)

    @pl.when(i == 0)
    def _():
        carry_ref[...] = jnp.zeros_like(carry_ref)

    neg = -jnp.inf
    scores = jax.nn.sigmoid(lg_ref[...])
    choice = scores + bias_ref[...]
    iota_g = lax.broadcasted_iota(I32, (GROUP_SIZE, tm), 0)
    gscore = []
    for g in range(N_EXPERT_GROUPS):
        c = choice[g * GROUP_SIZE:(g + 1) * GROUP_SIZE, :]
        m1 = jnp.max(c, axis=0, keepdims=True)
        f1 = _first_index(c, m1, iota_g, GROUP_SIZE)
        m2 = jnp.max(jnp.where(iota_g == f1, neg, c), axis=0, keepdims=True)
        gscore.append(m1 + m2)
    gs = jnp.concatenate(gscore, axis=0)
    iota_ng = lax.broadcasted_iota(I32, (N_EXPERT_GROUPS, tm), 0)
    gsel = jnp.zeros((N_EXPERT_GROUPS, tm), jnp.bool_)
    for _ in range(TOPK_GROUPS):
        m = jnp.max(gs, axis=0, keepdims=True)
        hit = iota_ng == _first_index(gs, m, iota_ng, N_EXPERT_GROUPS)
        gsel = gsel | hit
        gs = jnp.where(hit, neg, gs)
    masked = jnp.concatenate(
        [jnp.where(gsel[g:g + 1, :], choice[g * GROUP_SIZE:(g + 1) * GROUP_SIZE, :], neg)
         for g in range(N_EXPERT_GROUPS)], axis=0)
    iota_e = lax.broadcasted_iota(I32, (N_EXPERTS, tm), 0)
    hits, eidx, gates = [], [], []
    sel = jnp.zeros((N_EXPERTS, tm), jnp.bool_)
    for _ in range(TOP_K):
        m = jnp.max(masked, axis=0, keepdims=True)
        f = _first_index(masked, m, iota_e, N_EXPERTS)
        hit = iota_e == f
        hits.append(hit)
        eidx.append(f)
        gates.append(jnp.sum(jnp.where(hit, scores, 0.0), axis=0, keepdims=True))
        masked = jnp.where(hit, neg, masked)
        sel = sel | hit
    gate = jnp.concatenate(gates, axis=0)
    gate_ref[...] = gate / jnp.sum(gate, axis=0, keepdims=True) * ROUTED_SCALE
    selb = jnp.where(sel, 1.0, 0.0).astype(BF16)
    before = (lax.broadcasted_iota(I32, (tm, tm), 0) < lax.broadcasted_iota(I32, (tm, tm), 1))
    rank = jnp.dot(selb, jnp.where(before, 1.0, 0.0).astype(BF16), preferred_element_type=F32)
    rank = (rank + carry_ref[...]).astype(I32)
    rk = [jnp.sum(jnp.where(h, rank, 0), axis=0, keepdims=True) for h in hits]
    pk_ref[...] = (jnp.concatenate(eidx, axis=0) << 16) | jnp.concatenate(rk, axis=0)
    carry_ref[...] = carry_ref[...] + jnp.sum(selb.astype(F32), axis=1, keepdims=True)
    cnt_ref[...] = jnp.broadcast_to(carry_ref[...], cnt_ref.shape).astype(I32)


def _route(logits_t, bias, *, tm=512):
    return pl.pallas_call(
        functools.partial(_route_kernel, tm=tm),
        grid=(N_TOK // tm,),
        in_specs=[pl.BlockSpec((N_EXPERTS, tm), lambda i: (0, i)),
                  _resident((N_EXPERTS, 1))],
        out_specs=[pl.BlockSpec((TOP_K, tm), lambda i: (0, i)),
                   pl.BlockSpec((TOP_K, tm), lambda i: (0, i)),
                   pl.BlockSpec((N_EXPERTS, SLAB_LANES), lambda i: (0, 0))],
        out_shape=[jax.ShapeDtypeStruct((TOP_K, N_TOK), I32),
                   jax.ShapeDtypeStruct((TOP_K, N_TOK), F32),
                   jax.ShapeDtypeStruct((N_EXPERTS, SLAB_LANES), I32)],
        scratch_shapes=[pltpu.VMEM((N_EXPERTS, 1), F32)],
        compiler_params=pltpu.CompilerParams(dimension_semantics=("arbitrary",)),
        name="route",
    )(logits_t, bias)


def _expert_offsets(cnt_ref, ps_ref):
    def offsets(e, acc):
        ps_ref[e] = acc
        return acc + ((cnt_ref[e] + (ROW_BLOCK - 1)) // ROW_BLOCK) * ROW_BLOCK

    return lax.fori_loop(0, N_EXPERTS, offsets, jnp.int32(ROW_BLOCK))


def _positions_kernel(cnt_ref, pk_ref, pos_ref, ps_ref):
    _expert_offsets(cnt_ref, ps_ref)
    pk = pk_ref[...]
    eidx = pk >> 16
    pos = pk & 0xFFFF
    for e in range(N_EXPERTS):
        pos = pos + jnp.where(eidx == e, ps_ref[e], 0)
    pos_ref[...] = pos


def _positions(counts, packed):
    return pl.pallas_call(
        _positions_kernel,
        in_specs=[pl.BlockSpec(memory_space=pltpu.SMEM), pl.BlockSpec(memory_space=pltpu.VMEM)],
        out_specs=pl.BlockSpec(memory_space=pltpu.VMEM),
        out_shape=jax.ShapeDtypeStruct((TOP_K, N_TOK), I32),
        scratch_shapes=[pltpu.SMEM((N_EXPERTS,), I32)],
        name="positions",
    )(counts, packed)


def _finalize_kernel(pos_ref, cnt_ref, tok_ref, dst_ref, blk_ref, first_ref, nxt_ref, nused_ref,
                     srt_ref, ps_ref, *, tt):
    i = pl.program_id(0)
    lane = lax.broadcasted_iota(I32, (1, MAP_LANES), 1)

    @pl.when(i == 0)
    def _():
        total = _expert_offsets(cnt_ref, ps_ref)
        nused_ref[0] = total // ROW_BLOCK - 1
        entry = (lax.broadcasted_iota(I32, srt_ref.shape, 0) * MAP_LANES
                 + lax.broadcasted_iota(I32, srt_ref.shape, 1))
        srt_ref[...] = (N_TOK + lax.rem(entry // ROW_BLOCK + 1, 2) * ROW_BLOCK
                        + lax.rem(entry, ROW_BLOCK)) << TOK_BITS

        def defaults(b, c):
            blk_ref[b] = N_EXPERTS - 1
            first_ref[b] = 0
            nxt_ref[b] = -1
            return c

        lax.fori_loop(0, N_BLOCKS, defaults, 0)

        def per_expert(j, nxt_e):
            e = N_EXPERTS - 1 - j
            c = cnt_ref[e]
            nb = (c + (ROW_BLOCK - 1)) // ROW_BLOCK
            r0 = ps_ref[e]
            b0 = r0 // ROW_BLOCK - 1

            def blocks(jb, carry):
                blk_ref[b0 + jb] = e
                first_ref[b0 + jb] = jnp.where(jb == 0, 1, 0)
                nxt_ref[b0 + jb] = nxt_e
                return carry

            lax.fori_loop(0, nb, blocks, 0)
            return jnp.where(c > 0, e, nxt_e)

        lax.fori_loop(0, N_EXPERTS, per_expert, jnp.int32(-1))

    def per_token(tl, carry):
        t = i * tt + tl
        entry = t * ((1 << TOK_BITS) + 1)
        for k in range(TOP_K):
            p = pos_ref[t * TOP_K + k]
            val = jnp.full((1, MAP_LANES), entry + ((k * Y_ROWS) << TOK_BITS), I32)
            pltpu.store(srt_ref.at[pl.ds(p >> MAP_SHIFT, 1), :], val, mask=lane == (p & (MAP_LANES - 1)))
        return carry

    lax.fori_loop(0, tt, per_token, 0)

    @pl.when(i == pl.num_programs(0) - 1)
    def _():
        srt = srt_ref[...]
        tok_ref[...] = (srt & ((1 << TOK_BITS) - 1)) * SLAB_ROWS
        dst_ref[...] = (srt >> TOK_BITS) * SLAB_ROWS


def _finalize(pos_flat, counts, *, tt=512):
    smem = pl.BlockSpec(memory_space=pltpu.SMEM)
    row_map = pl.BlockSpec((MAP_ROWS, MAP_LANES), lambda i: (0, 0))
    return pl.pallas_call(
        functools.partial(_finalize_kernel, tt=tt),
        grid=(N_TOK // tt,),
        in_specs=[smem, smem],
        out_specs=[row_map, row_map] + [smem] * 4,
        out_shape=[jax.ShapeDtypeStruct((MAP_ROWS, MAP_LANES), I32),
                   jax.ShapeDtypeStruct((MAP_ROWS, MAP_LANES), I32),
                   jax.ShapeDtypeStruct((N_BLOCKS,), I32),
                   jax.ShapeDtypeStruct((N_BLOCKS,), I32),
                   jax.ShapeDtypeStruct((N_BLOCKS,), I32),
                   jax.ShapeDtypeStruct((1,), I32)],
        scratch_shapes=[pltpu.VMEM((MAP_ROWS, MAP_LANES), I32), pltpu.SMEM((N_EXPERTS,), I32)],
        compiler_params=pltpu.CompilerParams(dimension_semantics=("arbitrary",)),
        name="finalize",
    )(pos_flat, counts)


BLOCK_SLAB_ROWS = ROW_BLOCK * SLAB_ROWS
BLOCKS_PER_STEP = 2
CAST_VREGS = 64


def _cast_weight(src_ref, dst_ref):
    rows, cols = src_ref.shape
    step = CAST_VREGS * 8 * 128 // cols

    def body(r, carry):
        r0 = pl.multiple_of(r * step, step)
        dst_ref[pl.ds(r0, step), :] = src_ref[pl.ds(r0, step), :].astype(BF16)
        return carry

    lax.fori_loop(0, rows // step, body, 0)


def _expert_kernel(blk_ref, first_ref, nxt_ref, nused_ref,
                   xs_ref, wg_hbm, wu_hbm, wd_hbm, tok_hbm, dst_hbm, y_hbm,
                   sg_ref, su_ref, sd_ref, bg_ref, bu_ref, bd_ref,
                   xg_ref, xb_ref, ys_ref, idx_ref, wsem, ysem, isem, *, layer):
    step = pl.program_id(0)
    nused = nused_ref[0]

    def weight_copies(ee):
        return (pltpu.make_async_copy(wg_hbm.at[layer, ee], sg_ref, wsem.at[0]),
                pltpu.make_async_copy(wu_hbm.at[layer, ee], su_ref, wsem.at[1]),
                pltpu.make_async_copy(wd_hbm.at[layer, ee], sd_ref, wsem.at[2]))

    def map_rows(blk):
        return pl.ds((blk + 1) * ROWS_PER_BLOCK, ROWS_PER_BLOCK)

    def idx_rows(j):
        return pl.ds(j * ROWS_PER_BLOCK, ROWS_PER_BLOCK)

    def idx_at(j, i):
        return idx_ref[j * ROWS_PER_BLOCK + i // MAP_LANES, i % MAP_LANES]

    def index_copies(b, slot):
        return (pltpu.make_async_copy(tok_hbm.at[map_rows(b + 1)], idx_ref.at[idx_rows(slot)],
                                      isem.at[slot]),
                pltpu.make_async_copy(dst_hbm.at[map_rows(b - 1)], idx_ref.at[idx_rows(2 + slot)],
                                      isem.at[2 + slot]))

    def wait_rows(slot):
        pltpu.make_async_copy(ys_ref.at[slot], y_hbm.at[pl.ds(0, BLOCK_SLAB_ROWS)],
                              ysem.at[slot]).wait()

    def gather_block(idx_row, slot):
        for i in range(ROW_BLOCK):
            xg_ref[pl.ds(i * SLAB_ROWS, SLAB_ROWS), :] = (
                xs_ref[pl.ds(pl.multiple_of(idx_at(idx_row, i), SLAB_ROWS), SLAB_ROWS), :])
        for c in range(SLAB_ROWS):
            w = xg_ref[pl.ds(c, ROW_BLOCK, stride=SLAB_ROWS), :]
            xb_ref[slot, :, _lo_col(c):_lo_col(c) + SLAB_LANES] = _unpack_lo(w).astype(BF16)
            xb_ref[slot, :, _hi_col(c):_hi_col(c) + SLAB_LANES] = _unpack_hi(w).astype(BF16)

    @pl.when(step == 0)
    def _():
        for cp in weight_copies(blk_ref[0]):
            cp.start()
        ys_ref[...] = jnp.zeros_like(ys_ref)
        zero_copies = [
            pltpu.make_async_copy(
                ys_ref.at[0],
                y_hbm.at[pl.ds((k * Y_ROWS + N_TOK + s * ROW_BLOCK) * SLAB_ROWS, BLOCK_SLAB_ROWS)],
                ysem.at[0])
            for k in range(TOP_K) for s in range(2)]
        for cp in zero_copies:
            cp.start()
        for cp in zero_copies:
            cp.wait()
        first_rows = pltpu.make_async_copy(tok_hbm.at[map_rows(0)], idx_ref.at[idx_rows(1)], isem.at[1])
        first_rows.start()
        first_rows.wait()
        gather_block(1, 0)
        for cp in index_copies(0, 0):
            cp.start()

    def one_block(b, cur):
        prev = 1 - cur

        @pl.when(b <= nused)
        def _():
            for cp in index_copies(b, cur):
                cp.wait()

            @pl.when(b < nused)
            def _():
                for cp in index_copies(b + 1, prev):
                    cp.start()

            @pl.when(first_ref[b] == 1)
            def _():
                for cp in weight_copies(blk_ref[b]):
                    cp.wait()
                _cast_weight(sg_ref, bg_ref)
                _cast_weight(su_ref, bu_ref)
                _cast_weight(sd_ref, bd_ref)

                @pl.when(nxt_ref[b] >= 0)
                def _():
                    for cp in weight_copies(nxt_ref[b]):
                        cp.start()

            @pl.when(b > 0)
            def _():
                wait_rows(cur)

            for i in range(ROW_BLOCK):
                pltpu.make_async_copy(
                    ys_ref.at[prev, pl.ds(i * SLAB_ROWS, SLAB_ROWS)],
                    y_hbm.at[pl.ds(pl.multiple_of(idx_at(2 + cur, i), SLAB_ROWS), SLAB_ROWS)],
                    ysem.at[prev]).start()

            x = xb_ref[cur]
            hg = jnp.dot(x, bg_ref[...], preferred_element_type=F32)
            hu = jnp.dot(x, bu_ref[...], preferred_element_type=F32)
            act = (_silu(hg) * hu).astype(BF16)
            pk = _pack_rows(jnp.dot(act, bd_ref[...], preferred_element_type=F32))
            for c in range(SLAB_ROWS):
                ys_ref[cur, pl.ds(c, ROW_BLOCK, stride=SLAB_ROWS), :] = pk[c]
            gather_block(cur, prev)

            @pl.when(b == nused)
            def _():
                wait_rows(prev)

    for j in range(BLOCKS_PER_STEP):
        one_block(step * BLOCKS_PER_STEP + j, j)


def _experts(blk, first, nxt, nused, tok_rows, dst_rows, slabs, w_gate, w_up, w_down, *, layer):
    grid_spec = pltpu.PrefetchScalarGridSpec(
        num_scalar_prefetch=4,
        grid=(N_BLOCKS // BLOCKS_PER_STEP,),
        in_specs=[pl.BlockSpec((N_TOK * SLAB_ROWS, SLAB_LANES), lambda b, *_: (0, 0),
                               pipeline_mode=pl.Buffered(1))]
                 + [pl.BlockSpec(memory_space=pl.ANY)] * 5,
        out_specs=pl.BlockSpec(memory_space=pl.ANY),
        scratch_shapes=[pltpu.VMEM((D_MODEL, EXPERT_DIM), F32),
                        pltpu.VMEM((D_MODEL, EXPERT_DIM), F32),
                        pltpu.VMEM((EXPERT_DIM, D_MODEL), F32),
                        pltpu.VMEM((D_MODEL, EXPERT_DIM), BF16),
                        pltpu.VMEM((D_MODEL, EXPERT_DIM), BF16),
                        pltpu.VMEM((EXPERT_DIM, D_MODEL), BF16),
                        pltpu.VMEM((BLOCK_SLAB_ROWS, SLAB_LANES), U32),
                        pltpu.VMEM((2, ROW_BLOCK, D_MODEL), BF16),
                        pltpu.VMEM((2, BLOCK_SLAB_ROWS, SLAB_LANES), U32),
                        pltpu.SMEM((4 * ROWS_PER_BLOCK, MAP_LANES), I32),
                        pltpu.SemaphoreType.DMA((3,)),
                        pltpu.SemaphoreType.DMA((2,)),
                        pltpu.SemaphoreType.DMA((4,))],
    )
    return pl.pallas_call(
        functools.partial(_expert_kernel, layer=layer),
        grid_spec=grid_spec,
        out_shape=jax.ShapeDtypeStruct((TOP_K * Y_ROWS * SLAB_ROWS, SLAB_LANES), U32),
        compiler_params=pltpu.CompilerParams(
            dimension_semantics=("arbitrary",), vmem_limit_bytes=VMEM_LIMIT),
        name="experts",
    )(blk, first, nxt, nused, slabs, w_gate, w_up, w_down, tok_rows, dst_rows)


def _combine_kernel(x_ref, y_ref, gate_ref, sg_ref, su_ref, sd_ref, g_ref, b_ref, o_ref, r_ref, *, tm):
    x = x_ref[...]
    xb = x.astype(BF16)
    hg = jnp.dot(xb, sg_ref[...], preferred_element_type=F32)
    hu = jnp.dot(xb, su_ref[...], preferred_element_type=F32)
    shared = jnp.dot((_silu(hg) * hu).astype(BF16), sd_ref[...], preferred_element_type=F32)
    gate = gate_ref[...]
    for c in range(SLAB_ROWS):
        lo = jnp.zeros((tm, SLAB_LANES), F32)
        hi = jnp.zeros((tm, SLAB_LANES), F32)
        for k in range(TOP_K):
            w = y_ref[k, pl.ds(c, tm, stride=SLAB_ROWS), :]
            gk = gate[:, k:k + 1]
            lo = lo + gk * _unpack_lo(w)
            hi = hi + gk * _unpack_hi(w)
        r_ref[:, _lo_col(c):_lo_col(c) + SLAB_LANES] = lo
        r_ref[:, _hi_col(c):_hi_col(c) + SLAB_LANES] = hi
    z = ALPHA * x + (r_ref[...] + shared)
    o_ref[...] = _layer_norm(z, g_ref[...], b_ref[...])


def _combine(x2d, y8, gate_tk, s_gate, s_up, s_down, ln_g, ln_b, *, tm=256):
    return pl.pallas_call(
        functools.partial(_combine_kernel, tm=tm),
        grid=(N_TOK // tm,),
        in_specs=[pl.BlockSpec((tm, D_MODEL), lambda i: (i, 0)),
                  pl.BlockSpec((TOP_K, tm * SLAB_ROWS, SLAB_LANES), lambda i: (0, i, 0)),
                  pl.BlockSpec((tm, TOP_K), lambda i: (i, 0)),
                  _resident((D_MODEL, EXPERT_DIM)), _resident((D_MODEL, EXPERT_DIM)),
                  _resident((EXPERT_DIM, D_MODEL)),
                  _resident((1, D_MODEL)), _resident((1, D_MODEL))],
        out_specs=pl.BlockSpec((tm, D_MODEL), lambda i: (i, 0)),
        out_shape=jax.ShapeDtypeStruct((N_TOK, D_MODEL), F32),
        scratch_shapes=[pltpu.VMEM((tm, D_MODEL), F32)],
        compiler_params=pltpu.CompilerParams(
            dimension_semantics=("arbitrary",), vmem_limit_bytes=VMEM_LIMIT),
        name="combine",
    )(x2d, y8, gate_tk, s_gate, s_up, s_down, ln_g, ln_b)


def _moe(x_f32, slabs, logits_t, layer, router_bias, exp_w_gate, exp_w_up, exp_w_down,
         s_gate, s_up, s_down, ln_g, ln_b):
    packed, gate8, counts = _route(logits_t, router_bias[layer].reshape(N_EXPERTS, 1))
    counts = counts[:, 0]
    tok_rows, dst_rows, blk, first, nxt, nused = _finalize(
        _positions(counts, packed).T.reshape(-1), counts)
    y8 = _experts(blk, first, nxt, nused, tok_rows, dst_rows, slabs, exp_w_gate, exp_w_up, exp_w_down,
                  layer=layer)
    y8 = y8.reshape(TOP_K, Y_ROWS * SLAB_ROWS, SLAB_LANES)
    return _combine(x_f32, y8, gate8.T, s_gate[layer].astype(BF16), s_up[layer].astype(BF16),
                    s_down[layer].astype(BF16), ln_g[layer].reshape(1, D_MODEL),
                    ln_b[layer].reshape(1, D_MODEL))


W1_COLS = KV_LORA_RANK + 2 * QK_ROPE_DIM + Q_LORA_RANK
Q_HEAD_COLS = QK_NOPE_DIM + 2 * QK_ROPE_DIM
LOG2_E = 1.4426950408889634
Q_SCALE = QK_DIM ** -0.5 * LOG2_E
FLASH_CHAIN = 256


def _rope_chunk(chunk, cs):
    u = chunk * cs
    return u + pltpu.roll(u, QK_ROPE_DIM, axis=1)


def _proj_kernel(x_ref, cs_ref, w1_ref, kvn_ref, qn_ref, wuk_ref, wuv_ref, wuq_ref,
                 q_ref, kn_ref, kr_ref, v_ref, *, tm):
    xb = x_ref[...].astype(BF16)
    h = jnp.dot(xb, w1_ref[...], preferred_element_type=F32)
    cs = cs_ref[...]
    c = _rms_norm(h[:, :KV_LORA_RANK], kvn_ref[...]).astype(BF16)
    kr = _rope_chunk(h[:, KV_LORA_RANK:KV_LORA_RANK + 2 * QK_ROPE_DIM], cs)
    lane = lax.broadcasted_iota(I32, (tm, 2 * QK_ROPE_DIM), 1)
    kr_ref[...] = jnp.where(lane < QK_ROPE_DIM, kr, 0.0).astype(BF16)
    cq = _rms_norm(h[:, KV_LORA_RANK + 2 * QK_ROPE_DIM:], qn_ref[...]).astype(BF16)
    kn = jnp.dot(c, wuk_ref[...], preferred_element_type=F32)
    vt = lax.dot_general(wuv_ref[...], c, NT_DIMS, preferred_element_type=F32)
    q = jnp.dot(cq, wuq_ref[...], preferred_element_type=F32)
    for hd in range(N_HEADS):
        kn_ref[hd] = kn[:, hd * QK_NOPE_DIM:(hd + 1) * QK_NOPE_DIM].astype(BF16)
        v_ref[hd] = vt[hd * V_DIM:(hd + 1) * V_DIM, :].astype(BF16)
        q0 = hd * Q_HEAD_COLS
        q_ref[hd, :, :QK_NOPE_DIM] = (q[:, q0:q0 + QK_NOPE_DIM] * Q_SCALE).astype(BF16)
        q_ref[hd, :, QK_NOPE_DIM:] = (
            _rope_chunk(q[:, q0 + QK_NOPE_DIM:q0 + Q_HEAD_COLS], cs) * Q_SCALE).astype(BF16)


def _proj(x2d, cs, w1, kv_norm, q_norm, w_uk, w_uv, w_uq, *, tm=256):
    nj = SEQ // tm
    head_spec = lambda d: pl.BlockSpec((None, N_HEADS, tm, d), lambda b, j: (b, 0, j, 0))
    return pl.pallas_call(
        functools.partial(_proj_kernel, tm=tm),
        grid=(BATCH, nj),
        in_specs=[pl.BlockSpec((tm, D_MODEL), lambda b, j: (b * nj + j, 0)),
                  pl.BlockSpec((tm, 2 * QK_ROPE_DIM), lambda b, j: (j, 0)),
                  _resident(w1.shape), _resident((1, KV_LORA_RANK)), _resident((1, Q_LORA_RANK)),
                  _resident(w_uk.shape), _resident(w_uv.shape), _resident(w_uq.shape)],
        out_specs=[head_spec(Q_HEAD_COLS), head_spec(QK_NOPE_DIM),
                   pl.BlockSpec((None, tm, 2 * QK_ROPE_DIM), lambda b, j: (b, j, 0)),
                   pl.BlockSpec((None, N_HEADS, V_DIM, tm), lambda b, j: (b, 0, 0, j))],
        out_shape=[jax.ShapeDtypeStruct((BATCH, N_HEADS, SEQ, Q_HEAD_COLS), BF16),
                   jax.ShapeDtypeStruct((BATCH, N_HEADS, SEQ, QK_NOPE_DIM), BF16),
                   jax.ShapeDtypeStruct((BATCH, SEQ, 2 * QK_ROPE_DIM), BF16),
                   jax.ShapeDtypeStruct((BATCH, N_HEADS, V_DIM, SEQ), BF16)],
        compiler_params=pltpu.CompilerParams(
            dimension_semantics=("arbitrary", "arbitrary"), vmem_limit_bytes=VMEM_LIMIT),
        name="proj",
    )(x2d, cs, w1, kv_norm, q_norm, w_uk, w_uv, w_uq)


def _flash_kernel(q_ref, kn_ref, kr_ref, vt_ref, o_ref, sa_ref, sb_ref, *, tq):
    qi = pl.program_id(2)
    n_chain = tq // FLASH_CHAIN
    qs = [q_ref[c * FLASH_CHAIN:(c + 1) * FLASH_CHAIN, :] for c in range(n_chain)]

    def scores_into(j, s_ref):
        k0 = pl.multiple_of(j * tq, tq)
        k = jnp.concatenate([kn_ref[pl.ds(k0, tq), :], kr_ref[pl.ds(k0, tq), :]], axis=1)
        for c, qc in enumerate(qs):
            s_ref[:, c * FLASH_CHAIN:(c + 1) * FLASH_CHAIN] = (
                lax.dot_general(k, qc, NT_DIMS, preferred_element_type=F32))

    def consume(j, stats, s_ref, diagonal):
        vt = vt_ref[:, pl.ds(pl.multiple_of(j * tq, tq), tq)]
        out = []
        for c, (m, l, acc) in enumerate(stats):
            s = s_ref[:, c * FLASH_CHAIN:(c + 1) * FLASH_CHAIN]
            if diagonal:
                key = lax.broadcasted_iota(I32, (tq, FLASH_CHAIN), 0)
                qry = lax.broadcasted_iota(I32, (tq, FLASH_CHAIN), 1) + c * FLASH_CHAIN
                s = jnp.where(key <= qry, s, -jnp.inf)
            m_new = jnp.maximum(m, jnp.max(s, axis=0, keepdims=True))
            p = jnp.exp2(s - m_new)
            a = jnp.exp2(m - m_new)
            l = a * l + jnp.sum(p, axis=0, keepdims=True)
            acc = a * acc + jnp.dot(vt, p.astype(BF16), preferred_element_type=F32)
            out.append((m_new, l, acc))
        return tuple(out)

    def pair(i2, stats):
        j = 2 * i2
        scores_into(j + 1, sb_ref)
        stats = consume(j, stats, sa_ref, False)
        scores_into(j + 2, sa_ref)
        return consume(j + 1, stats, sb_ref, False)

    def finish(stats):
        for c, (_, l, acc) in enumerate(stats):
            o_ref[c * FLASH_CHAIN:(c + 1) * FLASH_CHAIN, :] = (acc / l).T.astype(BF16)

    init = tuple((jnp.full((1, FLASH_CHAIN), -jnp.inf, F32), jnp.zeros((1, FLASH_CHAIN), F32),
                  jnp.zeros((V_DIM, FLASH_CHAIN), F32)) for _ in range(n_chain))
    scores_into(0, sa_ref)
    stats = lax.fori_loop(0, qi // 2, pair, init)

    @pl.when(qi % 2 == 1)
    def _():
        scores_into(qi, sb_ref)
        finish(consume(qi, consume(qi - 1, stats, sa_ref, False), sb_ref, True))

    @pl.when(qi % 2 == 0)
    def _():
        finish(consume(qi, stats, sa_ref, True))


def _flash(q, kn, kr, v, *, tq=512):
    return pl.pallas_call(
        functools.partial(_flash_kernel, tq=tq),
        grid=(BATCH, N_HEADS, SEQ // tq),
        in_specs=[pl.BlockSpec((None, None, tq, Q_HEAD_COLS), lambda b, h, i: (b, h, i, 0)),
                  pl.BlockSpec((None, None, SEQ, QK_NOPE_DIM), lambda b, h, i: (b, h, 0, 0)),
                  pl.BlockSpec((None, SEQ, 2 * QK_ROPE_DIM), lambda b, h, i: (b, 0, 0)),
                  pl.BlockSpec((None, None, V_DIM, SEQ), lambda b, h, i: (b, h, 0, 0))],
        out_specs=pl.BlockSpec((None, tq, V_DIM), lambda b, h, i: (b, i, h)),
        out_shape=jax.ShapeDtypeStruct((BATCH, SEQ, N_HEADS * V_DIM), BF16),
        scratch_shapes=[pltpu.VMEM((tq, tq), F32), pltpu.VMEM((tq, tq), F32)],
        compiler_params=pltpu.CompilerParams(
            dimension_semantics=("arbitrary", "arbitrary", "arbitrary"), vmem_limit_bytes=VMEM_LIMIT),
        name="flash",
    )(q, kn, kr, v)


def _attn_out_kernel(a_ref, x_ref, wo_ref, g_ref, b_ref, rw_ref, xo_ref, slab_ref, lg_ref, *, tm):
    ys = [jnp.dot(a_ref[s * SUB_ROWS:(s + 1) * SUB_ROWS, :], wo_ref[...], preferred_element_type=F32)
          for s in range(tm // SUB_ROWS)]
    _ln1_epilogue(ys, x_ref, g_ref, b_ref, rw_ref, xo_ref, slab_ref, lg_ref)


def _attn_out(attn2d, x2d, w_out, ln_g, ln_b, rw, *, tm=512):
    out_specs, out_shape = _ln1_out_specs(tm)
    return pl.pallas_call(
        functools.partial(_attn_out_kernel, tm=tm),
        grid=(N_TOK // tm,),
        in_specs=[pl.BlockSpec((tm, D_MODEL), lambda i: (i, 0)),
                  pl.BlockSpec((tm, D_MODEL), lambda i: (i, 0)),
                  _resident((D_MODEL, D_MODEL)),
                  _resident((1, D_MODEL)), _resident((1, D_MODEL)),
                  _resident((2, N_EXPERTS, D_MODEL))],
        out_specs=out_specs, out_shape=out_shape,
        compiler_params=pltpu.CompilerParams(
            dimension_semantics=("arbitrary",), vmem_limit_bytes=VMEM_LIMIT),
        name="attn_out",
    )(attn2d, x2d, w_out, ln_g, ln_b, rw)


def _rope_table():
    inv_freq = 1.0 / (ROPE_THETA ** (jnp.arange(0, QK_ROPE_DIM, 2, dtype=F32) / QK_ROPE_DIM))
    ang = jnp.arange(SEQ, dtype=F32)[:, None] * inv_freq[None, :]
    cos, sin = jnp.cos(ang), jnp.sin(ang)
    return jnp.concatenate([cos, cos, -sin, sin], axis=1)


def _swap_halves(w):
    half = QK_ROPE_DIM // 2
    return jnp.concatenate([w[..., half:], w[..., :half]], axis=-1)


def kernel(x, a_w_in, a_w_grp, a_scale, a_w_out, kv_w_down, kv_norm, kv_w_uk, kv_w_uv, b_w_dq, b_q_norm, b_w_uq, b_w_out, ln1_g, ln1_b, ln2_g, ln2_b, router_w, router_bias, exp_w_gate, exp_w_up, exp_w_down, sh_w_gate, sh_w_up, sh_w_down):
    x2d = x.reshape(N_TOK, D_MODEL)
    row = lambda v: v.reshape(1, -1)
    moe_args = (router_bias, exp_w_gate, exp_w_up, exp_w_down, sh_w_gate, sh_w_up, sh_w_down, ln2_g, ln2_b)

    pooled = _pool_in(x2d, a_w_in[0].astype(BF16))
    x1, slabs, logits_t = _mix_out(pooled, x2d, a_w_grp[0].astype(BF16), row(a_scale[0]),
                                   a_w_out[0].astype(BF16), row(ln1_g[0]), row(ln1_b[0]),
                                   _split_bf16(router_w[0].T))
    x2 = _moe(x1, slabs, logits_t, 0, *moe_args)

    kr_w = kv_w_down[:, KV_LORA_RANK:]
    w1 = jnp.concatenate([kv_w_down[:, :KV_LORA_RANK], kr_w, _swap_halves(kr_w), b_w_dq[0]],
                         axis=1).astype(BF16)
    uq = b_w_uq[0]
    uq_rope = uq[:, :, QK_NOPE_DIM:]
    w_uq = jnp.concatenate([uq[:, :, :QK_NOPE_DIM], uq_rope, _swap_halves(uq_rope)], axis=2)
    w_uq = w_uq.reshape(Q_LORA_RANK, N_HEADS * Q_HEAD_COLS).astype(BF16)
    q, kn, kr, v = _proj(x2, _rope_table(), w1, row(kv_norm), row(b_q_norm[0]),
                         kv_w_uk.reshape(KV_LORA_RANK, N_HEADS * QK_NOPE_DIM).astype(BF16),
                         kv_w_uv.reshape(KV_LORA_RANK, N_HEADS * V_DIM).T.astype(BF16), w_uq)
    attn = _flash(q, kn, kr, v).reshape(N_TOK, N_HEADS * V_DIM)
    x3, slabs, logits_t = _attn_out(attn, x2, b_w_out[0].astype(BF16), row(ln1_g[1]), row(ln1_b[1]),
                                    _split_bf16(router_w[1].T))
    x4 = _moe(x3, slabs, logits_t, 1, *moe_args)
    return x4.reshape(BATCH, SEQ, D_MODEL)
```

```python
import functools

import jax
import jax.numpy as jnp
from jax import lax
from jax.experimental import pallas as pl
from jax.experimental.pallas import tpu as pltpu

D_MODEL = 2048
BATCH = 2
SEQ = 4096
DEPTH = 2
N_TOK = BATCH * SEQ
ALPHA = (2.0 * DEPTH) ** 0.25
POOL_WINDOWS = (2, 4, 8, 16)
POOL_GROUP_DIM = D_MODEL // len(POOL_WINDOWS)
POOL_HALO = 16
N_HEADS = 16
QK_NOPE_DIM = 128
QK_ROPE_DIM = 64
QK_DIM = QK_NOPE_DIM + QK_ROPE_DIM
V_DIM = 128
Q_LORA_RANK = D_MODEL // 4
KV_LORA_RANK = D_MODEL // 4
ROPE_THETA = 10000.0
N_EXPERTS = 64
TOP_K = 8
N_EXPERT_GROUPS = 8
GROUP_SIZE = N_EXPERTS // N_EXPERT_GROUPS
TOPK_GROUPS = 4
EXPERT_DIM = D_MODEL // 4
ROUTED_SCALE = 2.5
ROW_BLOCK = 256
MAP_LANES = 128
MAP_SHIFT = MAP_LANES.bit_length() - 1
ROWS_PER_BLOCK = ROW_BLOCK // MAP_LANES
assert 1 << MAP_SHIFT == MAP_LANES and ROWS_PER_BLOCK * MAP_LANES == ROW_BLOCK
LN_EPS = 1e-5
RMS_EPS = 1e-6

N_ASSIGN = N_TOK * TOP_K
N_BLOCKS = N_ASSIGN // ROW_BLOCK + N_EXPERTS
MAP_ROWS = (N_BLOCKS + 4) * ROWS_PER_BLOCK
Y_ROWS = N_TOK + 2 * ROW_BLOCK
TOK_BITS = 14
assert N_TOK <= 1 << TOK_BITS and TOP_K * Y_ROWS < 1 << (31 - TOK_BITS)
SLAB_ROWS = 8
SLAB_LANES = 128
SUB_ROWS = 128
V7X_VMEM_BYTES = 64 * 1024 * 1024
VMEM_LIMIT = V7X_VMEM_BYTES - 4 * 1024 * 1024

F32 = jnp.float32
BF16 = jnp.bfloat16
I32 = jnp.int32
U32 = jnp.uint32
NT_DIMS = (((1,), (1,)), ((), ()))


def _lo_col(c):
    return 2 * c * SLAB_LANES


def _hi_col(c):
    return (2 * c + 1) * SLAB_LANES


def _pack_rows(z):
    return [pltpu.pack_elementwise([z[:, _lo_col(c):_lo_col(c) + SLAB_LANES],
                                    z[:, _hi_col(c):_hi_col(c) + SLAB_LANES]], packed_dtype=BF16)
            for c in range(SLAB_ROWS)]


def _unpack_lo(w):
    return pltpu.unpack_elementwise(w, index=0, packed_dtype=BF16, unpacked_dtype=F32)


def _unpack_hi(w):
    return pltpu.unpack_elementwise(w, index=1, packed_dtype=BF16, unpacked_dtype=F32)


def _silu(x):
    return x * jax.nn.sigmoid(x)


def _layer_norm(z, g, b):
    mu = jnp.mean(z, axis=-1, keepdims=True)
    zc = z - mu
    var = jnp.mean(zc * zc, axis=-1, keepdims=True)
    return zc * lax.rsqrt(var + LN_EPS) * g + b


def _rms_norm(z, g):
    ms = jnp.mean(z * z, axis=-1, keepdims=True)
    return z * lax.rsqrt(ms + RMS_EPS) * g


def _resident(shape):
    nd = len(shape)
    return pl.BlockSpec(shape, lambda *_: (0,) * nd, pipeline_mode=pl.Buffered(1))


def _pool_in_kernel(x_ref, w_ref, o_ref, tail_ref, *, tm):
    j = pl.program_id(1)

    @pl.when(j == 0)
    def _():
        tail_ref[...] = jnp.zeros_like(tail_ref)

    hs = [jnp.dot(x_ref[r0:r0 + SUB_ROWS, :].astype(BF16), w_ref[...], preferred_element_type=F32)
          for r0 in range(0, tm, SUB_ROWS)]
    tail = tail_ref[...]
    for n, h in enumerate(hs):
        r0 = n * SUB_ROWS
        ext = jnp.concatenate([tail, h], axis=0)
        tail = h[SUB_ROWS - POOL_HALO:, :]
        pos = j * tm + r0 + lax.broadcasted_iota(I32, (SUB_ROWS, 1), 0)
        for g, w in enumerate(POOL_WINDOWS):
            c0, c1 = g * POOL_GROUP_DIM, (g + 1) * POOL_GROUP_DIM
            s = ext[:, c0:c1]
            sh = 1
            while sh < w:
                s = s + pltpu.roll(s, sh, axis=0)
                sh *= 2
            inv = 1.0 / jnp.minimum(pos + 1, w).astype(F32)
            o_ref[r0:r0 + SUB_ROWS, c0:c1] = (s[POOL_HALO:, :] * inv - h[:, c0:c1]).astype(BF16)
    tail_ref[...] = tail


def _pool_in(x2d, w_in, *, tm=512):
    nj = SEQ // tm
    return pl.pallas_call(
        functools.partial(_pool_in_kernel, tm=tm),
        grid=(BATCH, nj),
        in_specs=[pl.BlockSpec((tm, D_MODEL), lambda b, j: (b * nj + j, 0)),
                  _resident((D_MODEL, D_MODEL))],
        out_specs=pl.BlockSpec((tm, D_MODEL), lambda b, j: (b * nj + j, 0)),
        out_shape=jax.ShapeDtypeStruct((N_TOK, D_MODEL), BF16),
        scratch_shapes=[pltpu.VMEM((POOL_HALO, D_MODEL), F32)],
        compiler_params=pltpu.CompilerParams(
            dimension_semantics=("arbitrary", "arbitrary"), vmem_limit_bytes=VMEM_LIMIT),
        name="pool_in",
    )(x2d, w_in)


def _ln1_epilogue(ys, x_ref, g_ref, b_ref, rw_ref, xo_ref, slab_ref, lg_ref):
    nt = lambda a, b: lax.dot_general(a, b, NT_DIMS, preferred_element_type=F32)
    for s, y in enumerate(ys):
        r0 = s * SUB_ROWS
        xn = _layer_norm(ALPHA * x_ref[r0:r0 + SUB_ROWS, :] + y, g_ref[...], b_ref[...])
        xo_ref[r0:r0 + SUB_ROWS, :] = xn
        pk = _pack_rows(xn)
        for c in range(SLAB_ROWS):
            slab_ref[pl.ds(r0 * SLAB_ROWS + c, SUB_ROWS, stride=SLAB_ROWS), :] = pk[c]
        x_hi = xn.astype(BF16)
        x_lo = (xn - x_hi.astype(F32)).astype(BF16)
        lg_ref[:, r0:r0 + SUB_ROWS] = nt(rw_ref[0], x_hi) + (nt(rw_ref[0], x_lo) + nt(rw_ref[1], x_hi))


def _mix_out_kernel(p_ref, x_ref, wg_ref, sc_ref, wo_ref, g_ref, b_ref, rw_ref,
                    xo_ref, slab_ref, lg_ref, *, tm):
    ys = []
    for s in range(tm // SUB_ROWS):
        r0 = s * SUB_ROWS
        parts = []
        for g in range(len(POOL_WINDOWS)):
            c0, c1 = g * POOL_GROUP_DIM, (g + 1) * POOL_GROUP_DIM
            parts.append(jnp.dot(p_ref[r0:r0 + SUB_ROWS, c0:c1], wg_ref[g], preferred_element_type=F32))
        mixed = (jnp.concatenate(parts, axis=1) * sc_ref[...]).astype(BF16)
        ys.append(jnp.dot(mixed, wo_ref[...], preferred_element_type=F32))
    _ln1_epilogue(ys, x_ref, g_ref, b_ref, rw_ref, xo_ref, slab_ref, lg_ref)


def _ln1_out_specs(tm):
    return (
        [pl.BlockSpec((tm, D_MODEL), lambda i: (i, 0)),
         pl.BlockSpec((tm * SLAB_ROWS, SLAB_LANES), lambda i: (i, 0)),
         pl.BlockSpec((N_EXPERTS, tm), lambda i: (0, i))],
        [jax.ShapeDtypeStruct((N_TOK, D_MODEL), F32),
         jax.ShapeDtypeStruct((N_TOK * SLAB_ROWS, SLAB_LANES), U32),
         jax.ShapeDtypeStruct((N_EXPERTS, N_TOK), F32)],
    )


def _split_bf16(w):
    hi = w.astype(BF16)
    return jnp.stack([hi, (w - hi.astype(F32)).astype(BF16)])


def _mix_out(pooled, x2d, w_grp, scale, w_out, ln_g, ln_b, rw, *, tm=512):
    out_specs, out_shape = _ln1_out_specs(tm)
    return pl.pallas_call(
        functools.partial(_mix_out_kernel, tm=tm),
        grid=(N_TOK // tm,),
        in_specs=[pl.BlockSpec((tm, D_MODEL), lambda i: (i, 0)),
                  pl.BlockSpec((tm, D_MODEL), lambda i: (i, 0)),
                  _resident(w_grp.shape), _resident((1, D_MODEL)), _resident((D_MODEL, D_MODEL)),
                  _resident((1, D_MODEL)), _resident((1, D_MODEL)),
                  _resident((2, N_EXPERTS, D_MODEL))],
        out_specs=out_specs, out_shape=out_shape,
        compiler_params=pltpu.CompilerParams(
            dimension_semantics=("arbitrary",), vmem_limit_bytes=VMEM_LIMIT),
        name="mix_out",
    )(pooled, x2d, w_grp, scale, w_out, ln_g, ln_b, rw)


def _first_index(hit_src, best, iota, n):
    return jnp.min(jnp.where(hit_src == best, iota, n), axis=0, keepdims=True)


def _route_kernel(lg_ref, bias_ref, pk_ref, gate_ref, cnt_ref, carry_ref, *, tm):
    i = pl.program_id(0)

    @pl.when(i == 0)
    def _():
        carry_ref[...] = jnp.zeros_like(carry_ref)

    neg = -jnp.inf
    scores = jax.nn.sigmoid(lg_ref[...])
    choice = scores + bias_ref[...]
    iota_g = lax.broadcasted_iota(I32, (GROUP_SIZE, tm), 0)
    gscore = []
    for g in range(N_EXPERT_GROUPS):
        c = choice[g * GROUP_SIZE:(g + 1) * GROUP_SIZE, :]
        m1 = jnp.max(c, axis=0, keepdims=True)
        f1 = _first_index(c, m1, iota_g, GROUP_SIZE)
        m2 = jnp.max(jnp.where(iota_g == f1, neg, c), axis=0, keepdims=True)
        gscore.append(m1 + m2)
    gs = jnp.concatenate(gscore, axis=0)
    iota_ng = lax.broadcasted_iota(I32, (N_EXPERT_GROUPS, tm), 0)
    gsel = jnp.zeros((N_EXPERT_GROUPS, tm), jnp.bool_)
    for _ in range(TOPK_GROUPS):
        m = jnp.max(gs, axis=0, keepdims=True)
        hit = iota_ng == _first_index(gs, m, iota_ng, N_EXPERT_GROUPS)
        gsel = gsel | hit
        gs = jnp.where(hit, neg, gs)
    masked = jnp.concatenate(
        [jnp.where(gsel[g:g + 1, :], choice[g * GROUP_SIZE:(g + 1) * GROUP_SIZE, :], neg)
         for g in range(N_EXPERT_GROUPS)], axis=0)
    iota_e = lax.broadcasted_iota(I32, (N_EXPERTS, tm), 0)
    hits, eidx, gates = [], [], []
    sel = jnp.zeros((N_EXPERTS, tm), jnp.bool_)
    for _ in range(TOP_K):
        m = jnp.max(masked, axis=0, keepdims=True)
        f = _first_index(masked, m, iota_e, N_EXPERTS)
        hit = iota_e == f
        hits.append(hit)
        eidx.append(f)
        gates.append(jnp.sum(jnp.where(hit, scores, 0.0), axis=0, keepdims=True))
        masked = jnp.where(hit, neg, masked)
        sel = sel | hit
    gate = jnp.concatenate(gates, axis=0)
    gate_ref[...] = gate / jnp.sum(gate, axis=0, keepdims=True) * ROUTED_SCALE
    selb = jnp.where(sel, 1.0, 0.0).astype(BF16)
    before = (lax.broadcasted_iota(I32, (tm, tm), 0) < lax.broadcasted_iota(I32, (tm, tm), 1))
    rank = jnp.dot(selb, jnp.where(before, 1.0, 0.0).astype(BF16), preferred_element_type=F32)
    rank = (rank + carry_ref[...]).astype(I32)
    rk = [jnp.sum(jnp.where(h, rank, 0), axis=0, keepdims=True) for h in hits]
    pk_ref[...] = (jnp.concatenate(eidx, axis=0) << 16) | jnp.concatenate(rk, axis=0)
    carry_ref[...] = carry_ref[...] + jnp.sum(selb.astype(F32), axis=1, keepdims=True)
    cnt_ref[...] = jnp.broadcast_to(carry_ref[...], cnt_ref.shape).astype(I32)


def _route(logits_t, bias, *, tm=512):
    return pl.pallas_call(
        functools.partial(_route_kernel, tm=tm),
        grid=(N_TOK // tm,),
        in_specs=[pl.BlockSpec((N_EXPERTS, tm), lambda i: (0, i)),
                  _resident((N_EXPERTS, 1))],
        out_specs=[pl.BlockSpec((TOP_K, tm), lambda i: (0, i)),
                   pl.BlockSpec((TOP_K, tm), lambda i: (0, i)),
                   pl.BlockSpec((N_EXPERTS, SLAB_LANES), lambda i: (0, 0))],
        out_shape=[jax.ShapeDtypeStruct((TOP_K, N_TOK), I32),
                   jax.ShapeDtypeStruct((TOP_K, N_TOK), F32),
                   jax.ShapeDtypeStruct((N_EXPERTS, SLAB_LANES), I32)],
        scratch_shapes=[pltpu.VMEM((N_EXPERTS, 1), F32)],
        compiler_params=pltpu.CompilerParams(dimension_semantics=("arbitrary",)),
        name="route",
    )(logits_t, bias)


def _expert_offsets(cnt_ref, ps_ref):
    def offsets(e, acc):
        ps_ref[e] = acc
        return acc + ((cnt_ref[e] + (ROW_BLOCK - 1)) // ROW_BLOCK) * ROW_BLOCK

    return lax.fori_loop(0, N_EXPERTS, offsets, jnp.int32(ROW_BLOCK))


def _positions_kernel(cnt_ref, pk_ref, pos_ref, ps_ref):
    _expert_offsets(cnt_ref, ps_ref)
    pk = pk_ref[...]
    eidx = pk >> 16
    pos = pk & 0xFFFF
    for e in range(N_EXPERTS):
        pos = pos + jnp.where(eidx == e, ps_ref[e], 0)
    pos_ref[...] = pos


def _positions(counts, packed):
    return pl.pallas_call(
        _positions_kernel,
        in_specs=[pl.BlockSpec(memory_space=pltpu.SMEM), pl.BlockSpec(memory_space=pltpu.VMEM)],
        out_specs=pl.BlockSpec(memory_space=pltpu.VMEM),
        out_shape=jax.ShapeDtypeStruct((TOP_K, N_TOK), I32),
        scratch_shapes=[pltpu.SMEM((N_EXPERTS,), I32)],
        name="positions",
    )(counts, packed)


def _finalize_kernel(pos_ref, cnt_ref, tok_ref, dst_ref, blk_ref, first_ref, nxt_ref, nused_ref,
                     srt_ref, ps_ref, *, tt):
    i = pl.program_id(0)
    lane = lax.broadcasted_iota(I32, (1, MAP_LANES), 1)

    @pl.when(i == 0)
    def _():
        total = _expert_offsets(cnt_ref, ps_ref)
        nused_ref[0] = total // ROW_BLOCK - 1
        entry = (lax.broadcasted_iota(I32, srt_ref.shape, 0) * MAP_LANES
                 + lax.broadcasted_iota(I32, srt_ref.shape, 1))
        srt_ref[...] = (N_TOK + lax.rem(entry // ROW_BLOCK + 1, 2) * ROW_BLOCK
                        + lax.rem(entry, ROW_BLOCK)) << TOK_BITS

        def defaults(b, c):
            blk_ref[b] = N_EXPERTS - 1
            first_ref[b] = 0
            nxt_ref[b] = -1
            return c

        lax.fori_loop(0, N_BLOCKS, defaults, 0)

        def per_expert(j, nxt_e):
            e = N_EXPERTS - 1 - j
            c = cnt_ref[e]
            nb = (c + (ROW_BLOCK - 1)) // ROW_BLOCK
            r0 = ps_ref[e]
            b0 = r0 // ROW_BLOCK - 1

            def blocks(jb, carry):
                blk_ref[b0 + jb] = e
                first_ref[b0 + jb] = jnp.where(jb == 0, 1, 0)
                nxt_ref[b0 + jb] = nxt_e
                return carry

            lax.fori_loop(0, nb, blocks, 0)
            return jnp.where(c > 0, e, nxt_e)

        lax.fori_loop(0, N_EXPERTS, per_expert, jnp.int32(-1))

    def per_token(tl, carry):
        t = i * tt + tl
        entry = t * ((1 << TOK_BITS) + 1)
        for k in range(TOP_K):
            p = pos_ref[t * TOP_K + k]
            val = jnp.full((1, MAP_LANES), entry + ((k * Y_ROWS) << TOK_BITS), I32)
            pltpu.store(srt_ref.at[pl.ds(p >> MAP_SHIFT, 1), :], val, mask=lane == (p & (MAP_LANES - 1)))
        return carry

    lax.fori_loop(0, tt, per_token, 0)

    @pl.when(i == pl.num_programs(0) - 1)
    def _():
        srt = srt_ref[...]
        tok_ref[...] = (srt & ((1 << TOK_BITS) - 1)) * SLAB_ROWS
        dst_ref[...] = (srt >> TOK_BITS) * SLAB_ROWS


def _finalize(pos_flat, counts, *, tt=512):
    smem = pl.BlockSpec(memory_space=pltpu.SMEM)
    row_map = pl.BlockSpec((MAP_ROWS, MAP_LANES), lambda i: (0, 0))
    return pl.pallas_call(
        functools.partial(_finalize_kernel, tt=tt),
        grid=(N_TOK // tt,),
        in_specs=[smem, smem],
        out_specs=[row_map, row_map] + [smem] * 4,
        out_shape=[jax.ShapeDtypeStruct((MAP_ROWS, MAP_LANES), I32),
                   jax.ShapeDtypeStruct((MAP_ROWS, MAP_LANES), I32),
                   jax.ShapeDtypeStruct((N_BLOCKS,), I32),
                   jax.ShapeDtypeStruct((N_BLOCKS,), I32),
                   jax.ShapeDtypeStruct((N_BLOCKS,), I32),
                   jax.ShapeDtypeStruct((1,), I32)],
        scratch_shapes=[pltpu.VMEM((MAP_ROWS, MAP_LANES), I32), pltpu.SMEM((N_EXPERTS,), I32)],
        compiler_params=pltpu.CompilerParams(dimension_semantics=("arbitrary",)),
        name="finalize",
    )(pos_flat, counts)


BLOCK_SLAB_ROWS = ROW_BLOCK * SLAB_ROWS
BLOCKS_PER_STEP = 2
CAST_VREGS = 64


def _cast_weight(src_ref, dst_ref):
    rows, cols = src_ref.shape
    step = CAST_VREGS * SLAB_ROWS * SLAB_LANES // cols

    def body(r, carry):
        r0 = pl.multiple_of(r * step, step)
        dst_ref[pl.ds(r0, step), :] = src_ref[pl.ds(r0, step), :].astype(BF16)
        return carry

    lax.fori_loop(0, rows // step, body, 0)


def _expert_kernel(blk_ref, first_ref, nxt_ref, nused_ref,
                   xs_ref, wg_hbm, wu_hbm, wd_hbm, tok_hbm, dst_hbm, y_hbm,
                   sg_ref, su_ref, sd_ref, bg_ref, bu_ref, bd_ref,
                   xg_ref, xb_ref, ys_ref, idx_ref, wsem, ysem, isem, *, layer):
    step = pl.program_id(0)
    nused = nused_ref[0]

    def weight_copies(ee):
        return (pltpu.make_async_copy(wg_hbm.at[layer, ee], sg_ref, wsem.at[0]),
                pltpu.make_async_copy(wu_hbm.at[layer, ee], su_ref, wsem.at[1]),
                pltpu.make_async_copy(wd_hbm.at[layer, ee], sd_ref, wsem.at[2]))

    def map_rows(blk):
        return pl.ds((blk + 1) * ROWS_PER_BLOCK, ROWS_PER_BLOCK)

    def idx_rows(j):
        return pl.ds(j * ROWS_PER_BLOCK, ROWS_PER_BLOCK)

    def idx_at(j, i):
        return idx_ref[j * ROWS_PER_BLOCK + i // MAP_LANES, i % MAP_LANES]

    def index_copies(b, slot):
        return (pltpu.make_async_copy(tok_hbm.at[map_rows(b + 1)], idx_ref.at[idx_rows(slot)],
                                      isem.at[slot]),
                pltpu.make_async_copy(dst_hbm.at[map_rows(b - 1)], idx_ref.at[idx_rows(2 + slot)],
                                      isem.at[2 + slot]))

    def wait_rows(slot):
        pltpu.make_async_copy(ys_ref.at[slot], y_hbm.at[pl.ds(0, BLOCK_SLAB_ROWS)],
                              ysem.at[slot]).wait()

    def gather_block(idx_row, slot):
        for i in range(ROW_BLOCK):
            xg_ref[pl.ds(i * SLAB_ROWS, SLAB_ROWS), :] = (
                xs_ref[pl.ds(pl.multiple_of(idx_at(idx_row, i), SLAB_ROWS), SLAB_ROWS), :])
        for c in range(SLAB_ROWS):
            w = xg_ref[pl.ds(c, ROW_BLOCK, stride=SLAB_ROWS), :]
            xb_ref[slot, :, _lo_col(c):_lo_col(c) + SLAB_LANES] = _unpack_lo(w).astype(BF16)
            xb_ref[slot, :, _hi_col(c):_hi_col(c) + SLAB_LANES] = _unpack_hi(w).astype(BF16)

    @pl.when(step == 0)
    def _():
        for cp in weight_copies(blk_ref[0]):
            cp.start()
        ys_ref[...] = jnp.zeros_like(ys_ref)
        zero_copies = [
            pltpu.make_async_copy(
                ys_ref.at[0],
                y_hbm.at[pl.ds((k * Y_ROWS + N_TOK + s * ROW_BLOCK) * SLAB_ROWS, BLOCK_SLAB_ROWS)],
                ysem.at[0])
            for k in range(TOP_K) for s in range(2)]
        for cp in zero_copies:
            cp.start()
        for cp in zero_copies:
            cp.wait()
        first_rows = pltpu.make_async_copy(tok_hbm.at[map_rows(0)], idx_ref.at[idx_rows(1)], isem.at[1])
        first_rows.start()
        first_rows.wait()
        gather_block(1, 0)
        for cp in index_copies(0, 0):
            cp.start()

    def one_block(b, cur):
        prev = 1 - cur

        @pl.when(b <= nused)
        def _():
            for cp in index_copies(b, cur):
                cp.wait()

            @pl.when(b < nused)
            def _():
                for cp in index_copies(b + 1, prev):
                    cp.start()

            @pl.when(first_ref[b] == 1)
            def _():
                for cp in weight_copies(blk_ref[b]):
                    cp.wait()
                _cast_weight(sg_ref, bg_ref)
                _cast_weight(su_ref, bu_ref)
                _cast_weight(sd_ref, bd_ref)

                @pl.when(nxt_ref[b] >= 0)
                def _():
                    for cp in weight_copies(nxt_ref[b]):
                        cp.start()

            @pl.when(b > 0)
            def _():
                wait_rows(cur)

            for i in range(ROW_BLOCK):
                pltpu.make_async_copy(
                    ys_ref.at[prev, pl.ds(i * SLAB_ROWS, SLAB_ROWS)],
                    y_hbm.at[pl.ds(pl.multiple_of(idx_at(2 + cur, i), SLAB_ROWS), SLAB_ROWS)],
                    ysem.at[prev]).start()

            x = xb_ref[cur]
            hg = jnp.dot(x, bg_ref[...], preferred_element_type=F32)
            hu = jnp.dot(x, bu_ref[...], preferred_element_type=F32)
            act = (_silu(hg) * hu).astype(BF16)
            pk = _pack_rows(jnp.dot(act, bd_ref[...], preferred_element_type=F32))
            for c in range(SLAB_ROWS):
                ys_ref[cur, pl.ds(c, ROW_BLOCK, stride=SLAB_ROWS), :] = pk[c]
            gather_block(cur, prev)

            @pl.when(b == nused)
            def _():
                wait_rows(prev)

    for j in range(BLOCKS_PER_STEP):
        one_block(step * BLOCKS_PER_STEP + j, j)


def _experts(blk, first, nxt, nused, tok_rows, dst_rows, slabs, w_gate, w_up, w_down, *, layer):
    grid_spec = pltpu.PrefetchScalarGridSpec(
        num_scalar_prefetch=4,
        grid=(N_BLOCKS // BLOCKS_PER_STEP,),
        in_specs=[pl.BlockSpec((N_TOK * SLAB_ROWS, SLAB_LANES), lambda b, *_: (0, 0),
                               pipeline_mode=pl.Buffered(1))]
                 + [pl.BlockSpec(memory_space=pl.ANY)] * 5,
        out_specs=pl.BlockSpec(memory_space=pl.ANY),
        scratch_shapes=[pltpu.VMEM((D_MODEL, EXPERT_DIM), F32),
                        pltpu.VMEM((D_MODEL, EXPERT_DIM), F32),
                        pltpu.VMEM((EXPERT_DIM, D_MODEL), F32),
                        pltpu.VMEM((D_MODEL, EXPERT_DIM), BF16),
                        pltpu.VMEM((D_MODEL, EXPERT_DIM), BF16),
                        pltpu.VMEM((EXPERT_DIM, D_MODEL), BF16),
                        pltpu.VMEM((BLOCK_SLAB_ROWS, SLAB_LANES), U32),
                        pltpu.VMEM((2, ROW_BLOCK, D_MODEL), BF16),
                        pltpu.VMEM((2, BLOCK_SLAB_ROWS, SLAB_LANES), U32),
                        pltpu.SMEM((4 * ROWS_PER_BLOCK, MAP_LANES), I32),
                        pltpu.SemaphoreType.DMA((3,)),
                        pltpu.SemaphoreType.DMA((2,)),
                        pltpu.SemaphoreType.DMA((4,))],
    )
    return pl.pallas_call(
        functools.partial(_expert_kernel, layer=layer),
        grid_spec=grid_spec,
        out_shape=jax.ShapeDtypeStruct((TOP_K * Y_ROWS * SLAB_ROWS, SLAB_LANES), U32),
        compiler_params=pltpu.CompilerParams(
            dimension_semantics=("arbitrary",), vmem_limit_bytes=VMEM_LIMIT),
        name="experts",
    )(blk, first, nxt, nused, slabs, w_gate, w_up, w_down, tok_rows, dst_rows)


def _combine_kernel(x_ref, y_ref, gate_ref, sg_ref, su_ref, sd_ref, g_ref, b_ref, o_ref, r_ref, *, tm):
    x = x_ref[...]
    xb = x.astype(BF16)
    hg = jnp.dot(xb, sg_ref[...], preferred_element_type=F32)
    hu = jnp.dot(xb, su_ref[...], preferred_element_type=F32)
    shared = jnp.dot((_silu(hg) * hu).astype(BF16), sd_ref[...], preferred_element_type=F32)
    gate = gate_ref[...]
    for c in range(SLAB_ROWS):
        lo = jnp.zeros((tm, SLAB_LANES), F32)
        hi = jnp.zeros((tm, SLAB_LANES), F32)
        for k in range(TOP_K):
            w = y_ref[k, pl.ds(c, tm, stride=SLAB_ROWS), :]
            gk = gate[:, k:k + 1]
            lo = lo + gk * _unpack_lo(w)
            hi = hi + gk * _unpack_hi(w)
        r_ref[:, _lo_col(c):_lo_col(c) + SLAB_LANES] = lo
        r_ref[:, _hi_col(c):_hi_col(c) + SLAB_LANES] = hi
    z = ALPHA * x + (r_ref[...] + shared)
    o_ref[...] = _layer_norm(z, g_ref[...], b_ref[...])


def _combine(x2d, y8, gate_tk, s_gate, s_up, s_down, ln_g, ln_b, *, tm=256):
    return pl.pallas_call(
        functools.partial(_combine_kernel, tm=tm),
        grid=(N_TOK // tm,),
        in_specs=[pl.BlockSpec((tm, D_MODEL), lambda i: (i, 0)),
                  pl.BlockSpec((TOP_K, tm * SLAB_ROWS, SLAB_LANES), lambda i: (0, i, 0)),
                  pl.BlockSpec((tm, TOP_K), lambda i: (i, 0)),
                  _resident((D_MODEL, EXPERT_DIM)), _resident((D_MODEL, EXPERT_DIM)),
                  _resident((EXPERT_DIM, D_MODEL)),
                  _resident((1, D_MODEL)), _resident((1, D_MODEL))],
        out_specs=pl.BlockSpec((tm, D_MODEL), lambda i: (i, 0)),
        out_shape=jax.ShapeDtypeStruct((N_TOK, D_MODEL), F32),
        scratch_shapes=[pltpu.VMEM((tm, D_MODEL), F32)],
        compiler_params=pltpu.CompilerParams(
            dimension_semantics=("arbitrary",), vmem_limit_bytes=VMEM_LIMIT),
        name="combine",
    )(x2d, y8, gate_tk, s_gate, s_up, s_down, ln_g, ln_b)


def _moe(x_f32, slabs, logits_t, layer, router_bias, exp_w_gate, exp_w_up, exp_w_down,
         s_gate, s_up, s_down, ln_g, ln_b):
    packed, gate8, counts = _route(logits_t, router_bias[layer].reshape(N_EXPERTS, 1))
    counts = counts[:, 0]
    tok_rows, dst_rows, blk, first, nxt, nused = _finalize(
        _positions(counts, packed).T.reshape(-1), counts)
    y8 = _experts(blk, first, nxt, nused, tok_rows, dst_rows, slabs, exp_w_gate, exp_w_up, exp_w_down,
                  layer=layer)
    y8 = y8.reshape(TOP_K, Y_ROWS * SLAB_ROWS, SLAB_LANES)
    return _combine(x_f32, y8, gate8.T, s_gate[layer].astype(BF16), s_up[layer].astype(BF16),
                    s_down[layer].astype(BF16), ln_g[layer].reshape(1, D_MODEL),
                    ln_b[layer].reshape(1, D_MODEL))


W1_COLS = KV_LORA_RANK + 2 * QK_ROPE_DIM + Q_LORA_RANK
Q_HEAD_COLS = QK_NOPE_DIM + 2 * QK_ROPE_DIM
LOG2_E = 1.4426950408889634
Q_SCALE = QK_DIM ** -0.5 * LOG2_E
FLASH_CHAIN = 256


def _rope_chunk(chunk, cs):
    u = chunk * cs
    return u + pltpu.roll(u, QK_ROPE_DIM, axis=1)


def _proj_kernel(x_ref, cs_ref, w1_ref, kvn_ref, qn_ref, wuk_ref, wuv_ref, wuq_ref,
                 q_ref, kn_ref, kr_ref, v_ref, *, tm):
    xb = x_ref[...].astype(BF16)
    h = jnp.dot(xb, w1_ref[...], preferred_element_type=F32)
    cs = cs_ref[...]
    c = _rms_norm(h[:, :KV_LORA_RANK], kvn_ref[...]).astype(BF16)
    kr = _rope_chunk(h[:, KV_LORA_RANK:KV_LORA_RANK + 2 * QK_ROPE_DIM], cs)
    lane = lax.broadcasted_iota(I32, (tm, 2 * QK_ROPE_DIM), 1)
    kr_ref[...] = jnp.where(lane < QK_ROPE_DIM, kr, 0.0).astype(BF16)
    cq = _rms_norm(h[:, KV_LORA_RANK + 2 * QK_ROPE_DIM:], qn_ref[...]).astype(BF16)
    kn = jnp.dot(c, wuk_ref[...], preferred_element_type=F32)
    vt = lax.dot_general(wuv_ref[...], c, NT_DIMS, preferred_element_type=F32)
    q = jnp.dot(cq, wuq_ref[...], preferred_element_type=F32)
    for hd in range(N_HEADS):
        kn_ref[hd] = kn[:, hd * QK_NOPE_DIM:(hd + 1) * QK_NOPE_DIM].astype(BF16)
        v_ref[hd] = vt[hd * V_DIM:(hd + 1) * V_DIM, :].astype(BF16)
        q0 = hd * Q_HEAD_COLS
        q_ref[hd, :, :QK_NOPE_DIM] = (q[:, q0:q0 + QK_NOPE_DIM] * Q_SCALE).astype(BF16)
        q_ref[hd, :, QK_NOPE_DIM:] = (
            _rope_chunk(q[:, q0 + QK_NOPE_DIM:q0 + Q_HEAD_COLS], cs) * Q_SCALE).astype(BF16)


def _proj(x2d, cs, w1, kv_norm, q_norm, w_uk, w_uv, w_uq, *, tm=256):
    nj = SEQ // tm
    head_spec = lambda d: pl.BlockSpec((None, N_HEADS, tm, d), lambda b, j: (b, 0, j, 0))
    return pl.pallas_call(
        functools.partial(_proj_kernel, tm=tm),
        grid=(BATCH, nj),
        in_specs=[pl.BlockSpec((tm, D_MODEL), lambda b, j: (b * nj + j, 0)),
                  pl.BlockSpec((tm, 2 * QK_ROPE_DIM), lambda b, j: (j, 0)),
                  _resident(w1.shape), _resident((1, KV_LORA_RANK)), _resident((1, Q_LORA_RANK)),
                  _resident(w_uk.shape), _resident(w_uv.shape), _resident(w_uq.shape)],
        out_specs=[head_spec(Q_HEAD_COLS), head_spec(QK_NOPE_DIM),
                   pl.BlockSpec((None, tm, 2 * QK_ROPE_DIM), lambda b, j: (b, j, 0)),
                   pl.BlockSpec((None, N_HEADS, V_DIM, tm), lambda b, j: (b, 0, 0, j))],
        out_shape=[jax.ShapeDtypeStruct((BATCH, N_HEADS, SEQ, Q_HEAD_COLS), BF16),
                   jax.ShapeDtypeStruct((BATCH, N_HEADS, SEQ, QK_NOPE_DIM), BF16),
                   jax.ShapeDtypeStruct((BATCH, SEQ, 2 * QK_ROPE_DIM), BF16),
                   jax.ShapeDtypeStruct((BATCH, N_HEADS, V_DIM, SEQ), BF16)],
        compiler_params=pltpu.CompilerParams(
            dimension_semantics=("arbitrary", "arbitrary"), vmem_limit_bytes=VMEM_LIMIT),
        name="proj",
    )(x2d, cs, w1, kv_norm, q_norm, w_uk, w_uv, w_uq)


def _flash_kernel(q_ref, kn_ref, kr_ref, vt_ref, o_ref, sa_ref, sb_ref, *, tq):
    qi = pl.program_id(2)
    n_chain = tq // FLASH_CHAIN
    qs = [q_ref[c * FLASH_CHAIN:(c + 1) * FLASH_CHAIN, :] for c in range(n_chain)]

    def scores_into(j, s_ref):
        k0 = pl.multiple_of(j * tq, tq)
        k = jnp.concatenate([kn_ref[pl.ds(k0, tq), :], kr_ref[pl.ds(k0, tq), :]], axis=1)
        for c, qc in enumerate(qs):
            s_ref[:, c * FLASH_CHAIN:(c + 1) * FLASH_CHAIN] = (
                lax.dot_general(k, qc, NT_DIMS, preferred_element_type=F32))

    def consume(j, stats, s_ref, diagonal):
        vt = vt_ref[:, pl.ds(pl.multiple_of(j * tq, tq), tq)]
        out = []
        for c, (m, l, acc) in enumerate(stats):
            s = s_ref[:, c * FLASH_CHAIN:(c + 1) * FLASH_CHAIN]
            if diagonal:
                key = lax.broadcasted_iota(I32, (tq, FLASH_CHAIN), 0)
                qry = lax.broadcasted_iota(I32, (tq, FLASH_CHAIN), 1) + c * FLASH_CHAIN
                s = jnp.where(key <= qry, s, -jnp.inf)
            m_new = jnp.maximum(m, jnp.max(s, axis=0, keepdims=True))
            p = jnp.exp2(s - m_new)
            a = jnp.exp2(m - m_new)
            l = a * l + jnp.sum(p, axis=0, keepdims=True)
            acc = a * acc + jnp.dot(vt, p.astype(BF16), preferred_element_type=F32)
            out.append((m_new, l, acc))
        return tuple(out)

    def pair(i2, stats):
        j = 2 * i2
        scores_into(j + 1, sb_ref)
        stats = consume(j, stats, sa_ref, False)
        scores_into(j + 2, sa_ref)
        return consume(j + 1, stats, sb_ref, False)

    def finish(stats):
        for c, (_, l, acc) in enumerate(stats):
            o_ref[c * FLASH_CHAIN:(c + 1) * FLASH_CHAIN, :] = (acc / l).T.astype(BF16)

    init = tuple((jnp.full((1, FLASH_CHAIN), -jnp.inf, F32), jnp.zeros((1, FLASH_CHAIN), F32),
                  jnp.zeros((V_DIM, FLASH_CHAIN), F32)) for _ in range(n_chain))
    scores_into(0, sa_ref)
    stats = lax.fori_loop(0, qi // 2, pair, init)

    @pl.when(qi % 2 == 1)
    def _():
        scores_into(qi, sb_ref)
        finish(consume(qi, consume(qi - 1, stats, sa_ref, False), sb_ref, True))

    @pl.when(qi % 2 == 0)
    def _():
        finish(consume(qi, stats, sa_ref, True))


def _flash(q, kn, kr, v, *, tq=512):
    return pl.pallas_call(
        functools.partial(_flash_kernel, tq=tq),
        grid=(BATCH, N_HEADS, SEQ // tq),
        in_specs=[pl.BlockSpec((None, None, tq, Q_HEAD_COLS), lambda b, h, i: (b, h, i, 0)),
                  pl.BlockSpec((None, None, SEQ, QK_NOPE_DIM), lambda b, h, i: (b, h, 0, 0)),
                  pl.BlockSpec((None, SEQ, 2 * QK_ROPE_DIM), lambda b, h, i: (b, 0, 0)),
                  pl.BlockSpec((None, None, V_DIM, SEQ), lambda b, h, i: (b, h, 0, 0))],
        out_specs=pl.BlockSpec((None, tq, V_DIM), lambda b, h, i: (b, i, h)),
        out_shape=jax.ShapeDtypeStruct((BATCH, SEQ, N_HEADS * V_DIM), BF16),
        scratch_shapes=[pltpu.VMEM((tq, tq), F32), pltpu.VMEM((tq, tq), F32)],
        compiler_params=pltpu.CompilerParams(
            dimension_semantics=("arbitrary", "arbitrary", "arbitrary"), vmem_limit_bytes=VMEM_LIMIT),
        name="flash",
    )(q, kn, kr, v)


def _attn_out_kernel(a_ref, x_ref, wo_ref, g_ref, b_ref, rw_ref, xo_ref, slab_ref, lg_ref, *, tm):
    ys = [jnp.dot(a_ref[s * SUB_ROWS:(s + 1) * SUB_ROWS, :], wo_ref[...], preferred_element_type=F32)
          for s in range(tm // SUB_ROWS)]
    _ln1_epilogue(ys, x_ref, g_ref, b_ref, rw_ref, xo_ref, slab_ref, lg_ref)


def _attn_out(attn2d, x2d, w_out, ln_g, ln_b, rw, *, tm=512):
    out_specs, out_shape = _ln1_out_specs(tm)
    return pl.pallas_call(
        functools.partial(_attn_out_kernel, tm=tm),
        grid=(N_TOK // tm,),
        in_specs=[pl.BlockSpec((tm, D_MODEL), lambda i: (i, 0)),
                  pl.BlockSpec((tm, D_MODEL), lambda i: (i, 0)),
                  _resident((D_MODEL, D_MODEL)),
                  _resident((1, D_MODEL)), _resident((1, D_MODEL)),
                  _resident((2, N_EXPERTS, D_MODEL))],
        out_specs=out_specs, out_shape=out_shape,
        compiler_params=pltpu.CompilerParams(
            dimension_semantics=("arbitrary",), vmem_limit_bytes=VMEM_LIMIT),
        name="attn_out",
    )(attn2d, x2d, w_out, ln_g, ln_b, rw)


def _rope_table():
    inv_freq = 1.0 / (ROPE_THETA ** (jnp.arange(0, QK_ROPE_DIM, 2, dtype=F32) / QK_ROPE_DIM))
    ang = jnp.arange(SEQ, dtype=F32)[:, None] * inv_freq[None, :]
    cos, sin = jnp.cos(ang), jnp.sin(ang)
    return jnp.concatenate([cos, cos, -sin, sin], axis=1)


def _swap_halves(w):
    half = QK_ROPE_DIM // 2
    return jnp.concatenate([w[..., half:], w[..., :half]], axis=-1)


def kernel(x, a_w_in, a_w_grp, a_scale, a_w_out, kv_w_down, kv_norm, kv_w_uk, kv_w_uv, b_w_dq, b_q_norm, b_w_uq, b_w_out, ln1_g, ln1_b, ln2_g, ln2_b, router_w, router_bias, exp_w_gate, exp_w_up, exp_w_down, sh_w_gate, sh_w_up, sh_w_down):
    x2d = x.reshape(N_TOK, D_MODEL)
    row = lambda v: v.reshape(1, -1)
    moe_args = (router_bias, exp_w_gate, exp_w_up, exp_w_down, sh_w_gate, sh_w_up, sh_w_down, ln2_g, ln2_b)

    pooled = _pool_in(x2d, a_w_in[0].astype(BF16))
    x1, slabs, logits_t = _mix_out(pooled, x2d, a_w_grp[0].astype(BF16), row(a_scale[0]),
                                   a_w_out[0].astype(BF16), row(ln1_g[0]), row(ln1_b[0]),
                                   _split_bf16(router_w[0].T))
    x2 = _moe(x1, slabs, logits_t, 0, *moe_args)

    kr_w = kv_w_down[:, KV_LORA_RANK:]
    w1 = jnp.concatenate([kv_w_down[:, :KV_LORA_RANK], kr_w, _swap_halves(kr_w), b_w_dq[0]],
                         axis=1).astype(BF16)
    uq = b_w_uq[0]
    uq_rope = uq[:, :, QK_NOPE_DIM:]
    w_uq = jnp.concatenate([uq[:, :, :QK_NOPE_DIM], uq_rope, _swap_halves(uq_rope)], axis=2)
    w_uq = w_uq.reshape(Q_LORA_RANK, N_HEADS * Q_HEAD_COLS).astype(BF16)
    q, kn, kr, v = _proj(x2, _rope_table(), w1, row(kv_norm), row(b_q_norm[0]),
                         kv_w_uk.reshape(KV_LORA_RANK, N_HEADS * QK_NOPE_DIM).astype(BF16),
                         kv_w_uv.reshape(KV_LORA_RANK, N_HEADS * V_DIM).T.astype(BF16), w_uq)
    attn = _flash(q, kn, kr, v).reshape(N_TOK, N_HEADS * V_DIM)
    x3, slabs, logits_t = _attn_out(attn, x2, b_w_out[0].astype(BF16), row(ln1_g[1]), row(ln1_b[1]),
                                    _split_bf16(router_w[1].T))
    x4 = _moe(x3, slabs, logits_t, 1, *moe_args)
    return x4.reshape(BATCH, SEQ, D_MODEL)
```

```python
import functools

import jax
import jax.numpy as jnp
from jax import lax
from jax.experimental import pallas as pl
from jax.experimental.pallas import tpu as pltpu

D_MODEL = 2048
BATCH = 2
SEQ = 4096
DEPTH = 2
N_TOK = BATCH * SEQ
ALPHA = (2.0 * DEPTH) ** 0.25
POOL_WINDOWS = (2, 4, 8, 16)
POOL_GROUP_DIM = D_MODEL // len(POOL_WINDOWS)
POOL_HALO = 16
N_HEADS = 16
QK_NOPE_DIM = 128
QK_ROPE_DIM = 64
QK_DIM = QK_NOPE_DIM + QK_ROPE_DIM
V_DIM = 128
Q_LORA_RANK = D_MODEL // 4
KV_LORA_RANK = D_MODEL // 4
ROPE_THETA = 10000.0
N_EXPERTS = 64
TOP_K = 8
N_EXPERT_GROUPS = 8
GROUP_SIZE = N_EXPERTS // N_EXPERT_GROUPS
TOPK_GROUPS = 4
EXPERT_DIM = D_MODEL // 4
ROUTED_SCALE = 2.5
ROW_BLOCK = 256
MAP_LANES = 128
MAP_SHIFT = MAP_LANES.bit_length() - 1
ROWS_PER_BLOCK = ROW_BLOCK // MAP_LANES
assert 1 << MAP_SHIFT == MAP_LANES and ROWS_PER_BLOCK * MAP_LANES == ROW_BLOCK
LN_EPS = 1e-5
RMS_EPS = 1e-6

N_ASSIGN = N_TOK * TOP_K
N_BLOCKS = N_ASSIGN // ROW_BLOCK + N_EXPERTS
MAP_ROWS = (N_BLOCKS + 4) * ROWS_PER_BLOCK
Y_ROWS = N_TOK + 2 * ROW_BLOCK
TOK_BITS = 14
assert N_TOK <= 1 << TOK_BITS and TOP_K * Y_ROWS < 1 << (31 - TOK_BITS)
SLAB_ROWS = 8
SLAB_LANES = 128
SUB_ROWS = 128
V7X_VMEM_BYTES = 64 * 1024 * 1024
VMEM_LIMIT = V7X_VMEM_BYTES - 4 * 1024 * 1024

F32 = jnp.float32
BF16 = jnp.bfloat16
I32 = jnp.int32
U32 = jnp.uint32
NT_DIMS = (((1,), (1,)), ((), ()))


def _lo_col(c):
    return 2 * c * SLAB_LANES


def _hi_col(c):
    return (2 * c + 1) * SLAB_LANES


def _pack_rows(z):
    return [pltpu.pack_elementwise([z[:, _lo_col(c):_lo_col(c) + SLAB_LANES],
                                    z[:, _hi_col(c):_hi_col(c) + SLAB_LANES]], packed_dtype=BF16)
            for c in range(SLAB_ROWS)]


def _unpack_lo(w):
    return pltpu.unpack_elementwise(w, index=0, packed_dtype=BF16, unpacked_dtype=F32)


def _unpack_hi(w):
    return pltpu.unpack_elementwise(w, index=1, packed_dtype=BF16, unpacked_dtype=F32)


def _silu(x):
    return x * jax.nn.sigmoid(x)


def _layer_norm(z, g, b):
    mu = jnp.mean(z, axis=-1, keepdims=True)
    zc = z - mu
    var = jnp.mean(zc * zc, axis=-1, keepdims=True)
    return zc * lax.rsqrt(var + LN_EPS) * g + b


def _rms_norm(z, g):
    ms = jnp.mean(z * z, axis=-1, keepdims=True)
    return z * lax.rsqrt(ms + RMS_EPS) * g


def _resident(shape):
    nd = len(shape)
    return pl.BlockSpec(shape, lambda *_: (0,) * nd, pipeline_mode=pl.Buffered(1))


def _pool_in_kernel(x_ref, w_ref, o_ref, tail_ref, *, tm):
    j = pl.program_id(1)

    @pl.when(j == 0)
    def _():
        tail_ref[...] = jnp.zeros_like(tail_ref)

    hs = [jnp.dot(x_ref[r0:r0 + SUB_ROWS, :].astype(BF16), w_ref[...], preferred_element_type=F32)
          for r0 in range(0, tm, SUB_ROWS)]
    tail = tail_ref[...]
    for n, h in enumerate(hs):
        r0 = n * SUB_ROWS
        ext = jnp.concatenate([tail, h], axis=0)
        tail = h[SUB_ROWS - POOL_HALO:, :]
        pos = j * tm + r0 + lax.broadcasted_iota(I32, (SUB_ROWS, 1), 0)
        for g, w in enumerate(POOL_WINDOWS):
            c0, c1 = g * POOL_GROUP_DIM, (g + 1) * POOL_GROUP_DIM
            s = ext[:, c0:c1]
            sh = 1
            while sh < w:
                s = s + pltpu.roll(s, sh, axis=0)
                sh *= 2
            inv = 1.0 / jnp.minimum(pos + 1, w).astype(F32)
            o_ref[r0:r0 + SUB_ROWS, c0:c1] = (s[POOL_HALO:, :] * inv - h[:, c0:c1]).astype(BF16)
    tail_ref[...] = tail


def _pool_in(x2d, w_in, *, tm=512):
    nj = SEQ // tm
    return pl.pallas_call(
        functools.partial(_pool_in_kernel, tm=tm),
        grid=(BATCH, nj),
        in_specs=[pl.BlockSpec((tm, D_MODEL), lambda b, j: (b * nj + j, 0)),
                  _resident((D_MODEL, D_MODEL))],
        out_specs=pl.BlockSpec((tm, D_MODEL), lambda b, j: (b * nj + j, 0)),
        out_shape=jax.ShapeDtypeStruct((N_TOK, D_MODEL), BF16),
        scratch_shapes=[pltpu.VMEM((POOL_HALO, D_MODEL), F32)],
        compiler_params=pltpu.CompilerParams(
            dimension_semantics=("arbitrary", "arbitrary"), vmem_limit_bytes=VMEM_LIMIT),
        name="pool_in",
    )(x2d, w_in)


def _ln1_epilogue(ys, x_ref, g_ref, b_ref, rw_ref, xo_ref, slab_ref, lg_ref):
    nt = lambda a, b: lax.dot_general(a, b, NT_DIMS, preferred_element_type=F32)
    for s, y in enumerate(ys):
        r0 = s * SUB_ROWS
        xn = _layer_norm(ALPHA * x_ref[r0:r0 + SUB_ROWS, :] + y, g_ref[...], b_ref[...])
        xo_ref[r0:r0 + SUB_ROWS, :] = xn
        pk = _pack_rows(xn)
        for c in range(SLAB_ROWS):
            slab_ref[pl.ds(r0 * SLAB_ROWS + c, SUB_ROWS, stride=SLAB_ROWS), :] = pk[c]
        x_hi = xn.astype(BF16)
        x_lo = (xn - x_hi.astype(F32)).astype(BF16)
        lg_ref[:, r0:r0 + SUB_ROWS] = nt(rw_ref[0], x_hi) + (nt(rw_ref[0], x_lo) + nt(rw_ref[1], x_hi))


def _mix_out_kernel(p_ref, x_ref, wg_ref, sc_ref, wo_ref, g_ref, b_ref, rw_ref,
                    xo_ref, slab_ref, lg_ref, *, tm):
    ys = []
    for s in range(tm // SUB_ROWS):
        r0 = s * SUB_ROWS
        parts = []
        for g in range(len(POOL_WINDOWS)):
            c0, c1 = g * POOL_GROUP_DIM, (g + 1) * POOL_GROUP_DIM
            parts.append(jnp.dot(p_ref[r0:r0 + SUB_ROWS, c0:c1], wg_ref[g], preferred_element_type=F32))
        mixed = (jnp.concatenate(parts, axis=1) * sc_ref[...]).astype(BF16)
        ys.append(jnp.dot(mixed, wo_ref[...], preferred_element_type=F32))
    _ln1_epilogue(ys, x_ref, g_ref, b_ref, rw_ref, xo_ref, slab_ref, lg_ref)


def _ln1_out_specs(tm):
    return (
        [pl.BlockSpec((tm, D_MODEL), lambda i: (i, 0)),
         pl.BlockSpec((tm * SLAB_ROWS, SLAB_LANES), lambda i: (i, 0)),
         pl.BlockSpec((N_EXPERTS, tm), lambda i: (0, i))],
        [jax.ShapeDtypeStruct((N_TOK, D_MODEL), F32),
         jax.ShapeDtypeStruct((N_TOK * SLAB_ROWS, SLAB_LANES), U32),
         jax.ShapeDtypeStruct((N_EXPERTS, N_TOK), F32)],
    )


def _split_bf16(w):
    hi = w.astype(BF16)
    return jnp.stack([hi, (w - hi.astype(F32)).astype(BF16)])


def _mix_out(pooled, x2d, w_grp, scale, w_out, ln_g, ln_b, rw, *, tm=512):
    out_specs, out_shape = _ln1_out_specs(tm)
    return pl.pallas_call(
        functools.partial(_mix_out_kernel, tm=tm),
        grid=(N_TOK // tm,),
        in_specs=[pl.BlockSpec((tm, D_MODEL), lambda i: (i, 0)),
                  pl.BlockSpec((tm, D_MODEL), lambda i: (i, 0)),
                  _resident(w_grp.shape), _resident((1, D_MODEL)), _resident((D_MODEL, D_MODEL)),
                  _resident((1, D_MODEL)), _resident((1, D_MODEL)),
                  _resident((2, N_EXPERTS, D_MODEL))],
        out_specs=out_specs, out_shape=out_shape,
        compiler_params=pltpu.CompilerParams(
            dimension_semantics=("arbitrary",), vmem_limit_bytes=VMEM_LIMIT),
        name="mix_out",
    )(pooled, x2d, w_grp, scale, w_out, ln_g, ln_b, rw)


def _first_index(hit_src, best, iota, n):
    return jnp.min(jnp.where(hit_src == best, iota, n), axis=0, keepdims=True)


def _route_kernel(lg_ref, bias_ref, pk_ref, gate_ref, cnt_ref, carry_ref, *, tm):
    i = pl.program_id(0)

    @pl.when(i == 0)
    def _():
        carry_ref[...] = jnp.zeros_like(carry_ref)

    neg = -jnp.inf
    scores = jax.nn.sigmoid(lg_ref[...])
    choice = scores + bias_ref[...]
    iota_g = lax.broadcasted_iota(I32, (GROUP_SIZE, tm), 0)
    gscore = []
    for g in range(N_EXPERT_GROUPS):
        c = choice[g * GROUP_SIZE:(g + 1) * GROUP_SIZE, :]
        m1 = jnp.max(c, axis=0, keepdims=True)
        f1 = _first_index(c, m1, iota_g, GROUP_SIZE)
        m2 = jnp.max(jnp.where(iota_g == f1, neg, c), axis=0, keepdims=True)
        gscore.append(m1 + m2)
    gs = jnp.concatenate(gscore, axis=0)
    iota_ng = lax.broadcasted_iota(I32, (N_EXPERT_GROUPS, tm), 0)
    gsel = jnp.zeros((N_EXPERT_GROUPS, tm), jnp.bool_)
    for _ in range(TOPK_GROUPS):
        m = jnp.max(gs, axis=0, keepdims=True)
        hit = iota_ng == _first_index(gs, m, iota_ng, N_EXPERT_GROUPS)
        gsel = gsel | hit
        gs = jnp.where(hit, neg, gs)
    masked = jnp.concatenate(
        [jnp.where(gsel[g:g + 1, :], choice[g * GROUP_SIZE:(g + 1) * GROUP_SIZE, :], neg)
         for g in range(N_EXPERT_GROUPS)], axis=0)
    iota_e = lax.broadcasted_iota(I32, (N_EXPERTS, tm), 0)
    hits, eidx, gates = [], [], []
    sel = jnp.zeros((N_EXPERTS, tm), jnp.bool_)
    for _ in range(TOP_K):
        m = jnp.max(masked, axis=0, keepdims=True)
        f = _first_index(masked, m, iota_e, N_EXPERTS)
        hit = iota_e == f
        hits.append(hit)
        eidx.append(f)
        gates.append(jnp.sum(jnp.where(hit, scores, 0.0), axis=0, keepdims=True))
        masked = jnp.where(hit, neg, masked)
        sel = sel | hit
    gate = jnp.concatenate(gates, axis=0)
    gate_ref[...] = gate / jnp.sum(gate, axis=0, keepdims=True) * ROUTED_SCALE
    selb = jnp.where(sel, 1.0, 0.0).astype(BF16)
    before = (lax.broadcasted_iota(I32, (tm, tm), 0) < lax.broadcasted_iota(I32, (tm, tm), 1))
    rank = jnp.dot(selb, jnp.where(before, 1.0, 0.0).astype(BF16), preferred_element_type=F32)
    rank = (rank + carry_ref[...]).astype(I32)
    rk = [jnp.sum(jnp.where(h, rank, 0), axis=0, keepdims=True) for h in hits]
    pk_ref[...] = (jnp.concatenate(eidx, axis=0) << 16) | jnp.concatenate(rk, axis=0)
    carry_ref[...] = carry_ref[...] + jnp.sum(selb.astype(F32), axis=1, keepdims=True)
    cnt_ref[...] = jnp.broadcast_to(carry_ref[...], cnt_ref.shape).astype(I32)


def _route(logits_t, bias, *, tm=512):
    return pl.pallas_call(
        functools.partial(_route_kernel, tm=tm),
        grid=(N_TOK // tm,),
        in_specs=[pl.BlockSpec((N_EXPERTS, tm), lambda i: (0, i)),
                  _resident((N_EXPERTS, 1))],
        out_specs=[pl.BlockSpec((TOP_K, tm), lambda i: (0, i)),
                   pl.BlockSpec((TOP_K, tm), lambda i: (0, i)),
                   pl.BlockSpec((N_EXPERTS, SLAB_LANES), lambda i: (0, 0))],
        out_shape=[jax.ShapeDtypeStruct((TOP_K, N_TOK), I32),
                   jax.ShapeDtypeStruct((TOP_K, N_TOK), F32),
                   jax.ShapeDtypeStruct((N_EXPERTS, SLAB_LANES), I32)],
        scratch_shapes=[pltpu.VMEM((N_EXPERTS, 1), F32)],
        compiler_params=pltpu.CompilerParams(dimension_semantics=("arbitrary",)),
        name="route",
    )(logits_t, bias)


def _expert_offsets(cnt_ref, ps_ref):
    def offsets(e, acc):
        ps_ref[e] = acc
        return acc + ((cnt_ref[e] + (ROW_BLOCK - 1)) // ROW_BLOCK) * ROW_BLOCK

    return lax.fori_loop(0, N_EXPERTS, offsets, jnp.int32(ROW_BLOCK))


def _positions_kernel(cnt_ref, pk_ref, pos_ref, ps_ref):
    _expert_offsets(cnt_ref, ps_ref)
    pk = pk_ref[...]
    eidx = pk >> 16
    pos = pk & 0xFFFF
    for e in range(N_EXPERTS):
        pos = pos + jnp.where(eidx == e, ps_ref[e], 0)
    pos_ref[...] = pos


def _positions(counts, packed):
    return pl.pallas_call(
        _positions_kernel,
        in_specs=[pl.BlockSpec(memory_space=pltpu.SMEM), pl.BlockSpec(memory_space=pltpu.VMEM)],
        out_specs=pl.BlockSpec(memory_space=pltpu.VMEM),
        out_shape=jax.ShapeDtypeStruct((TOP_K, N_TOK), I32),
        scratch_shapes=[pltpu.SMEM((N_EXPERTS,), I32)],
        name="positions",
    )(counts, packed)


def _finalize_kernel(pos_ref, cnt_ref, tok_ref, dst_ref, blk_ref, first_ref, nxt_ref, nused_ref,
                     srt_ref, ps_ref, *, tt):
    i = pl.program_id(0)
    lane = lax.broadcasted_iota(I32, (1, MAP_LANES), 1)

    @pl.when(i == 0)
    def _():
        total = _expert_offsets(cnt_ref, ps_ref)
        nused_ref[0] = total // ROW_BLOCK - 1
        entry = (lax.broadcasted_iota(I32, srt_ref.shape, 0) * MAP_LANES
                 + lax.broadcasted_iota(I32, srt_ref.shape, 1))
        srt_ref[...] = (N_TOK + lax.rem(entry // ROW_BLOCK + 1, 2) * ROW_BLOCK
                        + lax.rem(entry, ROW_BLOCK)) << TOK_BITS

        def defaults(b, c):
            blk_ref[b] = N_EXPERTS - 1
            first_ref[b] = 0
            nxt_ref[b] = -1
            return c

        lax.fori_loop(0, N_BLOCKS, defaults, 0)

        def per_expert(j, nxt_e):
            e = N_EXPERTS - 1 - j
            c = cnt_ref[e]
            nb = (c + (ROW_BLOCK - 1)) // ROW_BLOCK
            r0 = ps_ref[e]
            b0 = r0 // ROW_BLOCK - 1

            def blocks(jb, carry):
                blk_ref[b0 + jb] = e
                first_ref[b0 + jb] = jnp.where(jb == 0, 1, 0)
                nxt_ref[b0 + jb] = nxt_e
                return carry

            lax.fori_loop(0, nb, blocks, 0)
            return jnp.where(c > 0, e, nxt_e)

        lax.fori_loop(0, N_EXPERTS, per_expert, jnp.int32(-1))

    def per_token(tl, carry):
        t = i * tt + tl
        entry = jnp.full((1, MAP_LANES), t * ((1 << TOK_BITS) + 1), I32)
        for k in range(TOP_K):
            p = pos_ref[t * TOP_K + k]
            pltpu.store(srt_ref.at[pl.ds(p >> MAP_SHIFT, 1), :], entry + ((k * Y_ROWS) << TOK_BITS),
                        mask=lane == (p & (MAP_LANES - 1)))
        return carry

    lax.fori_loop(0, tt, per_token, 0)

    @pl.when(i == pl.num_programs(0) - 1)
    def _():
        srt = srt_ref[...]
        tok_ref[...] = (srt & ((1 << TOK_BITS) - 1)) * SLAB_ROWS
        dst_ref[...] = (srt >> TOK_BITS) * SLAB_ROWS


def _finalize(pos_flat, counts, *, tt=512):
    smem = pl.BlockSpec(memory_space=pltpu.SMEM)
    row_map = pl.BlockSpec((MAP_ROWS, MAP_LANES), lambda i: (0, 0))
    return pl.pallas_call(
        functools.partial(_finalize_kernel, tt=tt),
        grid=(N_TOK // tt,),
        in_specs=[smem, smem],
        out_specs=[row_map, row_map] + [smem] * 4,
        out_shape=[jax.ShapeDtypeStruct((MAP_ROWS, MAP_LANES), I32),
                   jax.ShapeDtypeStruct((MAP_ROWS, MAP_LANES), I32),
                   jax.ShapeDtypeStruct((N_BLOCKS,), I32),
                   jax.ShapeDtypeStruct((N_BLOCKS,), I32),
                   jax.ShapeDtypeStruct((N_BLOCKS,), I32),
                   jax.ShapeDtypeStruct((1,), I32)],
        scratch_shapes=[pltpu.VMEM((MAP_ROWS, MAP_LANES), I32), pltpu.SMEM((N_EXPERTS,), I32)],
        compiler_params=pltpu.CompilerParams(dimension_semantics=("arbitrary",)),
        name="finalize",
    )(pos_flat, counts)


BLOCK_SLAB_ROWS = ROW_BLOCK * SLAB_ROWS
BLOCKS_PER_STEP = 2
CAST_VREGS = 64


def _cast_weight(src_ref, dst_ref):
    rows, cols = src_ref.shape
    step = CAST_VREGS * SLAB_ROWS * SLAB_LANES // cols

    def body(r, carry):
        r0 = pl.multiple_of(r * step, step)
        dst_ref[pl.ds(r0, step), :] = src_ref[pl.ds(r0, step), :].astype(BF16)
        return carry

    lax.fori_loop(0, rows // step, body, 0)


def _expert_kernel(blk_ref, first_ref, nxt_ref, nused_ref,
                   xs_ref, wg_hbm, wu_hbm, wd_hbm, tok_hbm, dst_hbm, y_hbm,
                   sg_ref, su_ref, sd_ref, bg_ref, bu_ref, bd_ref,
                   xg_ref, xb_ref, ys_ref, idx_ref, wsem, ysem, isem, *, layer):
    step = pl.program_id(0)
    nused = nused_ref[0]

    def weight_copies(ee):
        return (pltpu.make_async_copy(wg_hbm.at[layer, ee], sg_ref, wsem.at[0]),
                pltpu.make_async_copy(wu_hbm.at[layer, ee], su_ref, wsem.at[1]),
                pltpu.make_async_copy(wd_hbm.at[layer, ee], sd_ref, wsem.at[2]))

    def map_rows(blk):
        return pl.ds((blk + 1) * ROWS_PER_BLOCK, ROWS_PER_BLOCK)

    def idx_rows(j):
        return pl.ds(j * ROWS_PER_BLOCK, ROWS_PER_BLOCK)

    def idx_at(j, i):
        return idx_ref[j * ROWS_PER_BLOCK + i // MAP_LANES, i % MAP_LANES]

    def index_copies(b, slot):
        return (pltpu.make_async_copy(tok_hbm.at[map_rows(b + 1)], idx_ref.at[idx_rows(slot)],
                                      isem.at[slot]),
                pltpu.make_async_copy(dst_hbm.at[map_rows(b - 1)], idx_ref.at[idx_rows(2 + slot)],
                                      isem.at[2 + slot]))

    def wait_rows(slot):
        pltpu.make_async_copy(ys_ref.at[slot], y_hbm.at[pl.ds(0, BLOCK_SLAB_ROWS)],
                              ysem.at[slot]).wait()

    def gather_block(idx_row, slot):
        for i in range(ROW_BLOCK):
            xg_ref[pl.ds(i * SLAB_ROWS, SLAB_ROWS), :] = (
                xs_ref[pl.ds(pl.multiple_of(idx_at(idx_row, i), SLAB_ROWS), SLAB_ROWS), :])
        for c in range(SLAB_ROWS):
            w = xg_ref[pl.ds(c, ROW_BLOCK, stride=SLAB_ROWS), :]
            xb_ref[slot, :, _lo_col(c):_lo_col(c) + SLAB_LANES] = _unpack_lo(w).astype(BF16)
            xb_ref[slot, :, _hi_col(c):_hi_col(c) + SLAB_LANES] = _unpack_hi(w).astype(BF16)

    @pl.when(step == 0)
    def _():
        for cp in weight_copies(blk_ref[0]):
            cp.start()
        ys_ref[...] = jnp.zeros_like(ys_ref)
        zero_copies = [
            pltpu.make_async_copy(
                ys_ref.at[0],
                y_hbm.at[pl.ds((k * Y_ROWS + N_TOK + s * ROW_BLOCK) * SLAB_ROWS, BLOCK_SLAB_ROWS)],
                ysem.at[0])
            for k in range(TOP_K) for s in range(2)]
        for cp in zero_copies:
            cp.start()
        for cp in zero_copies:
            cp.wait()
        first_rows = pltpu.make_async_copy(tok_hbm.at[map_rows(0)], idx_ref.at[idx_rows(1)], isem.at[1])
        first_rows.start()
        first_rows.wait()
        gather_block(1, 0)
        for cp in index_copies(0, 0):
            cp.start()

    def one_block(b, cur):
        prev = 1 - cur

        @pl.when(b <= nused)
        def _():
            for cp in index_copies(b, cur):
                cp.wait()

            @pl.when(b < nused)
            def _():
                for cp in index_copies(b + 1, prev):
                    cp.start()

            @pl.when(first_ref[b] == 1)
            def _():
                for cp in weight_copies(blk_ref[b]):
                    cp.wait()
                _cast_weight(sg_ref, bg_ref)
                _cast_weight(su_ref, bu_ref)
                _cast_weight(sd_ref, bd_ref)

                @pl.when(nxt_ref[b] >= 0)
                def _():
                    for cp in weight_copies(nxt_ref[b]):
                        cp.start()

            @pl.when(b > 0)
            def _():
                wait_rows(cur)

            for i in range(ROW_BLOCK):
                pltpu.make_async_copy(
                    ys_ref.at[prev, pl.ds(i * SLAB_ROWS, SLAB_ROWS)],
                    y_hbm.at[pl.ds(pl.multiple_of(idx_at(2 + cur, i), SLAB_ROWS), SLAB_ROWS)],
                    ysem.at[prev]).start()

            x = xb_ref[cur]
            hg = jnp.dot(x, bg_ref[...], preferred_element_type=F32)
            hu = jnp.dot(x, bu_ref[...], preferred_element_type=F32)
            act = (_silu(hg) * hu).astype(BF16)
            pk = _pack_rows(jnp.dot(act, bd_ref[...], preferred_element_type=F32))
            for c in range(SLAB_ROWS):
                ys_ref[cur, pl.ds(c, ROW_BLOCK, stride=SLAB_ROWS), :] = pk[c]
            gather_block(cur, prev)

            @pl.when(b == nused)
            def _():
                wait_rows(prev)

    for j in range(BLOCKS_PER_STEP):
        one_block(step * BLOCKS_PER_STEP + j, j)


def _experts(blk, first, nxt, nused, tok_rows, dst_rows, slabs, w_gate, w_up, w_down, *, layer):
    grid_spec = pltpu.PrefetchScalarGridSpec(
        num_scalar_prefetch=4,
        grid=(N_BLOCKS // BLOCKS_PER_STEP,),
        in_specs=[pl.BlockSpec((N_TOK * SLAB_ROWS, SLAB_LANES), lambda b, *_: (0, 0),
                               pipeline_mode=pl.Buffered(1))]
                 + [pl.BlockSpec(memory_space=pl.ANY)] * 5,
        out_specs=pl.BlockSpec(memory_space=pl.ANY),
        scratch_shapes=[pltpu.VMEM((D_MODEL, EXPERT_DIM), F32),
                        pltpu.VMEM((D_MODEL, EXPERT_DIM), F32),
                        pltpu.VMEM((EXPERT_DIM, D_MODEL), F32),
                        pltpu.VMEM((D_MODEL, EXPERT_DIM), BF16),
                        pltpu.VMEM((D_MODEL, EXPERT_DIM), BF16),
                        pltpu.VMEM((EXPERT_DIM, D_MODEL), BF16),
                        pltpu.VMEM((BLOCK_SLAB_ROWS, SLAB_LANES), U32),
                        pltpu.VMEM((2, ROW_BLOCK, D_MODEL), BF16),
                        pltpu.VMEM((2, BLOCK_SLAB_ROWS, SLAB_LANES), U32),
                        pltpu.SMEM((4 * ROWS_PER_BLOCK, MAP_LANES), I32),
                        pltpu.SemaphoreType.DMA((3,)),
                        pltpu.SemaphoreType.DMA((2,)),
                        pltpu.SemaphoreType.DMA((4,))],
    )
    return pl.pallas_call(
        functools.partial(_expert_kernel, layer=layer),
        grid_spec=grid_spec,
        out_shape=jax.ShapeDtypeStruct((TOP_K * Y_ROWS * SLAB_ROWS, SLAB_LANES), U32),
        compiler_params=pltpu.CompilerParams(
            dimension_semantics=("arbitrary",), vmem_limit_bytes=VMEM_LIMIT),
        name="experts",
    )(blk, first, nxt, nused, slabs, w_gate, w_up, w_down, tok_rows, dst_rows)


def _combine_kernel(x_ref, y_ref, gate_ref, sg_ref, su_ref, sd_ref, g_ref, b_ref, o_ref, r_ref, *, tm):
    x = x_ref[...]
    xb = x.astype(BF16)
    hg = jnp.dot(xb, sg_ref[...], preferred_element_type=F32)
    hu = jnp.dot(xb, su_ref[...], preferred_element_type=F32)
    shared = jnp.dot((_silu(hg) * hu).astype(BF16), sd_ref[...], preferred_element_type=F32)
    gate = gate_ref[...]
    for c in range(SLAB_ROWS):
        lo = jnp.zeros((tm, SLAB_LANES), F32)
        hi = jnp.zeros((tm, SLAB_LANES), F32)
        for k in range(TOP_K):
            w = y_ref[k, pl.ds(c, tm, stride=SLAB_ROWS), :]
            gk = gate[:, k:k + 1]
            lo = lo + gk * _unpack_lo(w)
            hi = hi + gk * _unpack_hi(w)
        r_ref[:, _lo_col(c):_lo_col(c) + SLAB_LANES] = lo
        r_ref[:, _hi_col(c):_hi_col(c) + SLAB_LANES] = hi
    z = ALPHA * x + (r_ref[...] + shared)
    o_ref[...] = _layer_norm(z, g_ref[...], b_ref[...])


def _combine(x2d, y8, gate_tk, s_gate, s_up, s_down, ln_g, ln_b, *, tm=256):
    return pl.pallas_call(
        functools.partial(_combine_kernel, tm=tm),
        grid=(N_TOK // tm,),
        in_specs=[pl.BlockSpec((tm, D_MODEL), lambda i: (i, 0)),
                  pl.BlockSpec((TOP_K, tm * SLAB_ROWS, SLAB_LANES), lambda i: (0, i, 0)),
                  pl.BlockSpec((tm, TOP_K), lambda i: (i, 0)),
                  _resident((D_MODEL, EXPERT_DIM)), _resident((D_MODEL, EXPERT_DIM)),
                  _resident((EXPERT_DIM, D_MODEL)),
                  _resident((1, D_MODEL)), _resident((1, D_MODEL))],
        out_specs=pl.BlockSpec((tm, D_MODEL), lambda i: (i, 0)),
        out_shape=jax.ShapeDtypeStruct((N_TOK, D_MODEL), F32),
        scratch_shapes=[pltpu.VMEM((tm, D_MODEL), F32)],
        compiler_params=pltpu.CompilerParams(
            dimension_semantics=("arbitrary",), vmem_limit_bytes=VMEM_LIMIT),
        name="combine",
    )(x2d, y8, gate_tk, s_gate, s_up, s_down, ln_g, ln_b)


def _moe(x_f32, slabs, logits_t, layer, router_bias, exp_w_gate, exp_w_up, exp_w_down,
         s_gate, s_up, s_down, ln_g, ln_b):
    packed, gate8, counts = _route(logits_t, router_bias[layer].reshape(N_EXPERTS, 1))
    counts = counts[:, 0]
    tok_rows, dst_rows, blk, first, nxt, nused = _finalize(
        _positions(counts, packed).T.reshape(-1), counts)
    y8 = _experts(blk, first, nxt, nused, tok_rows, dst_rows, slabs, exp_w_gate, exp_w_up, exp_w_down,
                  layer=layer)
    y8 = y8.reshape(TOP_K, Y_ROWS * SLAB_ROWS, SLAB_LANES)
    return _combine(x_f32, y8, gate8.T, s_gate[layer].astype(BF16), s_up[layer].astype(BF16),
                    s_down[layer].astype(BF16), ln_g[layer].reshape(1, D_MODEL),
                    ln_b[layer].reshape(1, D_MODEL))


W1_COLS = KV_LORA_RANK + 2 * QK_ROPE_DIM + Q_LORA_RANK
Q_HEAD_COLS = QK_NOPE_DIM + 2 * QK_ROPE_DIM
LOG2_E = 1.4426950408889634
Q_SCALE = QK_DIM ** -0.5 * LOG2_E
FLASH_CHAIN = 256


def _rope_chunk(chunk, cs):
    u = chunk * cs
    return u + pltpu.roll(u, QK_ROPE_DIM, axis=1)


def _proj_kernel(x_ref, cs_ref, w1_ref, kvn_ref, qn_ref, wuk_ref, wuv_ref, wuq_ref,
                 q_ref, kn_ref, kr_ref, v_ref, *, tm):
    xb = x_ref[...].astype(BF16)
    h = jnp.dot(xb, w1_ref[...], preferred_element_type=F32)
    cs = cs_ref[...]
    c = _rms_norm(h[:, :KV_LORA_RANK], kvn_ref[...]).astype(BF16)
    kr = _rope_chunk(h[:, KV_LORA_RANK:KV_LORA_RANK + 2 * QK_ROPE_DIM], cs)
    lane = lax.broadcasted_iota(I32, (tm, 2 * QK_ROPE_DIM), 1)
    kr_ref[...] = jnp.where(lane < QK_ROPE_DIM, kr, 0.0).astype(BF16)
    cq = _rms_norm(h[:, KV_LORA_RANK + 2 * QK_ROPE_DIM:], qn_ref[...]).astype(BF16)
    kn = jnp.dot(c, wuk_ref[...], preferred_element_type=F32)
    vt = lax.dot_general(wuv_ref[...], c, NT_DIMS, preferred_element_type=F32)
    q = jnp.dot(cq, wuq_ref[...], preferred_element_type=F32)
    for hd in range(N_HEADS):
        kn_ref[hd] = kn[:, hd * QK_NOPE_DIM:(hd + 1) * QK_NOPE_DIM].astype(BF16)
        v_ref[hd] = vt[hd * V_DIM:(hd + 1) * V_DIM, :].astype(BF16)
        q0 = hd * Q_HEAD_COLS
        q_ref[hd, :, :QK_NOPE_DIM] = (q[:, q0:q0 + QK_NOPE_DIM] * Q_SCALE).astype(BF16)
        q_ref[hd, :, QK_NOPE_DIM:] = (
            _rope_chunk(q[:, q0 + QK_NOPE_DIM:q0 + Q_HEAD_COLS], cs) * Q_SCALE).astype(BF16)


def _proj(x2d, cs, w1, kv_norm, q_norm, w_uk, w_uv, w_uq, *, tm=256):
    nj = SEQ // tm
    head_spec = lambda d: pl.BlockSpec((None, N_HEADS, tm, d), lambda b, j: (b, 0, j, 0))
    return pl.pallas_call(
        functools.partial(_proj_kernel, tm=tm),
        grid=(BATCH, nj),
        in_specs=[pl.BlockSpec((tm, D_MODEL), lambda b, j: (b * nj + j, 0)),
                  pl.BlockSpec((tm, 2 * QK_ROPE_DIM), lambda b, j: (j, 0)),
                  _resident(w1.shape), _resident((1, KV_LORA_RANK)), _resident((1, Q_LORA_RANK)),
                  _resident(w_uk.shape), _resident(w_uv.shape), _resident(w_uq.shape)],
        out_specs=[head_spec(Q_HEAD_COLS), head_spec(QK_NOPE_DIM),
                   pl.BlockSpec((None, tm, 2 * QK_ROPE_DIM), lambda b, j: (b, j, 0)),
                   pl.BlockSpec((None, N_HEADS, V_DIM, tm), lambda b, j: (b, 0, 0, j))],
        out_shape=[jax.ShapeDtypeStruct((BATCH, N_HEADS, SEQ, Q_HEAD_COLS), BF16),
                   jax.ShapeDtypeStruct((BATCH, N_HEADS, SEQ, QK_NOPE_DIM), BF16),
                   jax.ShapeDtypeStruct((BATCH, SEQ, 2 * QK_ROPE_DIM), BF16),
                   jax.ShapeDtypeStruct((BATCH, N_HEADS, V_DIM, SEQ), BF16)],
        compiler_params=pltpu.CompilerParams(
            dimension_semantics=("arbitrary", "arbitrary"), vmem_limit_bytes=VMEM_LIMIT),
        name="proj",
    )(x2d, cs, w1, kv_norm, q_norm, w_uk, w_uv, w_uq)


def _flash_kernel(q_ref, kn_ref, kr_ref, vt_ref, o_ref, sa_ref, sb_ref, *, tq):
    qi = pl.program_id(2)
    n_chain = tq // FLASH_CHAIN
    qs = [q_ref[c * FLASH_CHAIN:(c + 1) * FLASH_CHAIN, :] for c in range(n_chain)]

    def scores_into(j, s_ref):
        k0 = pl.multiple_of(j * tq, tq)
        k = jnp.concatenate([kn_ref[pl.ds(k0, tq), :], kr_ref[pl.ds(k0, tq), :]], axis=1)
        for c, qc in enumerate(qs):
            s_ref[:, c * FLASH_CHAIN:(c + 1) * FLASH_CHAIN] = (
                lax.dot_general(k, qc, NT_DIMS, preferred_element_type=F32))

    def consume(j, stats, s_ref, diagonal):
        vt = vt_ref[:, pl.ds(pl.multiple_of(j * tq, tq), tq)]
        out = []
        for c, (m, l, acc) in enumerate(stats):
            s = s_ref[:, c * FLASH_CHAIN:(c + 1) * FLASH_CHAIN]
            if diagonal:
                key = lax.broadcasted_iota(I32, (tq, FLASH_CHAIN), 0)
                qry = lax.broadcasted_iota(I32, (tq, FLASH_CHAIN), 1) + c * FLASH_CHAIN
                s = jnp.where(key <= qry, s, -jnp.inf)
            m_new = jnp.maximum(m, jnp.max(s, axis=0, keepdims=True))
            p = jnp.exp2(s - m_new)
            a = jnp.exp2(m - m_new)
            l = a * l + jnp.sum(p, axis=0, keepdims=True)
            acc = a * acc + jnp.dot(vt, p.astype(BF16), preferred_element_type=F32)
            out.append((m_new, l, acc))
        return tuple(out)

    def pair(i2, stats):
        j = 2 * i2
        scores_into(j + 1, sb_ref)
        stats = consume(j, stats, sa_ref, False)
        scores_into(j + 2, sa_ref)
        return consume(j + 1, stats, sb_ref, False)

    def finish(stats):
        for c, (_, l, acc) in enumerate(stats):
            o_ref[c * FLASH_CHAIN:(c + 1) * FLASH_CHAIN, :] = (acc / l).T.astype(BF16)

    init = tuple((jnp.full((1, FLASH_CHAIN), -jnp.inf, F32), jnp.zeros((1, FLASH_CHAIN), F32),
                  jnp.zeros((V_DIM, FLASH_CHAIN), F32)) for _ in range(n_chain))
    scores_into(0, sa_ref)
    stats = lax.fori_loop(0, qi // 2, pair, init)

    @pl.when(qi % 2 == 1)
    def _():
        scores_into(qi, sb_ref)
        finish(consume(qi, consume(qi - 1, stats, sa_ref, False), sb_ref, True))

    @pl.when(qi % 2 == 0)
    def _():
        finish(consume(qi, stats, sa_ref, True))


def _flash(q, kn, kr, v, *, tq=512):
    return pl.pallas_call(
        functools.partial(_flash_kernel, tq=tq),
        grid=(BATCH, N_HEADS, SEQ // tq),
        in_specs=[pl.BlockSpec((None, None, tq, Q_HEAD_COLS), lambda b, h, i: (b, h, i, 0)),
                  pl.BlockSpec((None, None, SEQ, QK_NOPE_DIM), lambda b, h, i: (b, h, 0, 0)),
                  pl.BlockSpec((None, SEQ, 2 * QK_ROPE_DIM), lambda b, h, i: (b, 0, 0)),
                  pl.BlockSpec((None, None, V_DIM, SEQ), lambda b, h, i: (b, h, 0, 0))],
        out_specs=pl.BlockSpec((None, tq, V_DIM), lambda b, h, i: (b, i, h)),
        out_shape=jax.ShapeDtypeStruct((BATCH, SEQ, N_HEADS * V_DIM), BF16),
        scratch_shapes=[pltpu.VMEM((tq, tq), F32), pltpu.VMEM((tq, tq), F32)],
        compiler_params=pltpu.CompilerParams(
            dimension_semantics=("arbitrary", "arbitrary", "arbitrary"), vmem_limit_bytes=VMEM_LIMIT),
        name="flash",
    )(q, kn, kr, v)


def _attn_out_kernel(a_ref, x_ref, wo_ref, g_ref, b_ref, rw_ref, xo_ref, slab_ref, lg_ref, *, tm):
    ys = [jnp.dot(a_ref[s * SUB_ROWS:(s + 1) * SUB_ROWS, :], wo_ref[...], preferred_element_type=F32)
          for s in range(tm // SUB_ROWS)]
    _ln1_epilogue(ys, x_ref, g_ref, b_ref, rw_ref, xo_ref, slab_ref, lg_ref)


def _attn_out(attn2d, x2d, w_out, ln_g, ln_b, rw, *, tm=512):
    out_specs, out_shape = _ln1_out_specs(tm)
    return pl.pallas_call(
        functools.partial(_attn_out_kernel, tm=tm),
        grid=(N_TOK // tm,),
        in_specs=[pl.BlockSpec((tm, D_MODEL), lambda i: (i, 0)),
                  pl.BlockSpec((tm, D_MODEL), lambda i: (i, 0)),
                  _resident((D_MODEL, D_MODEL)),
                  _resident((1, D_MODEL)), _resident((1, D_MODEL)),
                  _resident((2, N_EXPERTS, D_MODEL))],
        out_specs=out_specs, out_shape=out_shape,
        compiler_params=pltpu.CompilerParams(
            dimension_semantics=("arbitrary",), vmem_limit_bytes=VMEM_LIMIT),
        name="attn_out",
    )(attn2d, x2d, w_out, ln_g, ln_b, rw)


def _rope_table():
    inv_freq = 1.0 / (ROPE_THETA ** (jnp.arange(0, QK_ROPE_DIM, 2, dtype=F32) / QK_ROPE_DIM))
    ang = jnp.arange(SEQ, dtype=F32)[:, None] * inv_freq[None, :]
    cos, sin = jnp.cos(ang), jnp.sin(ang)
    return jnp.concatenate([cos, cos, -sin, sin], axis=1)


def _swap_halves(w):
    half = QK_ROPE_DIM // 2
    return jnp.concatenate([w[..., half:], w[..., :half]], axis=-1)


def kernel(x, a_w_in, a_w_grp, a_scale, a_w_out, kv_w_down, kv_norm, kv_w_uk, kv_w_uv, b_w_dq, b_q_norm, b_w_uq, b_w_out, ln1_g, ln1_b, ln2_g, ln2_b, router_w, router_bias, exp_w_gate, exp_w_up, exp_w_down, sh_w_gate, sh_w_up, sh_w_down):
    x2d = x.reshape(N_TOK, D_MODEL)
    row = lambda v: v.reshape(1, -1)
    moe_args = (router_bias, exp_w_gate, exp_w_up, exp_w_down, sh_w_gate, sh_w_up, sh_w_down, ln2_g, ln2_b)

    pooled = _pool_in(x2d, a_w_in[0].astype(BF16))
    x1, slabs, logits_t = _mix_out(pooled, x2d, a_w_grp[0].astype(BF16), row(a_scale[0]),
                                   a_w_out[0].astype(BF16), row(ln1_g[0]), row(ln1_b[0]),
                                   _split_bf16(router_w[0].T))
    x2 = _moe(x1, slabs, logits_t, 0, *moe_args)

    kr_w = kv_w_down[:, KV_LORA_RANK:]
    w1 = jnp.concatenate([kv_w_down[:, :KV_LORA_RANK], kr_w, _swap_halves(kr_w), b_w_dq[0]],
                         axis=1).astype(BF16)
    uq = b_w_uq[0]
    uq_rope = uq[:, :, QK_NOPE_DIM:]
    w_uq = jnp.concatenate([uq[:, :, :QK_NOPE_DIM], uq_rope, _swap_halves(uq_rope)], axis=2)
    w_uq = w_uq.reshape(Q_LORA_RANK, N_HEADS * Q_HEAD_COLS).astype(BF16)
    q, kn, kr, v = _proj(x2, _rope_table(), w1, row(kv_norm), row(b_q_norm[0]),
                         kv_w_uk.reshape(KV_LORA_RANK, N_HEADS * QK_NOPE_DIM).astype(BF16),
                         kv_w_uv.reshape(KV_LORA_RANK, N_HEADS * V_DIM).T.astype(BF16), w_uq)
    attn = _flash(q, kn, kr, v).reshape(N_TOK, N_HEADS * V_DIM)
    x3, slabs, logits_t = _attn_out(attn, x2, b_w_out[0].astype(BF16), row(ln1_g[1]), row(ln1_b[1]),
                                    _split_bf16(router_w[1].T))
    x4 = _moe(x3, slabs, logits_t, 1, *moe_args)
    return x4.reshape(BATCH, SEQ, D_MODEL)
```

```python
import functools

import jax
import jax.numpy as jnp
from jax import lax
from jax.experimental import pallas as pl
from jax.experimental.pallas import tpu as pltpu

D_MODEL = 2048
BATCH = 2
SEQ = 4096
DEPTH = 2
N_TOK = BATCH * SEQ
ALPHA = (2.0 * DEPTH) ** 0.25
POOL_WINDOWS = (2, 4, 8, 16)
POOL_GROUP_DIM = D_MODEL // len(POOL_WINDOWS)
POOL_HALO = 16
N_HEADS = 16
QK_NOPE_DIM = 128
QK_ROPE_DIM = 64
QK_DIM = QK_NOPE_DIM + QK_ROPE_DIM
V_DIM = 128
Q_LORA_RANK = D_MODEL // 4
KV_LORA_RANK = D_MODEL // 4
ROPE_THETA = 10000.0
N_EXPERTS = 64
TOP_K = 8
N_EXPERT_GROUPS = 8
GROUP_SIZE = N_EXPERTS // N_EXPERT_GROUPS
TOPK_GROUPS = 4
EXPERT_DIM = D_MODEL // 4
ROUTED_SCALE = 2.5
ROW_BLOCK = 256
MAP_LANES = 128
MAP_SHIFT = MAP_LANES.bit_length() - 1
ROWS_PER_BLOCK = ROW_BLOCK // MAP_LANES
assert 1 << MAP_SHIFT == MAP_LANES and ROWS_PER_BLOCK * MAP_LANES == ROW_BLOCK
LN_EPS = 1e-5
RMS_EPS = 1e-6

N_ASSIGN = N_TOK * TOP_K
N_BLOCKS = N_ASSIGN // ROW_BLOCK + N_EXPERTS
MAP_ROWS = (N_BLOCKS + 4) * ROWS_PER_BLOCK
Y_ROWS = N_TOK + 2 * ROW_BLOCK
TOK_BITS = 14
assert N_TOK <= 1 << TOK_BITS and TOP_K * Y_ROWS < 1 << (31 - TOK_BITS)
SLAB_ROWS = 8
SLAB_LANES = 128
SUB_ROWS = 128
V7X_VMEM_BYTES = 64 * 1024 * 1024
VMEM_LIMIT = V7X_VMEM_BYTES - 4 * 1024 * 1024

F32 = jnp.float32
BF16 = jnp.bfloat16
I32 = jnp.int32
U32 = jnp.uint32
NT_DIMS = (((1,), (1,)), ((), ()))


def _lo_col(c):
    return 2 * c * SLAB_LANES


def _hi_col(c):
    return (2 * c + 1) * SLAB_LANES


def _pack_rows(z):
    return [pltpu.pack_elementwise([z[:, _lo_col(c):_lo_col(c) + SLAB_LANES],
                                    z[:, _hi_col(c):_hi_col(c) + SLAB_LANES]], packed_dtype=BF16)
            for c in range(SLAB_ROWS)]


def _unpack_lo(w):
    return pltpu.unpack_elementwise(w, index=0, packed_dtype=BF16, unpacked_dtype=F32)


def _unpack_hi(w):
    return pltpu.unpack_elementwise(w, index=1, packed_dtype=BF16, unpacked_dtype=F32)


def _silu(x):
    return x * jax.nn.sigmoid(x)


def _layer_norm(z, g, b):
    mu = jnp.mean(z, axis=-1, keepdims=True)
    zc = z - mu
    var = jnp.mean(zc * zc, axis=-1, keepdims=True)
    return zc * lax.rsqrt(var + LN_EPS) * g + b


def _rms_norm(z, g):
    ms = jnp.mean(z * z, axis=-1, keepdims=True)
    return z * lax.rsqrt(ms + RMS_EPS) * g


def _resident(shape):
    nd = len(shape)
    return pl.BlockSpec(shape, lambda *_: (0,) * nd, pipeline_mode=pl.Buffered(1))


def _pool_in_kernel(x_ref, w_ref, o_ref, tail_ref, *, tm):
    j = pl.program_id(1)

    @pl.when(j == 0)
    def _():
        tail_ref[...] = jnp.zeros_like(tail_ref)

    hs = [jnp.dot(x_ref[r0:r0 + SUB_ROWS, :].astype(BF16), w_ref[...], preferred_element_type=F32)
          for r0 in range(0, tm, SUB_ROWS)]
    tail = tail_ref[...]
    for n, h in enumerate(hs):
        r0 = n * SUB_ROWS
        ext = jnp.concatenate([tail, h], axis=0)
        tail = h[SUB_ROWS - POOL_HALO:, :]
        pos = j * tm + r0 + lax.broadcasted_iota(I32, (SUB_ROWS, 1), 0)
        for g, w in enumerate(POOL_WINDOWS):
            c0, c1 = g * POOL_GROUP_DIM, (g + 1) * POOL_GROUP_DIM
            s = ext[:, c0:c1]
            sh = 1
            while sh < w:
                s = s + pltpu.roll(s, sh, axis=0)
                sh *= 2
            inv = 1.0 / jnp.minimum(pos + 1, w).astype(F32)
            o_ref[r0:r0 + SUB_ROWS, c0:c1] = (s[POOL_HALO:, :] * inv - h[:, c0:c1]).astype(BF16)
    tail_ref[...] = tail


def _pool_in(x2d, w_in, *, tm=512):
    nj = SEQ // tm
    return pl.pallas_call(
        functools.partial(_pool_in_kernel, tm=tm),
        grid=(BATCH, nj),
        in_specs=[pl.BlockSpec((tm, D_MODEL), lambda b, j: (b * nj + j, 0)),
                  _resident((D_MODEL, D_MODEL))],
        out_specs=pl.BlockSpec((tm, D_MODEL), lambda b, j: (b * nj + j, 0)),
        out_shape=jax.ShapeDtypeStruct((N_TOK, D_MODEL), BF16),
        scratch_shapes=[pltpu.VMEM((POOL_HALO, D_MODEL), F32)],
        compiler_params=pltpu.CompilerParams(
            dimension_semantics=("arbitrary", "arbitrary"), vmem_limit_bytes=VMEM_LIMIT),
        name="pool_in",
    )(x2d, w_in)


def _ln1_epilogue(ys, x_ref, g_ref, b_ref, rw_ref, xo_ref, slab_ref, lg_ref):
    nt = lambda a, b: lax.dot_general(a, b, NT_DIMS, preferred_element_type=F32)
    for s, y in enumerate(ys):
        r0 = s * SUB_ROWS
        xn = _layer_norm(ALPHA * x_ref[r0:r0 + SUB_ROWS, :] + y, g_ref[...], b_ref[...])
        xo_ref[r0:r0 + SUB_ROWS, :] = xn
        pk = _pack_rows(xn)
        for c in range(SLAB_ROWS):
            slab_ref[pl.ds(r0 * SLAB_ROWS + c, SUB_ROWS, stride=SLAB_ROWS), :] = pk[c]
        x_hi = xn.astype(BF16)
        x_lo = (xn - x_hi.astype(F32)).astype(BF16)
        lg_ref[:, r0:r0 + SUB_ROWS] = nt(rw_ref[0], x_hi) + (nt(rw_ref[0], x_lo) + nt(rw_ref[1], x_hi))


def _mix_out_kernel(p_ref, x_ref, wg_ref, sc_ref, wo_ref, g_ref, b_ref, rw_ref,
                    xo_ref, slab_ref, lg_ref, *, tm):
    ys = []
    for s in range(tm // SUB_ROWS):
        r0 = s * SUB_ROWS
        parts = []
        for g in range(len(POOL_WINDOWS)):
            c0, c1 = g * POOL_GROUP_DIM, (g + 1) * POOL_GROUP_DIM
            parts.append(jnp.dot(p_ref[r0:r0 + SUB_ROWS, c0:c1], wg_ref[g], preferred_element_type=F32))
        mixed = (jnp.concatenate(parts, axis=1) * sc_ref[...]).astype(BF16)
        ys.append(jnp.dot(mixed, wo_ref[...], preferred_element_type=F32))
    _ln1_epilogue(ys, x_ref, g_ref, b_ref, rw_ref, xo_ref, slab_ref, lg_ref)


def _ln1_out_specs(tm):
    return (
        [pl.BlockSpec((tm, D_MODEL), lambda i: (i, 0)),
         pl.BlockSpec((tm * SLAB_ROWS, SLAB_LANES), lambda i: (i, 0)),
         pl.BlockSpec((N_EXPERTS, tm), lambda i: (0, i))],
        [jax.ShapeDtypeStruct((N_TOK, D_MODEL), F32),
         jax.ShapeDtypeStruct((N_TOK * SLAB_ROWS, SLAB_LANES), U32),
         jax.ShapeDtypeStruct((N_EXPERTS, N_TOK), F32)],
    )


def _split_bf16(w):
    hi = w.astype(BF16)
    return jnp.stack([hi, (w - hi.astype(F32)).astype(BF16)])


def _mix_out(pooled, x2d, w_grp, scale, w_out, ln_g, ln_b, rw, *, tm=512):
    out_specs, out_shape = _ln1_out_specs(tm)
    return pl.pallas_call(
        functools.partial(_mix_out_kernel, tm=tm),
        grid=(N_TOK // tm,),
        in_specs=[pl.BlockSpec((tm, D_MODEL), lambda i: (i, 0)),
                  pl.BlockSpec((tm, D_MODEL), lambda i: (i, 0)),
                  _resident(w_grp.shape), _resident((1, D_MODEL)), _resident((D_MODEL, D_MODEL)),
                  _resident((1, D_MODEL)), _resident((1, D_MODEL)),
                  _resident((2, N_EXPERTS, D_MODEL))],
        out_specs=out_specs, out_shape=out_shape,
        compiler_params=pltpu.CompilerParams(
            dimension_semantics=("arbitrary",), vmem_limit_bytes=VMEM_LIMIT),
        name="mix_out",
    )(pooled, x2d, w_grp, scale, w_out, ln_g, ln_b, rw)


def _first_index(hit_src, best, iota, n):
    return jnp.min(jnp.where(hit_src == best, iota, n), axis=0, keepdims=True)


def _route_kernel(lg_ref, bias_ref, pk_ref, gate_ref, cnt_ref, carry_ref, *, tm):
    i = pl.program_id(0)

    @pl.when(i == 0)
    def _():
        carry_ref[...] = jnp.zeros_like(carry_ref)

    neg = -jnp.inf
    scores = jax.nn.sigmoid(lg_ref[...])
    choice = scores + bias_ref[...]
    iota_g = lax.broadcasted_iota(I32, (GROUP_SIZE, tm), 0)
    gscore = []
    for g in range(N_EXPERT_GROUPS):
        c = choice[g * GROUP_SIZE:(g + 1) * GROUP_SIZE, :]
        m1 = jnp.max(c, axis=0, keepdims=True)
        f1 = _first_index(c, m1, iota_g, GROUP_SIZE)
        m2 = jnp.max(jnp.where(iota_g == f1, neg, c), axis=0, keepdims=True)
        gscore.append(m1 + m2)
    gs = jnp.concatenate(gscore, axis=0)
    iota_ng = lax.broadcasted_iota(I32, (N_EXPERT_GROUPS, tm), 0)
    gsel = jnp.zeros((N_EXPERT_GROUPS, tm), jnp.bool_)
    for _ in range(TOPK_GROUPS):
        m = jnp.max(gs, axis=0, keepdims=True)
        hit = iota_ng == _first_index(gs, m, iota_ng, N_EXPERT_GROUPS)
        gsel = gsel | hit
        gs = jnp.where(hit, neg, gs)
    masked = jnp.concatenate(
        [jnp.where(gsel[g:g + 1, :], choice[g * GROUP_SIZE:(g + 1) * GROUP_SIZE, :], neg)
         for g in range(N_EXPERT_GROUPS)], axis=0)
    iota_e = lax.broadcasted_iota(I32, (N_EXPERTS, tm), 0)
    hits, eidx, gates = [], [], []
    sel = jnp.zeros((N_EXPERTS, tm), jnp.bool_)
    for _ in range(TOP_K):
        m = jnp.max(masked, axis=0, keepdims=True)
        f = _first_index(masked, m, iota_e, N_EXPERTS)
        hit = iota_e == f
        hits.append(hit)
        eidx.append(f)
        gates.append(jnp.sum(jnp.where(hit, scores, 0.0), axis=0, keepdims=True))
        masked = jnp.where(hit, neg, masked)
        sel = sel | hit
    gate = jnp.concatenate(gates, axis=0)
    gate_ref[...] = gate / jnp.sum(gate, axis=0, keepdims=True) * ROUTED_SCALE
    selb = jnp.where(sel, 1.0, 0.0).astype(BF16)
    before = (lax.broadcasted_iota(I32, (tm, tm), 0) < lax.broadcasted_iota(I32, (tm, tm), 1))
    rank = jnp.dot(selb, jnp.where(before, 1.0, 0.0).astype(BF16), preferred_element_type=F32)
    rank = (rank + carry_ref[...]).astype(I32)
    rk = [jnp.sum(jnp.where(h, rank, 0), axis=0, keepdims=True) for h in hits]
    pk_ref[...] = (jnp.concatenate(eidx, axis=0) << 16) | jnp.concatenate(rk, axis=0)
    carry_ref[...] = carry_ref[...] + jnp.sum(selb.astype(F32), axis=1, keepdims=True)
    cnt_ref[...] = jnp.broadcast_to(carry_ref[...], cnt_ref.shape).astype(I32)


def _route(logits_t, bias, *, tm=512):
    return pl.pallas_call(
        functools.partial(_route_kernel, tm=tm),
        grid=(N_TOK // tm,),
        in_specs=[pl.BlockSpec((N_EXPERTS, tm), lambda i: (0, i)),
                  _resident((N_EXPERTS, 1))],
        out_specs=[pl.BlockSpec((TOP_K, tm), lambda i: (0, i)),
                   pl.BlockSpec((TOP_K, tm), lambda i: (0, i)),
                   pl.BlockSpec((N_EXPERTS, SLAB_LANES), lambda i: (0, 0))],
        out_shape=[jax.ShapeDtypeStruct((TOP_K, N_TOK), I32),
                   jax.ShapeDtypeStruct((TOP_K, N_TOK), F32),
                   jax.ShapeDtypeStruct((N_EXPERTS, SLAB_LANES), I32)],
        scratch_shapes=[pltpu.VMEM((N_EXPERTS, 1), F32)],
        compiler_params=pltpu.CompilerParams(dimension_semantics=("arbitrary",)),
        name="route",
    )(logits_t, bias)


def _expert_offsets(cnt_ref, ps_ref):
    def offsets(e, acc):
        ps_ref[e] = acc
        return acc + ((cnt_ref[e] + (ROW_BLOCK - 1)) // ROW_BLOCK) * ROW_BLOCK

    return lax.fori_loop(0, N_EXPERTS, offsets, jnp.int32(ROW_BLOCK))


def _positions_kernel(cnt_ref, pk_ref, pos_ref, ps_ref):
    _expert_offsets(cnt_ref, ps_ref)
    pk = pk_ref[...]
    eidx = pk >> 16
    pos = pk & 0xFFFF
    for e in range(N_EXPERTS):
        pos = pos + jnp.where(eidx == e, ps_ref[e], 0)
    pos_ref[...] = pos


def _positions(counts, packed):
    return pl.pallas_call(
        _positions_kernel,
        in_specs=[pl.BlockSpec(memory_space=pltpu.SMEM), pl.BlockSpec(memory_space=pltpu.VMEM)],
        out_specs=pl.BlockSpec(memory_space=pltpu.VMEM),
        out_shape=jax.ShapeDtypeStruct((TOP_K, N_TOK), I32),
        scratch_shapes=[pltpu.SMEM((N_EXPERTS,), I32)],
        name="positions",
    )(counts, packed)


def _finalize_kernel(pos_ref, cnt_ref, tok_ref, dst_ref, blk_ref, first_ref, nxt_ref, nused_ref,
                     srt_ref, ps_ref, *, tt):
    i = pl.program_id(0)
    lane = lax.broadcasted_iota(I32, (1, MAP_LANES), 1)

    @pl.when(i == 0)
    def _():
        total = _expert_offsets(cnt_ref, ps_ref)
        nused_ref[0] = total // ROW_BLOCK - 1
        entry = (lax.broadcasted_iota(I32, srt_ref.shape, 0) * MAP_LANES
                 + lax.broadcasted_iota(I32, srt_ref.shape, 1))
        srt_ref[...] = (N_TOK + lax.rem(entry // ROW_BLOCK + 1, 2) * ROW_BLOCK
                        + lax.rem(entry, ROW_BLOCK)) << TOK_BITS

        def defaults(b, c):
            blk_ref[b] = N_EXPERTS - 1
            first_ref[b] = 0
            nxt_ref[b] = -1
            return c

        lax.fori_loop(0, N_BLOCKS, defaults, 0)

        def per_expert(j, nxt_e):
            e = N_EXPERTS - 1 - j
            c = cnt_ref[e]
            nb = (c + (ROW_BLOCK - 1)) // ROW_BLOCK
            r0 = ps_ref[e]
            b0 = r0 // ROW_BLOCK - 1

            def blocks(jb, carry):
                blk_ref[b0 + jb] = e
                first_ref[b0 + jb] = jnp.where(jb == 0, 1, 0)
                nxt_ref[b0 + jb] = nxt_e
                return carry

            lax.fori_loop(0, nb, blocks, 0)
            return jnp.where(c > 0, e, nxt_e)

        lax.fori_loop(0, N_EXPERTS, per_expert, jnp.int32(-1))

    def per_token(tl, carry):
        t = i * tt + tl
        entry = jnp.full((1, MAP_LANES), t * ((1 << TOK_BITS) + 1), I32)
        for k in range(TOP_K):
            p = pos_ref[t * TOP_K + k]
            pltpu.store(srt_ref.at[pl.ds(p >> MAP_SHIFT, 1), :], entry + ((k * Y_ROWS) << TOK_BITS),
                        mask=lane == (p & (MAP_LANES - 1)))
        return carry

    lax.fori_loop(0, tt, per_token, 0)

    @pl.when(i == pl.num_programs(0) - 1)
    def _():
        srt = srt_ref[...]
        tok_ref[...] = (srt & ((1 << TOK_BITS) - 1)) * SLAB_ROWS
        dst_ref[...] = (srt >> TOK_BITS) * SLAB_ROWS


def _finalize(pos_flat, counts, *, tt=512):
    smem = pl.BlockSpec(memory_space=pltpu.SMEM)
    row_map = pl.BlockSpec((MAP_ROWS, MAP_LANES), lambda i: (0, 0))
    return pl.pallas_call(
        functools.partial(_finalize_kernel, tt=tt),
        grid=(N_TOK // tt,),
        in_specs=[smem, smem],
        out_specs=[row_map, row_map] + [smem] * 4,
        out_shape=[jax.ShapeDtypeStruct((MAP_ROWS, MAP_LANES), I32),
                   jax.ShapeDtypeStruct((MAP_ROWS, MAP_LANES), I32),
                   jax.ShapeDtypeStruct((N_BLOCKS,), I32),
                   jax.ShapeDtypeStruct((N_BLOCKS,), I32),
                   jax.ShapeDtypeStruct((N_BLOCKS,), I32),
                   jax.ShapeDtypeStruct((1,), I32)],
        scratch_shapes=[pltpu.VMEM((MAP_ROWS, MAP_LANES), I32), pltpu.SMEM((N_EXPERTS,), I32)],
        compiler_params=pltpu.CompilerParams(dimension_semantics=("arbitrary",)),
        name="finalize",
    )(pos_flat, counts)


BLOCK_SLAB_ROWS = ROW_BLOCK * SLAB_ROWS
BLOCKS_PER_STEP = 2
CAST_VREGS = 64


def _cast_weight(src_ref, dst_ref):
    rows, cols = src_ref.shape
    step = CAST_VREGS * SLAB_ROWS * SLAB_LANES // cols

    def body(r, carry):
        r0 = pl.multiple_of(r * step, step)
        dst_ref[pl.ds(r0, step), :] = src_ref[pl.ds(r0, step), :].astype(BF16)
        return carry

    lax.fori_loop(0, rows // step, body, 0)


def _expert_kernel(blk_ref, first_ref, nxt_ref, nused_ref,
                   xs_ref, wg_hbm, wu_hbm, wd_hbm, tok_hbm, dst_hbm, y_hbm,
                   sg_ref, su_ref, sd_ref, bg_ref, bu_ref, bd_ref,
                   xg_ref, xb_ref, ys_ref, idx_ref, wsem, ysem, isem, *, layer):
    step = pl.program_id(0)
    nused = nused_ref[0]

    def weight_copies(ee):
        return (pltpu.make_async_copy(wg_hbm.at[layer, ee], sg_ref, wsem.at[0]),
                pltpu.make_async_copy(wu_hbm.at[layer, ee], su_ref, wsem.at[1]),
                pltpu.make_async_copy(wd_hbm.at[layer, ee], sd_ref, wsem.at[2]))

    def map_rows(blk):
        return pl.ds((blk + 1) * ROWS_PER_BLOCK, ROWS_PER_BLOCK)

    def idx_rows(j):
        return pl.ds(j * ROWS_PER_BLOCK, ROWS_PER_BLOCK)

    def idx_at(j, i):
        return idx_ref[j * ROWS_PER_BLOCK + i // MAP_LANES, i % MAP_LANES]

    def index_copies(b, slot):
        return (pltpu.make_async_copy(tok_hbm.at[map_rows(b + 1)], idx_ref.at[idx_rows(slot)],
                                      isem.at[slot]),
                pltpu.make_async_copy(dst_hbm.at[map_rows(b - 1)], idx_ref.at[idx_rows(2 + slot)],
                                      isem.at[2 + slot]))

    def wait_rows(slot):
        pltpu.make_async_copy(ys_ref.at[slot], y_hbm.at[pl.ds(0, BLOCK_SLAB_ROWS)],
                              ysem.at[slot]).wait()

    def gather_block(idx_row, slot):
        for i in range(ROW_BLOCK):
            xg_ref[pl.ds(i * SLAB_ROWS, SLAB_ROWS), :] = (
                xs_ref[pl.ds(pl.multiple_of(idx_at(idx_row, i), SLAB_ROWS), SLAB_ROWS), :])
        for c in range(SLAB_ROWS):
            w = xg_ref[pl.ds(c, ROW_BLOCK, stride=SLAB_ROWS), :]
            xb_ref[slot, :, _lo_col(c):_lo_col(c) + SLAB_LANES] = _unpack_lo(w).astype(BF16)
            xb_ref[slot, :, _hi_col(c):_hi_col(c) + SLAB_LANES] = _unpack_hi(w).astype(BF16)

    @pl.when(step == 0)
    def _():
        for cp in weight_copies(blk_ref[0]):
            cp.start()
        ys_ref[...] = jnp.zeros_like(ys_ref)
        zero_copies = [
            pltpu.make_async_copy(
                ys_ref.at[0],
                y_hbm.at[pl.ds((k * Y_ROWS + N_TOK + s * ROW_BLOCK) * SLAB_ROWS, BLOCK_SLAB_ROWS)],
                ysem.at[0])
            for k in range(TOP_K) for s in range(2)]
        for cp in zero_copies:
            cp.start()
        for cp in zero_copies:
            cp.wait()
        first_rows = pltpu.make_async_copy(tok_hbm.at[map_rows(0)], idx_ref.at[idx_rows(1)], isem.at[1])
        first_rows.start()
        first_rows.wait()
        gather_block(1, 0)
        for cp in index_copies(0, 0):
            cp.start()

    def one_block(b, cur):
        prev = 1 - cur

        @pl.when(b <= nused)
        def _():
            for cp in index_copies(b, cur):
                cp.wait()

            @pl.when(b < nused)
            def _():
                for cp in index_copies(b + 1, prev):
                    cp.start()

            @pl.when(first_ref[b] == 1)
            def _():
                for cp in weight_copies(blk_ref[b]):
                    cp.wait()
                _cast_weight(sg_ref, bg_ref)
                _cast_weight(su_ref, bu_ref)
                _cast_weight(sd_ref, bd_ref)

                @pl.when(nxt_ref[b] >= 0)
                def _():
                    for cp in weight_copies(nxt_ref[b]):
                        cp.start()

            @pl.when(b > 0)
            def _():
                wait_rows(cur)

            for i in range(ROW_BLOCK):
                pltpu.make_async_copy(
                    ys_ref.at[prev, pl.ds(i * SLAB_ROWS, SLAB_ROWS)],
                    y_hbm.at[pl.ds(pl.multiple_of(idx_at(2 + cur, i), SLAB_ROWS), SLAB_ROWS)],
                    ysem.at[prev]).start()

            x = xb_ref[cur]
            hg = jnp.dot(x, bg_ref[...], preferred_element_type=F32)
            hu = jnp.dot(x, bu_ref[...], preferred_element_type=F32)
            act = (_silu(hg) * hu).astype(BF16)
            pk = _pack_rows(jnp.dot(act, bd_ref[...], preferred_element_type=F32))
            for c in range(SLAB_ROWS):
                ys_ref[cur, pl.ds(c, ROW_BLOCK, stride=SLAB_ROWS), :] = pk[c]
            gather_block(cur, prev)

            @pl.when(b == nused)
            def _():
                wait_rows(prev)

    for j in range(BLOCKS_PER_STEP):
        one_block(step * BLOCKS_PER_STEP + j, j)


def _experts(blk, first, nxt, nused, tok_rows, dst_rows, slabs, w_gate, w_up, w_down, *, layer):
    grid_spec = pltpu.PrefetchScalarGridSpec(
        num_scalar_prefetch=4,
        grid=(N_BLOCKS // BLOCKS_PER_STEP,),
        in_specs=[pl.BlockSpec((N_TOK * SLAB_ROWS, SLAB_LANES), lambda b, *_: (0, 0),
                               pipeline_mode=pl.Buffered(1))]
                 + [pl.BlockSpec(memory_space=pl.ANY)] * 5,
        out_specs=pl.BlockSpec(memory_space=pl.ANY),
        scratch_shapes=[pltpu.VMEM((D_MODEL, EXPERT_DIM), F32),
                        pltpu.VMEM((D_MODEL, EXPERT_DIM), F32),
                        pltpu.VMEM((EXPERT_DIM, D_MODEL), F32),
                        pltpu.VMEM((D_MODEL, EXPERT_DIM), BF16),
                        pltpu.VMEM((D_MODEL, EXPERT_DIM), BF16),
                        pltpu.VMEM((EXPERT_DIM, D_MODEL), BF16),
                        pltpu.VMEM((BLOCK_SLAB_ROWS, SLAB_LANES), U32),
                        pltpu.VMEM((2, ROW_BLOCK, D_MODEL), BF16),
                        pltpu.VMEM((2, BLOCK_SLAB_ROWS, SLAB_LANES), U32),
                        pltpu.SMEM((4 * ROWS_PER_BLOCK, MAP_LANES), I32),
                        pltpu.SemaphoreType.DMA((3,)),
                        pltpu.SemaphoreType.DMA((2,)),
                        pltpu.SemaphoreType.DMA((4,))],
    )
    return pl.pallas_call(
        functools.partial(_expert_kernel, layer=layer),
        grid_spec=grid_spec,
        out_shape=jax.ShapeDtypeStruct((TOP_K * Y_ROWS * SLAB_ROWS, SLAB_LANES), U32),
        compiler_params=pltpu.CompilerParams(
            dimension_semantics=("arbitrary",), vmem_limit_bytes=VMEM_LIMIT),
        name="experts",
    )(blk, first, nxt, nused, slabs, w_gate, w_up, w_down, tok_rows, dst_rows)


def _combine_kernel(x_ref, y_ref, gate_ref, sg_ref, su_ref, sd_ref, g_ref, b_ref, o_ref, r_ref, *, tm):
    x = x_ref[...]
    xb = x.astype(BF16)
    hg = jnp.dot(xb, sg_ref[...], preferred_element_type=F32)
    hu = jnp.dot(xb, su_ref[...], preferred_element_type=F32)
    shared = jnp.dot((_silu(hg) * hu).astype(BF16), sd_ref[...], preferred_element_type=F32)
    gate = gate_ref[...]
    gates = []
    for k in range(TOP_K):
        gk = jnp.broadcast_to(gate[:, k:k + 1], (tm, SLAB_LANES))
        gates.append(pltpu.bitcast(pltpu.pack_elementwise([gk, gk], packed_dtype=BF16), BF16))
    for c in range(SLAB_ROWS):
        lo = jnp.zeros((tm, SLAB_LANES), F32)
        hi = jnp.zeros((tm, SLAB_LANES), F32)
        for k in range(0, TOP_K, 2):
            z0 = pltpu.bitcast(y_ref[k, pl.ds(c, tm, stride=SLAB_ROWS), :], BF16)
            z1 = pltpu.bitcast(y_ref[k + 1, pl.ds(c, tm, stride=SLAB_ROWS), :], BF16)
            w = pltpu.bitcast(z0 * gates[k] + z1 * gates[k + 1], U32)
            lo = lo + _unpack_lo(w)
            hi = hi + _unpack_hi(w)
        r_ref[:, _lo_col(c):_lo_col(c) + SLAB_LANES] = lo
        r_ref[:, _hi_col(c):_hi_col(c) + SLAB_LANES] = hi
    z = ALPHA * x + (r_ref[...] + shared)
    o_ref[...] = _layer_norm(z, g_ref[...], b_ref[...])


def _combine(x2d, y8, gate_tk, s_gate, s_up, s_down, ln_g, ln_b, *, tm=256):
    return pl.pallas_call(
        functools.partial(_combine_kernel, tm=tm),
        grid=(N_TOK // tm,),
        in_specs=[pl.BlockSpec((tm, D_MODEL), lambda i: (i, 0)),
                  pl.BlockSpec((TOP_K, tm * SLAB_ROWS, SLAB_LANES), lambda i: (0, i, 0)),
                  pl.BlockSpec((tm, TOP_K), lambda i: (i, 0)),
                  _resident((D_MODEL, EXPERT_DIM)), _resident((D_MODEL, EXPERT_DIM)),
                  _resident((EXPERT_DIM, D_MODEL)),
                  _resident((1, D_MODEL)), _resident((1, D_MODEL))],
        out_specs=pl.BlockSpec((tm, D_MODEL), lambda i: (i, 0)),
        out_shape=jax.ShapeDtypeStruct((N_TOK, D_MODEL), F32),
        scratch_shapes=[pltpu.VMEM((tm, D_MODEL), F32)],
        compiler_params=pltpu.CompilerParams(
            dimension_semantics=("arbitrary",), vmem_limit_bytes=VMEM_LIMIT),
        name="combine",
    )(x2d, y8, gate_tk, s_gate, s_up, s_down, ln_g, ln_b)


def _moe(x_f32, slabs, logits_t, layer, router_bias, exp_w_gate, exp_w_up, exp_w_down,
         s_gate, s_up, s_down, ln_g, ln_b):
    packed, gate8, counts = _route(logits_t, router_bias[layer].reshape(N_EXPERTS, 1))
    counts = counts[:, 0]
    tok_rows, dst_rows, blk, first, nxt, nused = _finalize(
        _positions(counts, packed).T.reshape(-1), counts)
    y8 = _experts(blk, first, nxt, nused, tok_rows, dst_rows, slabs, exp_w_gate, exp_w_up, exp_w_down,
                  layer=layer)
    y8 = y8.reshape(TOP_K, Y_ROWS * SLAB_ROWS, SLAB_LANES)
    return _combine(x_f32, y8, gate8.T, s_gate[layer].astype(BF16), s_up[layer].astype(BF16),
                    s_down[layer].astype(BF16), ln_g[layer].reshape(1, D_MODEL),
                    ln_b[layer].reshape(1, D_MODEL))


W1_COLS = KV_LORA_RANK + 2 * QK_ROPE_DIM + Q_LORA_RANK
Q_HEAD_COLS = QK_NOPE_DIM + 2 * QK_ROPE_DIM
LOG2_E = 1.4426950408889634
Q_SCALE = QK_DIM ** -0.5 * LOG2_E
FLASH_CHAIN = 256


def _rope_chunk(chunk, cs):
    u = chunk * cs
    return u + pltpu.roll(u, QK_ROPE_DIM, axis=1)


def _proj_kernel(x_ref, cs_ref, w1_ref, kvn_ref, qn_ref, wuk_ref, wuv_ref, wuq_ref,
                 q_ref, kn_ref, kr_ref, v_ref, *, tm):
    xb = x_ref[...].astype(BF16)
    h = jnp.dot(xb, w1_ref[...], preferred_element_type=F32)
    cs = cs_ref[...]
    c = _rms_norm(h[:, :KV_LORA_RANK], kvn_ref[...]).astype(BF16)
    kr = _rope_chunk(h[:, KV_LORA_RANK:KV_LORA_RANK + 2 * QK_ROPE_DIM], cs)
    lane = lax.broadcasted_iota(I32, (tm, 2 * QK_ROPE_DIM), 1)
    kr_ref[...] = jnp.where(lane < QK_ROPE_DIM, kr, 0.0).astype(BF16)
    cq = _rms_norm(h[:, KV_LORA_RANK + 2 * QK_ROPE_DIM:], qn_ref[...]).astype(BF16)
    kn = jnp.dot(c, wuk_ref[...], preferred_element_type=F32)
    vt = lax.dot_general(wuv_ref[...], c, NT_DIMS, preferred_element_type=F32)
    q = jnp.dot(cq, wuq_ref[...], preferred_element_type=F32)
    for hd in range(N_HEADS):
        kn_ref[hd] = kn[:, hd * QK_NOPE_DIM:(hd + 1) * QK_NOPE_DIM].astype(BF16)
        v_ref[hd] = vt[hd * V_DIM:(hd + 1) * V_DIM, :].astype(BF16)
        q0 = hd * Q_HEAD_COLS
        q_ref[hd, :, :QK_NOPE_DIM] = (q[:, q0:q0 + QK_NOPE_DIM] * Q_SCALE).astype(BF16)
        q_ref[hd, :, QK_NOPE_DIM:] = (
            _rope_chunk(q[:, q0 + QK_NOPE_DIM:q0 + Q_HEAD_COLS], cs) * Q_SCALE).astype(BF16)


def _proj(x2d, cs, w1, kv_norm, q_norm, w_uk, w_uv, w_uq, *, tm=256):
    nj = SEQ // tm
    head_spec = lambda d: pl.BlockSpec((None, N_HEADS, tm, d), lambda b, j: (b, 0, j, 0))
    return pl.pallas_call(
        functools.partial(_proj_kernel, tm=tm),
        grid=(BATCH, nj),
        in_specs=[pl.BlockSpec((tm, D_MODEL), lambda b, j: (b * nj + j, 0)),
                  pl.BlockSpec((tm, 2 * QK_ROPE_DIM), lambda b, j: (j, 0)),
                  _resident(w1.shape), _resident((1, KV_LORA_RANK)), _resident((1, Q_LORA_RANK)),
                  _resident(w_uk.shape), _resident(w_uv.shape), _resident(w_uq.shape)],
        out_specs=[head_spec(Q_HEAD_COLS), head_spec(QK_NOPE_DIM),
                   pl.BlockSpec((None, tm, 2 * QK_ROPE_DIM), lambda b, j: (b, j, 0)),
                   pl.BlockSpec((None, N_HEADS, V_DIM, tm), lambda b, j: (b, 0, 0, j))],
        out_shape=[jax.ShapeDtypeStruct((BATCH, N_HEADS, SEQ, Q_HEAD_COLS), BF16),
                   jax.ShapeDtypeStruct((BATCH, N_HEADS, SEQ, QK_NOPE_DIM), BF16),
                   jax.ShapeDtypeStruct((BATCH, SEQ, 2 * QK_ROPE_DIM), BF16),
                   jax.ShapeDtypeStruct((BATCH, N_HEADS, V_DIM, SEQ), BF16)],
        compiler_params=pltpu.CompilerParams(
            dimension_semantics=("arbitrary", "arbitrary"), vmem_limit_bytes=VMEM_LIMIT),
        name="proj",
    )(x2d, cs, w1, kv_norm, q_norm, w_uk, w_uv, w_uq)


def _flash_kernel(q_ref, kn_ref, kr_ref, vt_ref, o_ref, sa_ref, sb_ref, *, tq):
    qi = pl.program_id(2)
    n_chain = tq // FLASH_CHAIN
    qs = [q_ref[c * FLASH_CHAIN:(c + 1) * FLASH_CHAIN, :] for c in range(n_chain)]

    def scores_into(j, s_ref):
        k0 = pl.multiple_of(j * tq, tq)
        k = jnp.concatenate([kn_ref[pl.ds(k0, tq), :], kr_ref[pl.ds(k0, tq), :]], axis=1)
        for c, qc in enumerate(qs):
            s_ref[:, c * FLASH_CHAIN:(c + 1) * FLASH_CHAIN] = (
                lax.dot_general(k, qc, NT_DIMS, preferred_element_type=F32))

    def consume(j, stats, s_ref, diagonal):
        vt = vt_ref[:, pl.ds(pl.multiple_of(j * tq, tq), tq)]
        out = []
        for c, (m, l, acc) in enumerate(stats):
            s = s_ref[:, c * FLASH_CHAIN:(c + 1) * FLASH_CHAIN]
            if diagonal:
                key = lax.broadcasted_iota(I32, (tq, FLASH_CHAIN), 0)
                qry = lax.broadcasted_iota(I32, (tq, FLASH_CHAIN), 1) + c * FLASH_CHAIN
                s = jnp.where(key <= qry, s, -jnp.inf)
            m_new = jnp.maximum(m, jnp.max(s, axis=0, keepdims=True))
            p = jnp.exp2(s - m_new)
            a = jnp.exp2(m - m_new)
            l = a * l + jnp.sum(p, axis=0, keepdims=True)
            acc = a * acc + jnp.dot(vt, p.astype(BF16), preferred_element_type=F32)
            out.append((m_new, l, acc))
        return tuple(out)

    def pair(i2, stats):
        j = 2 * i2
        scores_into(j + 1, sb_ref)
        stats = consume(j, stats, sa_ref, False)
        scores_into(j + 2, sa_ref)
        return consume(j + 1, stats, sb_ref, False)

    def finish(stats):
        for c, (_, l, acc) in enumerate(stats):
            o_ref[c * FLASH_CHAIN:(c + 1) * FLASH_CHAIN, :] = (acc / l).T.astype(BF16)

    init = tuple((jnp.full((1, FLASH_CHAIN), -jnp.inf, F32), jnp.zeros((1, FLASH_CHAIN), F32),
                  jnp.zeros((V_DIM, FLASH_CHAIN), F32)) for _ in range(n_chain))
    scores_into(0, sa_ref)
    stats = lax.fori_loop(0, qi // 2, pair, init)

    @pl.when(qi % 2 == 1)
    def _():
        scores_into(qi, sb_ref)
        finish(consume(qi, consume(qi - 1, stats, sa_ref, False), sb_ref, True))

    @pl.when(qi % 2 == 0)
    def _():
        finish(consume(qi, stats, sa_ref, True))


def _flash(q, kn, kr, v, *, tq=512):
    return pl.pallas_call(
        functools.partial(_flash_kernel, tq=tq),
        grid=(BATCH, N_HEADS, SEQ // tq),
        in_specs=[pl.BlockSpec((None, None, tq, Q_HEAD_COLS), lambda b, h, i: (b, h, i, 0)),
                  pl.BlockSpec((None, None, SEQ, QK_NOPE_DIM), lambda b, h, i: (b, h, 0, 0)),
                  pl.BlockSpec((None, SEQ, 2 * QK_ROPE_DIM), lambda b, h, i: (b, 0, 0)),
                  pl.BlockSpec((None, None, V_DIM, SEQ), lambda b, h, i: (b, h, 0, 0))],
        out_specs=pl.BlockSpec((None, tq, V_DIM), lambda b, h, i: (b, i, h)),
        out_shape=jax.ShapeDtypeStruct((BATCH, SEQ, N_HEADS * V_DIM), BF16),
        scratch_shapes=[pltpu.VMEM((tq, tq), F32), pltpu.VMEM((tq, tq), F32)],
        compiler_params=pltpu.CompilerParams(
            dimension_semantics=("arbitrary", "arbitrary", "arbitrary"), vmem_limit_bytes=VMEM_LIMIT),
        name="flash",
    )(q, kn, kr, v)


def _attn_out_kernel(a_ref, x_ref, wo_ref, g_ref, b_ref, rw_ref, xo_ref, slab_ref, lg_ref, *, tm):
    ys = [jnp.dot(a_ref[s * SUB_ROWS:(s + 1) * SUB_ROWS, :], wo_ref[...], preferred_element_type=F32)
          for s in range(tm // SUB_ROWS)]
    _ln1_epilogue(ys, x_ref, g_ref, b_ref, rw_ref, xo_ref, slab_ref, lg_ref)


def _attn_out(attn2d, x2d, w_out, ln_g, ln_b, rw, *, tm=512):
    out_specs, out_shape = _ln1_out_specs(tm)
    return pl.pallas_call(
        functools.partial(_attn_out_kernel, tm=tm),
        grid=(N_TOK // tm,),
        in_specs=[pl.BlockSpec((tm, D_MODEL), lambda i: (i, 0)),
                  pl.BlockSpec((tm, D_MODEL), lambda i: (i, 0)),
                  _resident((D_MODEL, D_MODEL)),
                  _resident((1, D_MODEL)), _resident((1, D_MODEL)),
                  _resident((2, N_EXPERTS, D_MODEL))],
        out_specs=out_specs, out_shape=out_shape,
        compiler_params=pltpu.CompilerParams(
            dimension_semantics=("arbitrary",), vmem_limit_bytes=VMEM_LIMIT),
        name="attn_out",
    )(attn2d, x2d, w_out, ln_g, ln_b, rw)


def _rope_table():
    inv_freq = 1.0 / (ROPE_THETA ** (jnp.arange(0, QK_ROPE_DIM, 2, dtype=F32) / QK_ROPE_DIM))
    ang = jnp.arange(SEQ, dtype=F32)[:, None] * inv_freq[None, :]
    cos, sin = jnp.cos(ang), jnp.sin(ang)
    return jnp.concatenate([cos, cos, -sin, sin], axis=1)


def _swap_halves(w):
    half = QK_ROPE_DIM // 2
    return jnp.concatenate([w[..., half:], w[..., :half]], axis=-1)


def kernel(x, a_w_in, a_w_grp, a_scale, a_w_out, kv_w_down, kv_norm, kv_w_uk, kv_w_uv, b_w_dq, b_q_norm, b_w_uq, b_w_out, ln1_g, ln1_b, ln2_g, ln2_b, router_w, router_bias, exp_w_gate, exp_w_up, exp_w_down, sh_w_gate, sh_w_up, sh_w_down):
    x2d = x.reshape(N_TOK, D_MODEL)
    row = lambda v: v.reshape(1, -1)
    moe_args = (router_bias, exp_w_gate, exp_w_up, exp_w_down, sh_w_gate, sh_w_up, sh_w_down, ln2_g, ln2_b)

    pooled = _pool_in(x2d, a_w_in[0].astype(BF16))
    x1, slabs, logits_t = _mix_out(pooled, x2d, a_w_grp[0].astype(BF16), row(a_scale[0]),
                                   a_w_out[0].astype(BF16), row(ln1_g[0]), row(ln1_b[0]),
                                   _split_bf16(router_w[0].T))
    x2 = _moe(x1, slabs, logits_t, 0, *moe_args)

    kr_w = kv_w_down[:, KV_LORA_RANK:]
    w1 = jnp.concatenate([kv_w_down[:, :KV_LORA_RANK], kr_w, _swap_halves(kr_w), b_w_dq[0]],
                         axis=1).astype(BF16)
    uq = b_w_uq[0]
    uq_rope = uq[:, :, QK_NOPE_DIM:]
    w_uq = jnp.concatenate([uq[:, :, :QK_NOPE_DIM], uq_rope, _swap_halves(uq_rope)], axis=2)
    w_uq = w_uq.reshape(Q_LORA_RANK, N_HEADS * Q_HEAD_COLS).astype(BF16)
    q, kn, kr, v = _proj(x2, _rope_table(), w1, row(kv_norm), row(b_q_norm[0]),
                         kv_w_uk.reshape(KV_LORA_RANK, N_HEADS * QK_NOPE_DIM).astype(BF16),
                         kv_w_uv.reshape(KV_LORA_RANK, N_HEADS * V_DIM).T.astype(BF16), w_uq)
    attn = _flash(q, kn, kr, v).reshape(N_TOK, N_HEADS * V_DIM)
    x3, slabs, logits_t = _attn_out(attn, x2, b_w_out[0].astype(BF16), row(ln1_g[1]), row(ln1_b[1]),
                                    _split_bf16(router_w[1].T))
    x4 = _moe(x3, slabs, logits_t, 1, *moe_args)
    return x4.reshape(BATCH, SEQ, D_MODEL)
```

```python
import functools

import jax
import jax.numpy as jnp
from jax import lax
from jax.experimental import pallas as pl
from jax.experimental.pallas import tpu as pltpu

D_MODEL = 2048
BATCH = 2
SEQ = 4096
DEPTH = 2
N_TOK = BATCH * SEQ
ALPHA = (2.0 * DEPTH) ** 0.25
POOL_WINDOWS = (2, 4, 8, 16)
POOL_GROUP_DIM = D_MODEL // len(POOL_WINDOWS)
POOL_HALO = 16
N_HEADS = 16
QK_NOPE_DIM = 128
QK_ROPE_DIM = 64
QK_DIM = QK_NOPE_DIM + QK_ROPE_DIM
V_DIM = 128
Q_LORA_RANK = D_MODEL // 4
KV_LORA_RANK = D_MODEL // 4
ROPE_THETA = 10000.0
N_EXPERTS = 64
TOP_K = 8
N_EXPERT_GROUPS = 8
GROUP_SIZE = N_EXPERTS // N_EXPERT_GROUPS
TOPK_GROUPS = 4
EXPERT_DIM = D_MODEL // 4
ROUTED_SCALE = 2.5
ROW_BLOCK = 256
MAP_LANES = 128
MAP_SHIFT = MAP_LANES.bit_length() - 1
ROWS_PER_BLOCK = ROW_BLOCK // MAP_LANES
assert 1 << MAP_SHIFT == MAP_LANES and ROWS_PER_BLOCK * MAP_LANES == ROW_BLOCK
LN_EPS = 1e-5
RMS_EPS = 1e-6

N_ASSIGN = N_TOK * TOP_K
N_BLOCKS = N_ASSIGN // ROW_BLOCK + N_EXPERTS
MAP_ROWS = (N_BLOCKS + 4) * ROWS_PER_BLOCK
Y_ROWS = N_TOK + 2 * ROW_BLOCK
TOK_BITS = 14
assert N_TOK <= 1 << TOK_BITS and TOP_K * Y_ROWS < 1 << (31 - TOK_BITS)
SLAB_ROWS = 8
SLAB_LANES = 128
SUB_ROWS = 128
V7X_VMEM_BYTES = 64 * 1024 * 1024
VMEM_LIMIT = V7X_VMEM_BYTES - 4 * 1024 * 1024

F32 = jnp.float32
BF16 = jnp.bfloat16
I32 = jnp.int32
U32 = jnp.uint32
NT_DIMS = (((1,), (1,)), ((), ()))


def _lo_col(c):
    return 2 * c * SLAB_LANES


def _hi_col(c):
    return (2 * c + 1) * SLAB_LANES


def _pack_rows(z):
    return [pltpu.pack_elementwise([z[:, _lo_col(c):_lo_col(c) + SLAB_LANES],
                                    z[:, _hi_col(c):_hi_col(c) + SLAB_LANES]], packed_dtype=BF16)
            for c in range(SLAB_ROWS)]


def _unpack_lo(w):
    return pltpu.unpack_elementwise(w, index=0, packed_dtype=BF16, unpacked_dtype=F32)


def _unpack_hi(w):
    return pltpu.unpack_elementwise(w, index=1, packed_dtype=BF16, unpacked_dtype=F32)


def _silu(x):
    return x * jax.nn.sigmoid(x)


def _layer_norm(z, g, b):
    mu = jnp.mean(z, axis=-1, keepdims=True)
    zc = z - mu
    var = jnp.mean(zc * zc, axis=-1, keepdims=True)
    return zc * lax.rsqrt(var + LN_EPS) * g + b


def _rms_norm(z, g):
    ms = jnp.mean(z * z, axis=-1, keepdims=True)
    return z * lax.rsqrt(ms + RMS_EPS) * g


def _resident(shape):
    nd = len(shape)
    return pl.BlockSpec(shape, lambda *_: (0,) * nd, pipeline_mode=pl.Buffered(1))


def _pool_in_kernel(x_ref, w_ref, o_ref, tail_ref, *, tm):
    j = pl.program_id(1)

    @pl.when(j == 0)
    def _():
        tail_ref[...] = jnp.zeros_like(tail_ref)

    hs = [jnp.dot(x_ref[r0:r0 + SUB_ROWS, :].astype(BF16), w_ref[...], preferred_element_type=F32)
          for r0 in range(0, tm, SUB_ROWS)]
    tail = tail_ref[...]
    for n, h in enumerate(hs):
        r0 = n * SUB_ROWS
        ext = jnp.concatenate([tail, h], axis=0)
        tail = h[SUB_ROWS - POOL_HALO:, :]
        pos = j * tm + r0 + lax.broadcasted_iota(I32, (SUB_ROWS, 1), 0)
        for g, w in enumerate(POOL_WINDOWS):
            c0, c1 = g * POOL_GROUP_DIM, (g + 1) * POOL_GROUP_DIM
            s = ext[:, c0:c1]
            sh = 1
            while sh < w:
                s = s + pltpu.roll(s, sh, axis=0)
                sh *= 2
            inv = 1.0 / jnp.minimum(pos + 1, w).astype(F32)
            o_ref[r0:r0 + SUB_ROWS, c0:c1] = (s[POOL_HALO:, :] * inv - h[:, c0:c1]).astype(BF16)
    tail_ref[...] = tail


def _pool_in(x2d, w_in, *, tm=512):
    nj = SEQ // tm
    return pl.pallas_call(
        functools.partial(_pool_in_kernel, tm=tm),
        grid=(BATCH, nj),
        in_specs=[pl.BlockSpec((tm, D_MODEL), lambda b, j: (b * nj + j, 0)),
                  _resident((D_MODEL, D_MODEL))],
        out_specs=pl.BlockSpec((tm, D_MODEL), lambda b, j: (b * nj + j, 0)),
        out_shape=jax.ShapeDtypeStruct((N_TOK, D_MODEL), BF16),
        scratch_shapes=[pltpu.VMEM((POOL_HALO, D_MODEL), F32)],
        compiler_params=pltpu.CompilerParams(
            dimension_semantics=("arbitrary", "arbitrary"), vmem_limit_bytes=VMEM_LIMIT),
        name="pool_in",
    )(x2d, w_in)


def _ln1_epilogue(ys, x_ref, g_ref, b_ref, rw_ref, xo_ref, slab_ref, lg_ref):
    nt = lambda a, b: lax.dot_general(a, b, NT_DIMS, preferred_element_type=F32)
    for s, y in enumerate(ys):
        r0 = s * SUB_ROWS
        xn = _layer_norm(ALPHA * x_ref[r0:r0 + SUB_ROWS, :] + y, g_ref[...], b_ref[...])
        xo_ref[r0:r0 + SUB_ROWS, :] = xn
        pk = _pack_rows(xn)
        for c in range(SLAB_ROWS):
            slab_ref[pl.ds(r0 * SLAB_ROWS + c, SUB_ROWS, stride=SLAB_ROWS), :] = pk[c]
        x_hi = xn.astype(BF16)
        x_lo = (xn - x_hi.astype(F32)).astype(BF16)
        lg_ref[:, r0:r0 + SUB_ROWS] = nt(rw_ref[0], x_hi) + (nt(rw_ref[0], x_lo) + nt(rw_ref[1], x_hi))


def _mix_out_kernel(p_ref, x_ref, wg_ref, sc_ref, wo_ref, g_ref, b_ref, rw_ref,
                    xo_ref, slab_ref, lg_ref, *, tm):
    ys = []
    for s in range(tm // SUB_ROWS):
        r0 = s * SUB_ROWS
        parts = []
        for g in range(len(POOL_WINDOWS)):
            c0, c1 = g * POOL_GROUP_DIM, (g + 1) * POOL_GROUP_DIM
            parts.append(jnp.dot(p_ref[r0:r0 + SUB_ROWS, c0:c1], wg_ref[g], preferred_element_type=F32))
        mixed = (jnp.concatenate(parts, axis=1) * sc_ref[...]).astype(BF16)
        ys.append(jnp.dot(mixed, wo_ref[...], preferred_element_type=F32))
    _ln1_epilogue(ys, x_ref, g_ref, b_ref, rw_ref, xo_ref, slab_ref, lg_ref)


def _ln1_out_specs(tm):
    return (
        [pl.BlockSpec((tm, D_MODEL), lambda i: (i, 0)),
         pl.BlockSpec((tm * SLAB_ROWS, SLAB_LANES), lambda i: (i, 0)),
         pl.BlockSpec((N_EXPERTS, tm), lambda i: (0, i))],
        [jax.ShapeDtypeStruct((N_TOK, D_MODEL), F32),
         jax.ShapeDtypeStruct((N_TOK * SLAB_ROWS, SLAB_LANES), U32),
         jax.ShapeDtypeStruct((N_EXPERTS, N_TOK), F32)],
    )


def _split_bf16(w):
    hi = w.astype(BF16)
    return jnp.stack([hi, (w - hi.astype(F32)).astype(BF16)])


def _mix_out(pooled, x2d, w_grp, scale, w_out, ln_g, ln_b, rw, *, tm=512):
    out_specs, out_shape = _ln1_out_specs(tm)
    return pl.pallas_call(
        functools.partial(_mix_out_kernel, tm=tm),
        grid=(N_TOK // tm,),
        in_specs=[pl.BlockSpec((tm, D_MODEL), lambda i: (i, 0)),
                  pl.BlockSpec((tm, D_MODEL), lambda i: (i, 0)),
                  _resident(w_grp.shape), _resident((1, D_MODEL)), _resident((D_MODEL, D_MODEL)),
                  _resident((1, D_MODEL)), _resident((1, D_MODEL)),
                  _resident((2, N_EXPERTS, D_MODEL))],
        out_specs=out_specs, out_shape=out_shape,
        compiler_params=pltpu.CompilerParams(
            dimension_semantics=("arbitrary",), vmem_limit_bytes=VMEM_LIMIT),
        name="mix_out",
    )(pooled, x2d, w_grp, scale, w_out, ln_g, ln_b, rw)


def _first_index(hit_src, best, iota, n):
    return jnp.min(jnp.where(hit_src == best, iota, n), axis=0, keepdims=True)


def _route_kernel(lg_ref, bias_ref, pk_ref, gate_ref, cnt_ref, carry_ref, *, tm):
    i = pl.program_id(0)

    @pl.when(i == 0)
    def _():
        carry_ref[...] = jnp.zeros_like(carry_ref)

    neg = -jnp.inf
    scores = jax.nn.sigmoid(lg_ref[...])
    choice = scores + bias_ref[...]
    iota_g = lax.broadcasted_iota(I32, (GROUP_SIZE, tm), 0)
    gscore = []
    for g in range(N_EXPERT_GROUPS):
        c = choice[g * GROUP_SIZE:(g + 1) * GROUP_SIZE, :]
        m1 = jnp.max(c, axis=0, keepdims=True)
        f1 = _first_index(c, m1, iota_g, GROUP_SIZE)
        m2 = jnp.max(jnp.where(iota_g == f1, neg, c), axis=0, keepdims=True)
        gscore.append(m1 + m2)
    gs = jnp.concatenate(gscore, axis=0)
    iota_ng = lax.broadcasted_iota(I32, (N_EXPERT_GROUPS, tm), 0)
    gsel = jnp.zeros((N_EXPERT_GROUPS, tm), jnp.bool_)
    for _ in range(TOPK_GROUPS):
        m = jnp.max(gs, axis=0, keepdims=True)
        hit = iota_ng == _first_index(gs, m, iota_ng, N_EXPERT_GROUPS)
        gsel = gsel | hit
        gs = jnp.where(hit, neg, gs)
    masked = jnp.concatenate(
        [jnp.where(gsel[g:g + 1, :], choice[g * GROUP_SIZE:(g + 1) * GROUP_SIZE, :], neg)
         for g in range(N_EXPERT_GROUPS)], axis=0)
    iota_e = lax.broadcasted_iota(I32, (N_EXPERTS, tm), 0)
    hits, eidx, gates = [], [], []
    sel = jnp.zeros((N_EXPERTS, tm), jnp.bool_)
    for _ in range(TOP_K):
        m = jnp.max(masked, axis=0, keepdims=True)
        f = _first_index(masked, m, iota_e, N_EXPERTS)
        hit = iota_e == f
        hits.append(hit)
        eidx.append(f)
        gates.append(jnp.sum(jnp.where(hit, scores, 0.0), axis=0, keepdims=True))
        masked = jnp.where(hit, neg, masked)
        sel = sel | hit
    gate = jnp.concatenate(gates, axis=0)
    gate_ref[...] = gate / jnp.sum(gate, axis=0, keepdims=True) * ROUTED_SCALE
    selb = jnp.where(sel, 1.0, 0.0).astype(BF16)
    before = (lax.broadcasted_iota(I32, (tm, tm), 0) < lax.broadcasted_iota(I32, (tm, tm), 1))
    rank = jnp.dot(selb, jnp.where(before, 1.0, 0.0).astype(BF16), preferred_element_type=F32)
    rank = (rank + carry_ref[...]).astype(I32)
    rk = [jnp.sum(jnp.where(h, rank, 0), axis=0, keepdims=True) for h in hits]
    pk_ref[...] = (jnp.concatenate(eidx, axis=0) << 16) | jnp.concatenate(rk, axis=0)
    carry_ref[...] = carry_ref[...] + jnp.sum(selb.astype(F32), axis=1, keepdims=True)
    cnt_ref[...] = jnp.broadcast_to(carry_ref[...], cnt_ref.shape).astype(I32)


def _route(logits_t, bias, *, tm=512):
    return pl.pallas_call(
        functools.partial(_route_kernel, tm=tm),
        grid=(N_TOK // tm,),
        in_specs=[pl.BlockSpec((N_EXPERTS, tm), lambda i: (0, i)),
                  _resident((N_EXPERTS, 1))],
        out_specs=[pl.BlockSpec((TOP_K, tm), lambda i: (0, i)),
                   pl.BlockSpec((TOP_K, tm), lambda i: (0, i)),
                   pl.BlockSpec((N_EXPERTS, SLAB_LANES), lambda i: (0, 0))],
        out_shape=[jax.ShapeDtypeStruct((TOP_K, N_TOK), I32),
                   jax.ShapeDtypeStruct((TOP_K, N_TOK), F32),
                   jax.ShapeDtypeStruct((N_EXPERTS, SLAB_LANES), I32)],
        scratch_shapes=[pltpu.VMEM((N_EXPERTS, 1), F32)],
        compiler_params=pltpu.CompilerParams(dimension_semantics=("arbitrary",)),
        name="route",
    )(logits_t, bias)


def _expert_offsets(cnt_ref, ps_ref):
    def offsets(e, acc):
        ps_ref[e] = acc
        return acc + ((cnt_ref[e] + (ROW_BLOCK - 1)) // ROW_BLOCK) * ROW_BLOCK

    return lax.fori_loop(0, N_EXPERTS, offsets, jnp.int32(ROW_BLOCK))


def _positions_kernel(cnt_ref, pk_ref, pos_ref, ps_ref):
    _expert_offsets(cnt_ref, ps_ref)
    pk = pk_ref[...]
    eidx = pk >> 16
    pos = pk & 0xFFFF
    for e in range(N_EXPERTS):
        pos = pos + jnp.where(eidx == e, ps_ref[e], 0)
    pos_ref[...] = pos


def _positions(counts, packed):
    return pl.pallas_call(
        _positions_kernel,
        in_specs=[pl.BlockSpec(memory_space=pltpu.SMEM), pl.BlockSpec(memory_space=pltpu.VMEM)],
        out_specs=pl.BlockSpec(memory_space=pltpu.VMEM),
        out_shape=jax.ShapeDtypeStruct((TOP_K, N_TOK), I32),
        scratch_shapes=[pltpu.SMEM((N_EXPERTS,), I32)],
        name="positions",
    )(counts, packed)


def _finalize_kernel(pos_ref, cnt_ref, tok_ref, dst_ref, blk_ref, first_ref, nxt_ref, nused_ref,
                     srt_ref, ps_ref, *, tt):
    i = pl.program_id(0)
    lane = lax.broadcasted_iota(I32, (1, MAP_LANES), 1)

    @pl.when(i == 0)
    def _():
        total = _expert_offsets(cnt_ref, ps_ref)
        nused_ref[0] = total // ROW_BLOCK - 1
        entry = (lax.broadcasted_iota(I32, srt_ref.shape, 0) * MAP_LANES
                 + lax.broadcasted_iota(I32, srt_ref.shape, 1))
        srt_ref[...] = (N_TOK + lax.rem(entry // ROW_BLOCK + 1, 2) * ROW_BLOCK
                        + lax.rem(entry, ROW_BLOCK)) << TOK_BITS

        def defaults(b, c):
            blk_ref[b] = N_EXPERTS - 1
            first_ref[b] = 0
            nxt_ref[b] = -1
            return c

        lax.fori_loop(0, N_BLOCKS, defaults, 0)

        def per_expert(j, nxt_e):
            e = N_EXPERTS - 1 - j
            c = cnt_ref[e]
            nb = (c + (ROW_BLOCK - 1)) // ROW_BLOCK
            r0 = ps_ref[e]
            b0 = r0 // ROW_BLOCK - 1

            def blocks(jb, carry):
                blk_ref[b0 + jb] = e
                first_ref[b0 + jb] = jnp.where(jb == 0, 1, 0)
                nxt_ref[b0 + jb] = nxt_e
                return carry

            lax.fori_loop(0, nb, blocks, 0)
            return jnp.where(c > 0, e, nxt_e)

        lax.fori_loop(0, N_EXPERTS, per_expert, jnp.int32(-1))

    def per_token(tl, carry):
        t = i * tt + tl
        entry = jnp.full((1, MAP_LANES), t * ((1 << TOK_BITS) + 1), I32)
        for k in range(TOP_K):
            p = pos_ref[t * TOP_K + k]
            pltpu.store(srt_ref.at[pl.ds(p >> MAP_SHIFT, 1), :], entry + ((k * Y_ROWS) << TOK_BITS),
                        mask=lane == (p & (MAP_LANES - 1)))
        return carry

    lax.fori_loop(0, tt, per_token, 0)

    @pl.when(i == pl.num_programs(0) - 1)
    def _():
        srt = srt_ref[...]
        tok_ref[...] = (srt & ((1 << TOK_BITS) - 1)) * SLAB_ROWS
        dst_ref[...] = (srt >> TOK_BITS) * SLAB_ROWS


def _finalize(pos_flat, counts, *, tt=512):
    smem = pl.BlockSpec(memory_space=pltpu.SMEM)
    row_map = pl.BlockSpec((MAP_ROWS, MAP_LANES), lambda i: (0, 0))
    return pl.pallas_call(
        functools.partial(_finalize_kernel, tt=tt),
        grid=(N_TOK // tt,),
        in_specs=[smem, smem],
        out_specs=[row_map, row_map] + [smem] * 4,
        out_shape=[jax.ShapeDtypeStruct((MAP_ROWS, MAP_LANES), I32),
                   jax.ShapeDtypeStruct((MAP_ROWS, MAP_LANES), I32),
                   jax.ShapeDtypeStruct((N_BLOCKS,), I32),
                   jax.ShapeDtypeStruct((N_BLOCKS,), I32),
                   jax.ShapeDtypeStruct((N_BLOCKS,), I32),
                   jax.ShapeDtypeStruct((1,), I32)],
        scratch_shapes=[pltpu.VMEM((MAP_ROWS, MAP_LANES), I32), pltpu.SMEM((N_EXPERTS,), I32)],
        compiler_params=pltpu.CompilerParams(dimension_semantics=("arbitrary",)),
        name="finalize",
    )(pos_flat, counts)


BLOCK_SLAB_ROWS = ROW_BLOCK * SLAB_ROWS
BLOCKS_PER_STEP = 2
CAST_VREGS = 64


def _cast_weight(src_ref, dst_ref):
    rows, cols = src_ref.shape
    step = CAST_VREGS * SLAB_ROWS * SLAB_LANES // cols

    def body(r, carry):
        r0 = pl.multiple_of(r * step, step)
        dst_ref[pl.ds(r0, step), :] = src_ref[pl.ds(r0, step), :].astype(BF16)
        return carry

    lax.fori_loop(0, rows // step, body, 0)


def _expert_kernel(blk_ref, first_ref, nxt_ref, nused_ref,
                   xs_ref, wg_hbm, wu_hbm, wd_hbm, tok_hbm, dst_hbm, y_hbm,
                   sg_ref, su_ref, sd_ref, bg_ref, bu_ref, bd_ref,
                   xg_ref, xb_ref, ys_ref, idx_ref, wsem, ysem, isem, *, layer):
    step = pl.program_id(0)
    nused = nused_ref[0]

    def weight_copies(ee):
        return (pltpu.make_async_copy(wg_hbm.at[layer, ee], sg_ref, wsem.at[0]),
                pltpu.make_async_copy(wu_hbm.at[layer, ee], su_ref, wsem.at[1]),
                pltpu.make_async_copy(wd_hbm.at[layer, ee], sd_ref, wsem.at[2]))

    def map_rows(blk):
        return pl.ds((blk + 1) * ROWS_PER_BLOCK, ROWS_PER_BLOCK)

    def idx_rows(j):
        return pl.ds(j * ROWS_PER_BLOCK, ROWS_PER_BLOCK)

    def idx_at(j, i):
        return idx_ref[j * ROWS_PER_BLOCK + i // MAP_LANES, i % MAP_LANES]

    def index_copies(b, slot):
        return (pltpu.make_async_copy(tok_hbm.at[map_rows(b + 1)], idx_ref.at[idx_rows(slot)],
                                      isem.at[slot]),
                pltpu.make_async_copy(dst_hbm.at[map_rows(b - 1)], idx_ref.at[idx_rows(2 + slot)],
                                      isem.at[2 + slot]))

    def wait_rows(slot):
        pltpu.make_async_copy(ys_ref.at[slot], y_hbm.at[pl.ds(0, BLOCK_SLAB_ROWS)],
                              ysem.at[slot]).wait()

    def gather_block(idx_row, slot):
        for i in range(ROW_BLOCK):
            xg_ref[pl.ds(i * SLAB_ROWS, SLAB_ROWS), :] = (
                xs_ref[pl.ds(pl.multiple_of(idx_at(idx_row, i), SLAB_ROWS), SLAB_ROWS), :])
        for c in range(SLAB_ROWS):
            w = xg_ref[pl.ds(c, ROW_BLOCK, stride=SLAB_ROWS), :]
            xb_ref[slot, :, _lo_col(c):_lo_col(c) + SLAB_LANES] = _unpack_lo(w).astype(BF16)
            xb_ref[slot, :, _hi_col(c):_hi_col(c) + SLAB_LANES] = _unpack_hi(w).astype(BF16)

    @pl.when(step == 0)
    def _():
        for cp in weight_copies(blk_ref[0]):
            cp.start()
        ys_ref[...] = jnp.zeros_like(ys_ref)
        zero_copies = [
            pltpu.make_async_copy(
                ys_ref.at[0],
                y_hbm.at[pl.ds((k * Y_ROWS + N_TOK + s * ROW_BLOCK) * SLAB_ROWS, BLOCK_SLAB_ROWS)],
                ysem.at[0])
            for k in range(TOP_K) for s in range(2)]
        for cp in zero_copies:
            cp.start()
        for cp in zero_copies:
            cp.wait()
        first_rows = pltpu.make_async_copy(tok_hbm.at[map_rows(0)], idx_ref.at[idx_rows(1)], isem.at[1])
        first_rows.start()
        first_rows.wait()
        gather_block(1, 0)
        for cp in index_copies(0, 0):
            cp.start()

    def one_block(b, cur):
        prev = 1 - cur

        @pl.when(b <= nused)
        def _():
            for cp in index_copies(b, cur):
                cp.wait()

            @pl.when(b < nused)
            def _():
                for cp in index_copies(b + 1, prev):
                    cp.start()

            @pl.when(first_ref[b] == 1)
            def _():
                for cp in weight_copies(blk_ref[b]):
                    cp.wait()
                _cast_weight(sg_ref, bg_ref)
                _cast_weight(su_ref, bu_ref)
                _cast_weight(sd_ref, bd_ref)

                @pl.when(nxt_ref[b] >= 0)
                def _():
                    for cp in weight_copies(nxt_ref[b]):
                        cp.start()

            @pl.when(b > 0)
            def _():
                wait_rows(cur)

            for i in range(ROW_BLOCK):
                pltpu.make_async_copy(
                    ys_ref.at[prev, pl.ds(i * SLAB_ROWS, SLAB_ROWS)],
                    y_hbm.at[pl.ds(pl.multiple_of(idx_at(2 + cur, i), SLAB_ROWS), SLAB_ROWS)],
                    ysem.at[prev]).start(priority=i % 2)

            x = xb_ref[cur]
            hg = jnp.dot(x, bg_ref[...], preferred_element_type=F32)
            hu = jnp.dot(x, bu_ref[...], preferred_element_type=F32)
            act = (_silu(hg) * hu).astype(BF16)
            pk = _pack_rows(jnp.dot(act, bd_ref[...], preferred_element_type=F32))
            for c in range(SLAB_ROWS):
                ys_ref[cur, pl.ds(c, ROW_BLOCK, stride=SLAB_ROWS), :] = pk[c]
            gather_block(cur, prev)

            @pl.when(b == nused)
            def _():
                wait_rows(prev)

    for j in range(BLOCKS_PER_STEP):
        one_block(step * BLOCKS_PER_STEP + j, j)


def _experts(blk, first, nxt, nused, tok_rows, dst_rows, slabs, w_gate, w_up, w_down, *, layer):
    grid_spec = pltpu.PrefetchScalarGridSpec(
        num_scalar_prefetch=4,
        grid=(N_BLOCKS // BLOCKS_PER_STEP,),
        in_specs=[pl.BlockSpec((N_TOK * SLAB_ROWS, SLAB_LANES), lambda b, *_: (0, 0),
                               pipeline_mode=pl.Buffered(1))]
                 + [pl.BlockSpec(memory_space=pl.ANY)] * 5,
        out_specs=pl.BlockSpec(memory_space=pl.ANY),
        scratch_shapes=[pltpu.VMEM((D_MODEL, EXPERT_DIM), F32),
                        pltpu.VMEM((D_MODEL, EXPERT_DIM), F32),
                        pltpu.VMEM((EXPERT_DIM, D_MODEL), F32),
                        pltpu.VMEM((D_MODEL, EXPERT_DIM), BF16),
                        pltpu.VMEM((D_MODEL, EXPERT_DIM), BF16),
                        pltpu.VMEM((EXPERT_DIM, D_MODEL), BF16),
                        pltpu.VMEM((BLOCK_SLAB_ROWS, SLAB_LANES), U32),
                        pltpu.VMEM((2, ROW_BLOCK, D_MODEL), BF16),
                        pltpu.VMEM((2, BLOCK_SLAB_ROWS, SLAB_LANES), U32),
                        pltpu.SMEM((4 * ROWS_PER_BLOCK, MAP_LANES), I32),
                        pltpu.SemaphoreType.DMA((3,)),
                        pltpu.SemaphoreType.DMA((2,)),
                        pltpu.SemaphoreType.DMA((4,))],
    )
    return pl.pallas_call(
        functools.partial(_expert_kernel, layer=layer),
        grid_spec=grid_spec,
        out_shape=jax.ShapeDtypeStruct((TOP_K * Y_ROWS * SLAB_ROWS, SLAB_LANES), U32),
        compiler_params=pltpu.CompilerParams(
            dimension_semantics=("arbitrary",), vmem_limit_bytes=VMEM_LIMIT),
        name="experts",
    )(blk, first, nxt, nused, slabs, w_gate, w_up, w_down, tok_rows, dst_rows)


def _combine_kernel(x_ref, y_ref, gate_ref, sg_ref, su_ref, sd_ref, g_ref, b_ref, o_ref, r_ref, *, tm):
    x = x_ref[...]
    xb = x.astype(BF16)
    hg = jnp.dot(xb, sg_ref[...], preferred_element_type=F32)
    hu = jnp.dot(xb, su_ref[...], preferred_element_type=F32)
    shared = jnp.dot((_silu(hg) * hu).astype(BF16), sd_ref[...], preferred_element_type=F32)
    gate = gate_ref[...]
    gates = []
    for k in range(TOP_K):
        gk = jnp.broadcast_to(gate[:, k:k + 1], (tm, SLAB_LANES))
        gates.append(pltpu.bitcast(pltpu.pack_elementwise([gk, gk], packed_dtype=BF16), BF16))
    for c in range(SLAB_ROWS):
        lo = jnp.zeros((tm, SLAB_LANES), F32)
        hi = jnp.zeros((tm, SLAB_LANES), F32)
        for k in range(0, TOP_K, 2):
            z0 = pltpu.bitcast(y_ref[k, pl.ds(c, tm, stride=SLAB_ROWS), :], BF16)
            z1 = pltpu.bitcast(y_ref[k + 1, pl.ds(c, tm, stride=SLAB_ROWS), :], BF16)
            w = pltpu.bitcast(z0 * gates[k] + z1 * gates[k + 1], U32)
            lo = lo + _unpack_lo(w)
            hi = hi + _unpack_hi(w)
        r_ref[:, _lo_col(c):_lo_col(c) + SLAB_LANES] = lo
        r_ref[:, _hi_col(c):_hi_col(c) + SLAB_LANES] = hi
    z = ALPHA * x + (r_ref[...] + shared)
    o_ref[...] = _layer_norm(z, g_ref[...], b_ref[...])


def _combine(x2d, y8, gate_tk, s_gate, s_up, s_down, ln_g, ln_b, *, tm=256):
    return pl.pallas_call(
        functools.partial(_combine_kernel, tm=tm),
        grid=(N_TOK // tm,),
        in_specs=[pl.BlockSpec((tm, D_MODEL), lambda i: (i, 0)),
                  pl.BlockSpec((TOP_K, tm * SLAB_ROWS, SLAB_LANES), lambda i: (0, i, 0)),
                  pl.BlockSpec((tm, TOP_K), lambda i: (i, 0)),
                  _resident((D_MODEL, EXPERT_DIM)), _resident((D_MODEL, EXPERT_DIM)),
                  _resident((EXPERT_DIM, D_MODEL)),
                  _resident((1, D_MODEL)), _resident((1, D_MODEL))],
        out_specs=pl.BlockSpec((tm, D_MODEL), lambda i: (i, 0)),
        out_shape=jax.ShapeDtypeStruct((N_TOK, D_MODEL), F32),
        scratch_shapes=[pltpu.VMEM((tm, D_MODEL), F32)],
        compiler_params=pltpu.CompilerParams(
            dimension_semantics=("arbitrary",), vmem_limit_bytes=VMEM_LIMIT),
        name="combine",
    )(x2d, y8, gate_tk, s_gate, s_up, s_down, ln_g, ln_b)


def _moe(x_f32, slabs, logits_t, layer, router_bias, exp_w_gate, exp_w_up, exp_w_down,
         s_gate, s_up, s_down, ln_g, ln_b):
    packed, gate8, counts = _route(logits_t, router_bias[layer].reshape(N_EXPERTS, 1))
    counts = counts[:, 0]
    tok_rows, dst_rows, blk, first, nxt, nused = _finalize(
        _positions(counts, packed).T.reshape(-1), counts)
    y8 = _experts(blk, first, nxt, nused, tok_rows, dst_rows, slabs, exp_w_gate, exp_w_up, exp_w_down,
                  layer=layer)
    y8 = y8.reshape(TOP_K, Y_ROWS * SLAB_ROWS, SLAB_LANES)
    return _combine(x_f32, y8, gate8.T, s_gate[layer].astype(BF16), s_up[layer].astype(BF16),
                    s_down[layer].astype(BF16), ln_g[layer].reshape(1, D_MODEL),
                    ln_b[layer].reshape(1, D_MODEL))


W1_COLS = KV_LORA_RANK + 2 * QK_ROPE_DIM + Q_LORA_RANK
Q_HEAD_COLS = QK_NOPE_DIM + 2 * QK_ROPE_DIM
LOG2_E = 1.4426950408889634
Q_SCALE = QK_DIM ** -0.5 * LOG2_E
FLASH_CHAIN = 256


def _rope_chunk(chunk, cs):
    u = chunk * cs
    return u + pltpu.roll(u, QK_ROPE_DIM, axis=1)


def _proj_kernel(x_ref, cs_ref, w1_ref, kvn_ref, qn_ref, wuk_ref, wuv_ref, wuq_ref,
                 q_ref, kn_ref, kr_ref, v_ref, *, tm):
    xb = x_ref[...].astype(BF16)
    h = jnp.dot(xb, w1_ref[...], preferred_element_type=F32)
    cs = cs_ref[...]
    c = _rms_norm(h[:, :KV_LORA_RANK], kvn_ref[...]).astype(BF16)
    kr = _rope_chunk(h[:, KV_LORA_RANK:KV_LORA_RANK + 2 * QK_ROPE_DIM], cs)
    lane = lax.broadcasted_iota(I32, (tm, 2 * QK_ROPE_DIM), 1)
    kr_ref[...] = jnp.where(lane < QK_ROPE_DIM, kr, 0.0).astype(BF16)
    cq = _rms_norm(h[:, KV_LORA_RANK + 2 * QK_ROPE_DIM:], qn_ref[...]).astype(BF16)
    kn = jnp.dot(c, wuk_ref[...], preferred_element_type=F32)
    vt = lax.dot_general(wuv_ref[...], c, NT_DIMS, preferred_element_type=F32)
    q = jnp.dot(cq, wuq_ref[...], preferred_element_type=F32)
    for hd in range(N_HEADS):
        kn_ref[hd] = kn[:, hd * QK_NOPE_DIM:(hd + 1) * QK_NOPE_DIM].astype(BF16)
        v_ref[hd] = vt[hd * V_DIM:(hd + 1) * V_DIM, :].astype(BF16)
        q0 = hd * Q_HEAD_COLS
        q_ref[hd, :, :QK_NOPE_DIM] = (q[:, q0:q0 + QK_NOPE_DIM] * Q_SCALE).astype(BF16)
        q_ref[hd, :, QK_NOPE_DIM:] = (
            _rope_chunk(q[:, q0 + QK_NOPE_DIM:q0 + Q_HEAD_COLS], cs) * Q_SCALE).astype(BF16)


def _proj(x2d, cs, w1, kv_norm, q_norm, w_uk, w_uv, w_uq, *, tm=256):
    nj = SEQ // tm
    head_spec = lambda d: pl.BlockSpec((None, N_HEADS, tm, d), lambda b, j: (b, 0, j, 0))
    return pl.pallas_call(
        functools.partial(_proj_kernel, tm=tm),
        grid=(BATCH, nj),
        in_specs=[pl.BlockSpec((tm, D_MODEL), lambda b, j: (b * nj + j, 0)),
                  pl.BlockSpec((tm, 2 * QK_ROPE_DIM), lambda b, j: (j, 0)),
                  _resident(w1.shape), _resident((1, KV_LORA_RANK)), _resident((1, Q_LORA_RANK)),
                  _resident(w_uk.shape), _resident(w_uv.shape), _resident(w_uq.shape)],
        out_specs=[head_spec(Q_HEAD_COLS), head_spec(QK_NOPE_DIM),
                   pl.BlockSpec((None, tm, 2 * QK_ROPE_DIM), lambda b, j: (b, j, 0)),
                   pl.BlockSpec((None, N_HEADS, V_DIM, tm), lambda b, j: (b, 0, 0, j))],
        out_shape=[jax.ShapeDtypeStruct((BATCH, N_HEADS, SEQ, Q_HEAD_COLS), BF16),
                   jax.ShapeDtypeStruct((BATCH, N_HEADS, SEQ, QK_NOPE_DIM), BF16),
                   jax.ShapeDtypeStruct((BATCH, SEQ, 2 * QK_ROPE_DIM), BF16),
                   jax.ShapeDtypeStruct((BATCH, N_HEADS, V_DIM, SEQ), BF16)],
        compiler_params=pltpu.CompilerParams(
            dimension_semantics=("arbitrary", "arbitrary"), vmem_limit_bytes=VMEM_LIMIT),
        name="proj",
    )(x2d, cs, w1, kv_norm, q_norm, w_uk, w_uv, w_uq)


def _flash_kernel(q_ref, kn_ref, kr_ref, vt_ref, o_ref, sa_ref, sb_ref, *, tq):
    qi = pl.program_id(2)
    n_chain = tq // FLASH_CHAIN
    qs = [q_ref[c * FLASH_CHAIN:(c + 1) * FLASH_CHAIN, :] for c in range(n_chain)]

    def scores_into(j, s_ref):
        k0 = pl.multiple_of(j * tq, tq)
        k = jnp.concatenate([kn_ref[pl.ds(k0, tq), :], kr_ref[pl.ds(k0, tq), :]], axis=1)
        for c, qc in enumerate(qs):
            s_ref[:, c * FLASH_CHAIN:(c + 1) * FLASH_CHAIN] = (
                lax.dot_general(k, qc, NT_DIMS, preferred_element_type=F32))

    def consume(j, stats, s_ref, diagonal):
        vt = vt_ref[:, pl.ds(pl.multiple_of(j * tq, tq), tq)]
        out = []
        for c, (m, l, acc) in enumerate(stats):
            s = s_ref[:, c * FLASH_CHAIN:(c + 1) * FLASH_CHAIN]
            if diagonal:
                key = lax.broadcasted_iota(I32, (tq, FLASH_CHAIN), 0)
                qry = lax.broadcasted_iota(I32, (tq, FLASH_CHAIN), 1) + c * FLASH_CHAIN
                s = jnp.where(key <= qry, s, -jnp.inf)
            m_new = jnp.maximum(m, jnp.max(s, axis=0, keepdims=True))
            p = jnp.exp2(s - m_new)
            a = jnp.exp2(m - m_new)
            l = a * l + jnp.sum(p, axis=0, keepdims=True)
            acc = a * acc + jnp.dot(vt, p.astype(BF16), preferred_element_type=F32)
            out.append((m_new, l, acc))
        return tuple(out)

    def pair(i2, stats):
        j = 2 * i2
        scores_into(j + 1, sb_ref)
        stats = consume(j, stats, sa_ref, False)
        scores_into(j + 2, sa_ref)
        return consume(j + 1, stats, sb_ref, False)

    def finish(stats):
        for c, (_, l, acc) in enumerate(stats):
            o_ref[c * FLASH_CHAIN:(c + 1) * FLASH_CHAIN, :] = (acc / l).T.astype(BF16)

    init = tuple((jnp.full((1, FLASH_CHAIN), -jnp.inf, F32), jnp.zeros((1, FLASH_CHAIN), F32),
                  jnp.zeros((V_DIM, FLASH_CHAIN), F32)) for _ in range(n_chain))
    scores_into(0, sa_ref)
    stats = lax.fori_loop(0, qi // 2, pair, init)

    @pl.when(qi % 2 == 1)
    def _():
        scores_into(qi, sb_ref)
        finish(consume(qi, consume(qi - 1, stats, sa_ref, False), sb_ref, True))

    @pl.when(qi % 2 == 0)
    def _():
        finish(consume(qi, stats, sa_ref, True))


def _flash(q, kn, kr, v, *, tq=512):
    return pl.pallas_call(
        functools.partial(_flash_kernel, tq=tq),
        grid=(BATCH, N_HEADS, SEQ // tq),
        in_specs=[pl.BlockSpec((None, None, tq, Q_HEAD_COLS), lambda b, h, i: (b, h, i, 0)),
                  pl.BlockSpec((None, None, SEQ, QK_NOPE_DIM), lambda b, h, i: (b, h, 0, 0)),
                  pl.BlockSpec((None, SEQ, 2 * QK_ROPE_DIM), lambda b, h, i: (b, 0, 0)),
                  pl.BlockSpec((None, None, V_DIM, SEQ), lambda b, h, i: (b, h, 0, 0))],
        out_specs=pl.BlockSpec((None, tq, V_DIM), lambda b, h, i: (b, i, h)),
        out_shape=jax.ShapeDtypeStruct((BATCH, SEQ, N_HEADS * V_DIM), BF16),
        scratch_shapes=[pltpu.VMEM((tq, tq), F32), pltpu.VMEM((tq, tq), F32)],
        compiler_params=pltpu.CompilerParams(
            dimension_semantics=("arbitrary", "arbitrary", "arbitrary"), vmem_limit_bytes=VMEM_LIMIT),
        name="flash",
    )(q, kn, kr, v)


def _attn_out_kernel(a_ref, x_ref, wo_ref, g_ref, b_ref, rw_ref, xo_ref, slab_ref, lg_ref, *, tm):
    ys = [jnp.dot(a_ref[s * SUB_ROWS:(s + 1) * SUB_ROWS, :], wo_ref[...], preferred_element_type=F32)
          for s in range(tm // SUB_ROWS)]
    _ln1_epilogue(ys, x_ref, g_ref, b_ref, rw_ref, xo_ref, slab_ref, lg_ref)


def _attn_out(attn2d, x2d, w_out, ln_g, ln_b, rw, *, tm=512):
    out_specs, out_shape = _ln1_out_specs(tm)
    return pl.pallas_call(
        functools.partial(_attn_out_kernel, tm=tm),
        grid=(N_TOK // tm,),
        in_specs=[pl.BlockSpec((tm, D_MODEL), lambda i: (i, 0)),
                  pl.BlockSpec((tm, D_MODEL), lambda i: (i, 0)),
                  _resident((D_MODEL, D_MODEL)),
                  _resident((1, D_MODEL)), _resident((1, D_MODEL)),
                  _resident((2, N_EXPERTS, D_MODEL))],
        out_specs=out_specs, out_shape=out_shape,
        compiler_params=pltpu.CompilerParams(
            dimension_semantics=("arbitrary",), vmem_limit_bytes=VMEM_LIMIT),
        name="attn_out",
    )(attn2d, x2d, w_out, ln_g, ln_b, rw)


def _rope_table():
    inv_freq = 1.0 / (ROPE_THETA ** (jnp.arange(0, QK_ROPE_DIM, 2, dtype=F32) / QK_ROPE_DIM))
    ang = jnp.arange(SEQ, dtype=F32)[:, None] * inv_freq[None, :]
    cos, sin = jnp.cos(ang), jnp.sin(ang)
    return jnp.concatenate([cos, cos, -sin, sin], axis=1)


def _swap_halves(w):
    half = QK_ROPE_DIM // 2
    return jnp.concatenate([w[..., half:], w[..., :half]], axis=-1)


def kernel(x, a_w_in, a_w_grp, a_scale, a_w_out, kv_w_down, kv_norm, kv_w_uk, kv_w_uv, b_w_dq, b_q_norm, b_w_uq, b_w_out, ln1_g, ln1_b, ln2_g, ln2_b, router_w, router_bias, exp_w_gate, exp_w_up, exp_w_down, sh_w_gate, sh_w_up, sh_w_down):
    x2d = x.reshape(N_TOK, D_MODEL)
    row = lambda v: v.reshape(1, -1)
    moe_args = (router_bias, exp_w_gate, exp_w_up, exp_w_down, sh_w_gate, sh_w_up, sh_w_down, ln2_g, ln2_b)

    pooled = _pool_in(x2d, a_w_in[0].astype(BF16))
    x1, slabs, logits_t = _mix_out(pooled, x2d, a_w_grp[0].astype(BF16), row(a_scale[0]),
                                   a_w_out[0].astype(BF16), row(ln1_g[0]), row(ln1_b[0]),
                                   _split_bf16(router_w[0].T))
    x2 = _moe(x1, slabs, logits_t, 0, *moe_args)

    kr_w = kv_w_down[:, KV_LORA_RANK:]
    w1 = jnp.concatenate([kv_w_down[:, :KV_LORA_RANK], kr_w, _swap_halves(kr_w), b_w_dq[0]],
                         axis=1).astype(BF16)
    uq = b_w_uq[0]
    uq_rope = uq[:, :, QK_NOPE_DIM:]
    w_uq = jnp.concatenate([uq[:, :, :QK_NOPE_DIM], uq_rope, _swap_halves(uq_rope)], axis=2)
    w_uq = w_uq.reshape(Q_LORA_RANK, N_HEADS * Q_HEAD_COLS).astype(BF16)
    q, kn, kr, v = _proj(x2, _rope_table(), w1, row(kv_norm), row(b_q_norm[0]),
                         kv_w_uk.reshape(KV_LORA_RANK, N_HEADS * QK_NOPE_DIM).astype(BF16),
                         kv_w_uv.reshape(KV_LORA_RANK, N_HEADS * V_DIM).T.astype(BF16), w_uq)
    attn = _flash(q, kn, kr, v).reshape(N_TOK, N_HEADS * V_DIM)
    x3, slabs, logits_t = _attn_out(attn, x2, b_w_out[0].astype(BF16), row(ln1_g[1]), row(ln1_b[1]),
                                    _split_bf16(router_w[1].T))
    x4 = _moe(x3, slabs, logits_t, 1, *moe_args)
    return x4.reshape(BATCH, SEQ, D_MODEL)
```
